```python
import math
import jax
import jax.numpy as jnp
from jax import lax
import numpy as np


D_MODEL = 1024
BATCH = 4
SEQ = 8192
DEPTH = 2
DEC_BATCH = 2
DEC_SEQ = 8192
PAST_LEN = 128

A_GROUPS = 4
A_WIDTH = D_MODEL // 2
A_GROUP_DIM = A_WIDTH // A_GROUPS
CHUNK = 128
B_WIDTH = D_MODEL // 2
HYENA_ORDER = 2
SHORT_CONV = 3
FILTER_BANDS = 16
FILTER_EMB = 1 + 2 * FILTER_BANDS
FILTER_HIDDEN = 64
FILTER_DIRS = 2
DECAY_TARGET = 1e-2
FAST_DECAY_PCT = 0.3
SLOW_DECAY_PCT = 1.5
EVEN_IN = 2 * A_WIDTH + (HYENA_ORDER + 1) * B_WIDTH
MIX_WIDTH = A_WIDTH + B_WIDTH
HEAD_DIM = 64
N_HEADS = D_MODEL // HEAD_DIM
N_KV_HEADS = N_HEADS // 4
GQA_GROUP = N_HEADS // N_KV_HEADS
WINDOW = 128
ATT_BLOCK = 128
QKV_WIDTH = (N_HEADS + 2 * N_KV_HEADS) * HEAD_DIM
REL_BUCKETS = 32
REL_MAX_DIST = 128
NEG_INF = -1e30
D_FF = 2816
N_EXPERTS = 8
TOP_K = 2
D_FF_EXPERT = 3584
MOE_BLOCK = 256
EPS = 1e-6

kernel_name = 'hybrid_gmlp_hyena_swa_moe_encoder'


def rms_norm(x, g):
    xf = x.astype(jnp.float32)
    y = xf * lax.rsqrt(jnp.mean(xf * xf, axis=-1, keepdims=True) + EPS)
    return (y * g.astype(jnp.float32)).astype(x.dtype)


def layer_norm(x, g, b):
    xf = x.astype(jnp.float32)
    mu = jnp.mean(xf, axis=-1, keepdims=True)
    xc = xf - mu
    var = jnp.mean(xc * xc, axis=-1, keepdims=True)
    return (xc * lax.rsqrt(var + EPS) * g.astype(jnp.float32) + b.astype(jnp.float32)).astype(x.dtype)


def chunked_sgu(u, v, ln_g, ln_b, w_s, b_s):
    B, L, _ = v.shape
    nc = L // CHUNK
    vg = v.reshape(B, nc, CHUNK, A_GROUPS, A_GROUP_DIM)
    vg = layer_norm(vg, ln_g.reshape(A_GROUPS, A_GROUP_DIM), ln_b.reshape(A_GROUPS, A_GROUP_DIM))
    mixed = jnp.einsum('hij,bcjhd->bcihd', w_s, vg) + b_s.T[None, None, :, :, None]
    return u * mixed.reshape(B, L, A_WIDTH)


def short_conv(x, w):
    L = x.shape[1]
    pad = SHORT_CONV // 2
    xp = jnp.pad(x, ((0, 0), (pad, pad), (0, 0)))
    out = xp[:, 0:L] * w[0]
    for k in range(1, SHORT_CONV):
        out = out + xp[:, k:k + L] * w[k]
    return out


def hyena_filters(L, w1, b1, w2, b2, w3, b3, freq):
    f32 = jnp.float32
    t = jnp.linspace(0.0, 1.0, L, dtype=f32)[:, None]
    w = 2.0 * math.pi * jnp.arange(L, dtype=f32)[:, None] / L
    bands = jnp.linspace(1e-4, FILTER_BANDS - 1, FILTER_BANDS, dtype=f32)[None, :]
    feats = jnp.concatenate([t, jnp.cos(bands * w), jnp.sin(bands * w)], axis=-1)
    fr = freq.astype(f32)
    h = jnp.sin(fr[0] * (feats @ w1.astype(f32) + b1.astype(f32)))
    h = jnp.sin(fr[1] * (h @ w2.astype(f32) + b2.astype(f32)))
    h = (h @ w3.astype(f32) + b3.astype(f32)).reshape(L, FILTER_DIRS, HYENA_ORDER, B_WIDTH)
    deltas = jnp.linspace(math.log(DECAY_TARGET) / SLOW_DECAY_PCT,
                          math.log(DECAY_TARGET) / FAST_DECAY_PCT, B_WIDTH, dtype=f32)
    decay = jnp.exp(-t * jnp.abs(deltas)[None, :])
    return h * decay[:, None, None, :]


def bidir_long_conv(z, h_fwd, h_bwd):
    L = z.shape[1]
    k = jnp.concatenate([h_fwd, jnp.zeros_like(h_fwd[:1]), h_bwd[1:][::-1]], axis=0)
    zf = jnp.fft.rfft(z.astype(jnp.float32), n=2 * L, axis=1)
    kf = jnp.fft.rfft(k, n=2 * L, axis=0)
    return jnp.fft.irfft(zf * kf[None], n=2 * L, axis=1)[:, :L]


def hyena(hb, conv_w, filt, skip):
    hb = short_conv(hb, conv_w)
    v, x1, x2 = jnp.split(hb, HYENA_ORDER + 1, axis=-1)
    z = v.astype(jnp.float32)
    sk = skip.astype(jnp.float32)
    for n, gate in enumerate((x1, x2)):
        z = gate.astype(jnp.float32) * (bidir_long_conv(z, filt[:, 0, n], filt[:, 1, n]) + sk[n] * z)
    return z


def even_mixer(x, w_in, w_out, sgu_ln_g, sgu_ln_b, sgu_w, sgu_b, hy_conv,
               f_w1, f_b1, f_w2, f_b2, f_w3, f_b3, f_freq, hy_skip):
    L = x.shape[1]
    proj = x @ w_in
    u = jax.nn.gelu(proj[..., :A_WIDTH])
    v = jax.nn.gelu(proj[..., A_WIDTH:2 * A_WIDTH])
    hb = proj[..., 2 * A_WIDTH:]
    ya = chunked_sgu(u, v, sgu_ln_g, sgu_ln_b, sgu_w, sgu_b)
    filt = hyena_filters(L, f_w1, f_b1, f_w2, f_b2, f_w3, f_b3, f_freq)
    yb = hyena(hb, hy_conv, filt, hy_skip).astype(x.dtype)
    return jnp.concatenate([ya, yb], axis=-1) @ w_out


def t5_bucket(rel):
    nb = REL_BUCKETS // 2
    max_exact = nb // 2
    ret = jnp.where(rel > 0, nb, 0)
    n = jnp.abs(rel)
    large = max_exact + (jnp.log(jnp.maximum(n, 1).astype(jnp.float32) / max_exact)
                         / math.log(REL_MAX_DIST / max_exact) * (nb - max_exact)).astype(jnp.int32)
    large = jnp.minimum(large, nb - 1)
    return ret + jnp.where(n < max_exact, n, large)


def window_attention(x, w_qkv, q_g, k_g, sink, w_out, rel_bias):
    B, L, _ = x.shape
    nb = L // ATT_BLOCK
    KB = 3 * ATT_BLOCK
    qkv = x @ w_qkv
    q = qkv[..., :N_HEADS * HEAD_DIM].reshape(B, L, N_HEADS, HEAD_DIM)
    k = qkv[..., N_HEADS * HEAD_DIM:(N_HEADS + N_KV_HEADS) * HEAD_DIM].reshape(B, L, N_KV_HEADS, HEAD_DIM)
    v = qkv[..., (N_HEADS + N_KV_HEADS) * HEAD_DIM:].reshape(B, L, N_KV_HEADS, HEAD_DIM)
    q = rms_norm(q, q_g)
    k = rms_norm(k, k_g)
    qb = q.reshape(B, nb, ATT_BLOCK, N_KV_HEADS, GQA_GROUP, HEAD_DIM)

    def band(t):
        tp = jnp.pad(t, ((0, 0), (ATT_BLOCK, ATT_BLOCK), (0, 0), (0, 0)))
        tp = tp.reshape(B, nb + 2, ATT_BLOCK, N_KV_HEADS, HEAD_DIM)
        return jnp.concatenate([tp[:, :-2], tp[:, 1:-1], tp[:, 2:]], axis=2)

    kb, vb = band(k), band(v)
    s = jnp.einsum('bnqhgd,bnkhd->bnhgqk', qb, kb, preferred_element_type=jnp.float32) * (HEAD_DIM ** -0.5)
    rel = jnp.arange(KB)[None, :] - ATT_BLOCK - jnp.arange(ATT_BLOCK)[:, None]
    bias = rel_bias.astype(jnp.float32)[t5_bucket(rel)]
    bias = bias.transpose(2, 0, 1).reshape(N_KV_HEADS, GQA_GROUP, ATT_BLOCK, KB)
    kpos = (jnp.arange(nb)[:, None] - 1) * ATT_BLOCK + jnp.arange(KB)[None, :]
    mask = (jnp.abs(rel) <= WINDOW)[None] & ((kpos >= 0) & (kpos < L))[:, None, :]
    s = jnp.where(mask[None, :, None, None], s + bias[None, None], NEG_INF)
    sk = sink.astype(jnp.float32).reshape(N_KV_HEADS, GQA_GROUP)[None, None, :, :, None, None]
    m = jnp.maximum(jnp.max(s, axis=-1, keepdims=True), sk)
    p = jnp.exp(s - m)
    probs = p / (jnp.sum(p, axis=-1, keepdims=True) + jnp.exp(sk - m))
    o = jnp.einsum('bnhgqk,bnkhd->bnqhgd', probs.astype(vb.dtype), vb)
    return o.reshape(B, L, N_HEADS * HEAD_DIM) @ w_out


def swiglu(x, wg, wu, wd):
    return (jax.nn.silu(x @ wg) * (x @ wu)) @ wd


def moe_swiglu(x, w_router, w_gate, w_up, w_down):
    B, L, D = x.shape
    T = B * L
    xt = x.reshape(T, D)
    logits = jnp.dot(xt, w_router, preferred_element_type=jnp.float32)
    top_val, top_idx = lax.top_k(logits, TOP_K)
    gates = jax.nn.softmax(top_val, axis=-1)
    A = T * TOP_K
    expert_of = top_idx.reshape(A)
    token_of = jnp.repeat(jnp.arange(T, dtype=jnp.int32), TOP_K)
    gate_of = gates.reshape(A)
    order = jnp.argsort(expert_of)
    s_exp, s_tok, s_gate = expert_of[order], token_of[order], gate_of[order]
    counts = jnp.bincount(expert_of, length=N_EXPERTS)
    padded = (counts + MOE_BLOCK - 1) // MOE_BLOCK * MOE_BLOCK
    start = jnp.cumsum(counts) - counts
    pend = jnp.cumsum(padded)
    pstart = pend - padded
    dest = pstart[s_exp] + jnp.arange(A, dtype=jnp.int32) - start[s_exp]
    P = A + N_EXPERTS * MOE_BLOCK
    buf_tok = jnp.zeros((P,), jnp.int32).at[dest].set(s_tok)
    buf_gate = jnp.zeros((P,), jnp.float32).at[dest].set(s_gate)
    n_blk = P // MOE_BLOCK
    blk_exp = jnp.minimum(jnp.searchsorted(pend, jnp.arange(n_blk) * MOE_BLOCK, side='right'), N_EXPERTS - 1)
    xs = xt[buf_tok].reshape(n_blk, MOE_BLOCK, D)

    def expert_block(args):
        xb, e = args
        return (jax.nn.silu(xb @ w_gate[e]) * (xb @ w_up[e])) @ w_down[e]

    ys = lax.map(expert_block, (xs, blk_exp)).reshape(P, D)
    ys = ys * buf_gate[:, None].astype(ys.dtype)
    return jnp.zeros((T, D), ys.dtype).at[buf_tok].add(ys).reshape(B, L, D)


def run_trunk(x, norm_mix, norm_ffn, even_p, odd_p, rel_bias):
    (ev_w_in, ev_w_out, sgu_ln_g, sgu_ln_b, sgu_w, sgu_b, hy_conv, hy_f_w1, hy_f_b1, hy_f_w2, hy_f_b2,
     hy_f_w3, hy_f_b3, hy_f_freq, hy_skip, ffn_w_gate, ffn_w_up, ffn_w_down) = even_p
    (at_w_qkv, at_q_norm, at_k_norm, at_sink, at_w_out, moe_router, moe_w_gate, moe_w_up, moe_w_down) = odd_p
    for i in range(DEPTH):
        j = i // 2
        h = rms_norm(x, norm_mix[i])
        if i % 2 == 0:
            x = x + even_mixer(h, ev_w_in[j], ev_w_out[j], sgu_ln_g[j], sgu_ln_b[j], sgu_w[j], sgu_b[j],
                               hy_conv[j], hy_f_w1[j], hy_f_b1[j], hy_f_w2[j], hy_f_b2[j], hy_f_w3[j],
                               hy_f_b3[j], hy_f_freq[j], hy_skip[j])
            x = x + swiglu(rms_norm(x, norm_ffn[i]), ffn_w_gate[j], ffn_w_up[j], ffn_w_down[j])
        else:
            x = x + window_attention(h, at_w_qkv[j], at_q_norm[j], at_k_norm[j], at_sink[j], at_w_out[j], rel_bias)
            x = x + moe_swiglu(rms_norm(x, norm_ffn[i]), moe_router[j], moe_w_gate[j], moe_w_up[j], moe_w_down[j])
    return x


def setup_inputs(seed: int = 0) -> dict:
    key = jax.random.key(seed)
    ks = iter(jax.random.split(key, 40))
    ne = (DEPTH + 1) // 2
    no = DEPTH // 2

    def nrm(shape, scale):
        return jax.random.normal(next(ks), shape, jnp.float32) * scale

    return {
        'x_prompt': nrm((BATCH, SEQ, D_MODEL), 1.0),
        'x_sample': nrm((DEC_BATCH, DEC_SEQ, D_MODEL), 1.0),
        'norm_mix': 1.0 + nrm((DEPTH, D_MODEL), 0.02),
        'norm_ffn': 1.0 + nrm((DEPTH, D_MODEL), 0.02),
        'ev_w_in': nrm((ne, D_MODEL, EVEN_IN), D_MODEL ** -0.5),
        'ev_w_out': nrm((ne, MIX_WIDTH, D_MODEL), MIX_WIDTH ** -0.5),
        'sgu_ln_g': 1.0 + nrm((ne, A_WIDTH), 0.02),
        'sgu_ln_b': nrm((ne, A_WIDTH), 0.02),
        'sgu_w': nrm((ne, A_GROUPS, CHUNK, CHUNK), CHUNK ** -0.5),
        'sgu_b': 1.0 + nrm((ne, A_GROUPS, CHUNK), 0.02),
        'hy_conv': nrm((ne, SHORT_CONV, (HYENA_ORDER + 1) * B_WIDTH), SHORT_CONV ** -0.5),
        'hy_f_w1': nrm((ne, FILTER_EMB, FILTER_HIDDEN), FILTER_EMB ** -0.5),
        'hy_f_b1': nrm((ne, FILTER_HIDDEN), 0.1),
        'hy_f_w2': nrm((ne, FILTER_HIDDEN, FILTER_HIDDEN), FILTER_HIDDEN ** -0.5),
        'hy_f_b2': nrm((ne, FILTER_HIDDEN), 0.1),
        'hy_f_w3': nrm((ne, FILTER_HIDDEN, FILTER_DIRS * HYENA_ORDER * B_WIDTH), 0.02 * FILTER_HIDDEN ** -0.5),
        'hy_f_b3': nrm((ne, FILTER_DIRS * HYENA_ORDER * B_WIDTH), 0.005),
        'hy_f_freq': 1.0 + nrm((ne, 2, FILTER_HIDDEN), 0.02),
        'hy_skip': nrm((ne, HYENA_ORDER, B_WIDTH), 0.5),
        'ffn_w_gate': nrm((ne, D_MODEL, D_FF), D_MODEL ** -0.5),
        'ffn_w_up': nrm((ne, D_MODEL, D_FF), D_MODEL ** -0.5),
        'ffn_w_down': nrm((ne, D_FF, D_MODEL), D_FF ** -0.5),
        'at_w_qkv': nrm((no, D_MODEL, QKV_WIDTH), D_MODEL ** -0.5),
        'at_q_norm': 1.0 + nrm((no, HEAD_DIM), 0.02),
        'at_k_norm': 1.0 + nrm((no, HEAD_DIM), 0.02),
        'at_sink': nrm((no, N_HEADS), 0.5),
        'at_w_out': nrm((no, N_HEADS * HEAD_DIM, D_MODEL), (N_HEADS * HEAD_DIM) ** -0.5),
        'rel_bias': nrm((REL_BUCKETS, N_HEADS), 0.5),
        'moe_router': nrm((no, D_MODEL, N_EXPERTS), D_MODEL ** -0.5),
        'moe_w_gate': nrm((no, N_EXPERTS, D_MODEL, D_FF_EXPERT), D_MODEL ** -0.5),
        'moe_w_up': nrm((no, N_EXPERTS, D_MODEL, D_FF_EXPERT), D_MODEL ** -0.5),
        'moe_w_down': nrm((no, N_EXPERTS, D_FF_EXPERT, D_MODEL), D_FF_EXPERT ** -0.5),
    }


def reference(x_prompt, x_sample, norm_mix, norm_ffn, ev_w_in, ev_w_out, sgu_ln_g, sgu_ln_b, sgu_w, sgu_b,
              hy_conv, hy_f_w1, hy_f_b1, hy_f_w2, hy_f_b2, hy_f_w3, hy_f_b3, hy_f_freq, hy_skip,
              ffn_w_gate, ffn_w_up, ffn_w_down, at_w_qkv, at_q_norm, at_k_norm, at_sink, at_w_out,
              rel_bias, moe_router, moe_w_gate, moe_w_up, moe_w_down):
    even_p = (ev_w_in, ev_w_out, sgu_ln_g, sgu_ln_b, sgu_w, sgu_b, hy_conv, hy_f_w1, hy_f_b1, hy_f_w2,
              hy_f_b2, hy_f_w3, hy_f_b3, hy_f_freq, hy_skip, ffn_w_gate, ffn_w_up, ffn_w_down)
    odd_p = (at_w_qkv, at_q_norm, at_k_norm, at_sink, at_w_out, moe_router, moe_w_gate, moe_w_up, moe_w_down)
    y_prompt = run_trunk(x_prompt, norm_mix, norm_ffn, even_p, odd_p, rel_bias)
    y_sample = run_trunk(x_sample, norm_mix, norm_ffn, even_p, odd_p, rel_bias)
    return (y_prompt, y_sample)
```

```python
import functools
import math

import jax
import jax.numpy as jnp
from jax import lax
from jax.experimental import pallas as pl
from jax.experimental.pallas import tpu as pltpu

F32 = jnp.float32
BF16 = jnp.bfloat16
U32 = jnp.uint32
I32 = jnp.int32

D_MODEL = 1024
A_GROUPS = 4
A_WIDTH = D_MODEL // 2
CHUNK = 128
B_WIDTH = D_MODEL // 2
HYENA_ORDER = 2
FILTER_BANDS = 16
FILTER_EMB = 1 + 2 * FILTER_BANDS
FILTER_HIDDEN = 64
DECAY_TARGET = 1e-2
FAST_DECAY_PCT = 0.3
SLOW_DECAY_PCT = 1.5
HEAD_DIM = 64
N_HEADS = D_MODEL // HEAD_DIM
N_KV_HEADS = N_HEADS // 4
GQA_GROUP = N_HEADS // N_KV_HEADS
WINDOW = 128
ATT_BLOCK = 128
REL_BUCKETS = 32
REL_MAX_DIST = 128
NEG_INF = -1e30
D_FF = 2816
N_EXPERTS = 8
D_FF_EXPERT = 3584
EPS = 1e-6

LANES = 128
FFT_COLS = 128
HY_PHASES = 4
MIB = 1024 * 1024

TM_PROJ = 512
TM_FFN = 512
TF_FFN = D_FF // 2
TM_MOE = 512
TF_MOE = D_FF_EXPERT // 2
TM_ROUTER = 512
TM_COMBINE = 256


def _cparams(sem, vmem_mib):
    return pltpu.CompilerParams(dimension_semantics=sem, vmem_limit_bytes=vmem_mib * MIB)


def _rms_scale(x):
    return x * lax.rsqrt(jnp.mean(x * x, axis=-1, keepdims=True) + EPS)


def _dot(a, b):
    return jnp.dot(a, b, preferred_element_type=F32)


def _even_in_body(x_ref, g_ref, w_ref, lng_ref, lnb_ref, sw_ref, sb_ref, ya_ref, hb_ref):
    x = x_ref[...]
    h = (_rms_scale(x) * g_ref[...]).astype(BF16)
    u = jax.nn.gelu(_dot(h, w_ref[:, 0:A_WIDTH]))
    v = jax.nn.gelu(_dot(h, w_ref[:, A_WIDTH:2 * A_WIDTH]))
    hb_ref[...] = _dot(h, w_ref[:, 2 * A_WIDTH:])
    tm = x.shape[0]
    for gi in range(A_GROUPS):
        cs = slice(gi * LANES, (gi + 1) * LANES)
        vg = v[:, cs]
        xc = vg - jnp.mean(vg, axis=-1, keepdims=True)
        var = jnp.mean(xc * xc, axis=-1, keepdims=True)
        vn = (xc * lax.rsqrt(var + EPS) * lng_ref[:, cs] + lnb_ref[:, cs]).astype(BF16)
        for c in range(tm // CHUNK):
            rs = slice(c * CHUNK, (c + 1) * CHUNK)
            mixed = _dot(sw_ref[gi], vn[rs]) + sb_ref[gi]
            ya_ref[rs, cs] = (u[rs, cs] * mixed).astype(ya_ref.dtype)


def _even_in(x2d, g, w_in, ln_g, ln_b, sgu_w, sgu_b):
    t = x2d.shape[0]
    tm = TM_PROJ
    n_in = w_in.shape[1]
    const = lambda *shape: pl.BlockSpec(shape, lambda i: (0,) * len(shape))
    return pl.pallas_call(
        _even_in_body,
        grid=(t // tm,),
        in_specs=[
            pl.BlockSpec((tm, D_MODEL), lambda i: (i, 0)),
            const(1, D_MODEL),
            const(D_MODEL, n_in),
            const(1, A_WIDTH),
            const(1, A_WIDTH),
            const(A_GROUPS, CHUNK, CHUNK),
            const(A_GROUPS, CHUNK, LANES),
        ],
        out_specs=[
            pl.BlockSpec((tm, A_WIDTH), lambda i: (i, 0)),
            pl.BlockSpec((tm, n_in - 2 * A_WIDTH), lambda i: (i, 0)),
        ],
        out_shape=[
            jax.ShapeDtypeStruct((t, A_WIDTH), BF16),
            jax.ShapeDtypeStruct((t, n_in - 2 * A_WIDTH), F32),
        ],
        compiler_params=_cparams(("parallel",), 48),
        name="even_in",
    )(x2d, g, w_in, ln_g, ln_b, sgu_w, sgu_b)


def _filter_body(ft_ref, w1_ref, b1_ref, w2_ref, b2_ref, w3_ref, b3_ref, fr0_ref, fr1_ref, absd_ref, o_ref):
    hp = lax.Precision.HIGHEST
    ft = ft_ref[...]
    h = jnp.sin(fr0_ref[...] * (jnp.dot(ft, w1_ref[...], precision=hp, preferred_element_type=F32) + b1_ref[...]))
    h = jnp.sin(fr1_ref[...] * (jnp.dot(h, w2_ref[...], precision=hp, preferred_element_type=F32) + b2_ref[...]))
    h = jnp.dot(h, w3_ref[...], precision=hp, preferred_element_type=F32) + b3_ref[...]
    decay = jnp.exp(-ft[:, 0:1] * absd_ref[...])
    reps = h.shape[1] // decay.shape[1]
    o_ref[...] = h * jnp.concatenate([decay] * reps, axis=1)


def _hyena_filters(seq, w1, b1, w2, b2, w3, b3, freq):
    t = jnp.linspace(0.0, 1.0, seq, dtype=F32)[:, None]
    w = 2.0 * math.pi * jnp.arange(seq, dtype=F32)[:, None] / seq
    bands = jnp.linspace(1e-4, FILTER_BANDS - 1, FILTER_BANDS, dtype=F32)[None, :]
    feats = jnp.concatenate([t, jnp.cos(bands * w), jnp.sin(bands * w)], axis=-1)
    feats = jnp.pad(feats, ((0, 0), (0, LANES - FILTER_EMB)))
    hpad = LANES - FILTER_HIDDEN
    w1p = jnp.pad(w1.astype(F32), ((0, LANES - FILTER_EMB), (0, hpad)))
    w2p = jnp.pad(w2.astype(F32), ((0, hpad), (0, hpad)))
    w3p = jnp.pad(w3.astype(F32), ((0, hpad), (0, 0)))
    b1p = jnp.pad(b1.astype(F32), (0, hpad))[None, :]
    b2p = jnp.pad(b2.astype(F32), (0, hpad))[None, :]
    fr = jnp.pad(freq.astype(F32), ((0, 0), (0, hpad)))
    deltas = jnp.linspace(math.log(DECAY_TARGET) / SLOW_DECAY_PCT,
                          math.log(DECAY_TARGET) / FAST_DECAY_PCT, B_WIDTH, dtype=F32)
    absd = jnp.abs(deltas)[None, :]
    n_out = w3.shape[1]
    tl = min(seq, 1024)
    const = lambda *shape: pl.BlockSpec(shape, lambda i: (0,) * len(shape))
    return pl.pallas_call(
        _filter_body,
        grid=(seq // tl,),
        in_specs=[
            pl.BlockSpec((tl, LANES), lambda i: (i, 0)),
            const(LANES, LANES), const(1, LANES), const(LANES, LANES), const(1, LANES),
            const(LANES, n_out), const(1, n_out), const(1, LANES), const(1, LANES), const(1, B_WIDTH),
        ],
        out_specs=pl.BlockSpec((tl, n_out), lambda i: (i, 0)),
        out_shape=jax.ShapeDtypeStruct((seq, n_out), F32),
        compiler_params=_cparams(("parallel",), 48),
        name="hyena_filter",
    )(feats, w1p, b1p, w2p, b2p, w3p, b3.astype(F32)[None, :], fr[0:1], fr[1:2], absd)


def _dft_tables(seq):
    n = 2 * seq
    nb = n // FFT_COLS
    n2 = jnp.arange(FFT_COLS, dtype=I32)[:, None, None]
    k1 = jnp.arange(nb, dtype=I32)[None, :, None]
    n1 = jnp.arange(nb, dtype=I32)[None, None, :]
    ang = (2.0 * math.pi / n) * ((k1 * (FFT_COLS * n1 + n2)) % n).astype(F32)
    g1_full = jnp.concatenate([jnp.cos(ang), -jnp.sin(ang)], axis=1)
    g1_half = g1_full[:, :, : nb // 2]
    g4_half = jnp.swapaxes(g1_half, 1, 2)
    a = jnp.arange(FFT_COLS, dtype=I32)
    ang2 = (2.0 * math.pi / FFT_COLS) * ((a[:, None] * a[None, :]) % FFT_COLS).astype(F32)
    cr, ci = jnp.cos(ang2), -jnp.sin(ang2)
    fc = jnp.concatenate([jnp.concatenate([cr, -ci], axis=1), jnp.concatenate([ci, cr], axis=1)], axis=0)
    fch = jnp.concatenate([jnp.concatenate([cr, ci], axis=1), jnp.concatenate([-ci, cr], axis=1)], axis=0)
    return (g1_full.astype(BF16), g1_half.astype(BF16), g4_half.astype(BF16), fc.astype(BF16), fch.astype(BF16))


def _spec_body(k_ref, g1_ref, fc_ref, o_ref, a_ref, *, nb):
    p = pl.program_id(2)
    nc = FFT_COLS // HY_PHASES
    kc = nb // HY_PHASES
    inv_n = 1.0 / (nb * FFT_COLS)

    @pl.when(p < HY_PHASES)
    def _():
        def step(j, c):
            n2 = p * nc + j
            xs = k_ref[pl.ds(n2, nb, stride=FFT_COLS), :].astype(BF16)
            a_ref[pl.ds(pl.multiple_of(n2 * 2 * nb, 2 * nb), 2 * nb), :] = _dot(g1_ref[j], xs)
            return c
        lax.fori_loop(0, nc, step, 0)

    @pl.when(p >= HY_PHASES)
    def _():
        def step(j, c):
            kl = 2 * j
            k1 = (p - HY_PHASES) * kc + kl
            cols = []
            for d in range(2):
                br = a_ref[pl.ds(k1 + d, FFT_COLS, stride=2 * nb), :]
                bi = a_ref[pl.ds(nb + k1 + d, FFT_COLS, stride=2 * nb), :]
                cols.append(jnp.concatenate([br, bi], axis=0))
            z = _dot(fc_ref[...], jnp.concatenate(cols, axis=1).astype(BF16)) * inv_n
            o_ref[kl] = z[:, :LANES]
            o_ref[kl + 1] = z[:, LANES:]
            return c
        lax.fori_loop(0, kc // 2, step, 0)


def _filter_spectrum(kfull, g1_full, fc):
    orders, n, c = kfull.shape
    nb = n // FFT_COLS
    kc = nb // HY_PHASES
    nc = FFT_COLS // HY_PHASES
    ph = HY_PHASES
    return pl.pallas_call(
        functools.partial(_spec_body, nb=nb),
        grid=(orders, c // LANES, 2 * ph),
        in_specs=[
            pl.BlockSpec((None, n, LANES), lambda o, cb, p: (o, 0, cb)),
            pl.BlockSpec((nc, 2 * nb, nb), lambda o, cb, p: (jnp.minimum(p, ph - 1), 0, 0)),
            pl.BlockSpec((2 * FFT_COLS, 2 * FFT_COLS), lambda o, cb, p: (0, 0)),
        ],
        out_specs=pl.BlockSpec((None, kc, 2 * FFT_COLS, LANES),
                               lambda o, cb, p: (o, jnp.clip(p - ph, 0, ph - 1), 0, cb)),
        out_shape=jax.ShapeDtypeStruct((orders, nb, 2 * FFT_COLS, c), F32),
        scratch_shapes=[pltpu.VMEM((FFT_COLS * 2 * nb, LANES), F32)],
        compiler_params=_cparams(("parallel", "parallel", "arbitrary"), 56),
        name="hyena_filter_spectrum",
    )(kfull, g1_full, fc)


def _shift_rows(x, down):
    rows = x.shape[0]
    idx = lax.broadcasted_iota(I32, x.shape, 0)
    if down:
        return jnp.where(idx == 0, 0.0, pltpu.roll(x, 1, axis=0))
    return jnp.where(idx == rows - 1, 0.0, pltpu.roll(x, rows - 1, axis=0))


def _load_rows(ref, n2, rows):
    return ref[pl.ds(n2, rows, stride=FFT_COLS), :]


def _load_conv_rows(ref, taps_ref, n2, rows):
    last = FFT_COLS - 1
    x0 = _load_rows(ref, n2, rows)
    xm = _load_rows(ref, jnp.where(n2 == 0, last, n2 - 1), rows)
    xm = jnp.where(n2 == 0, _shift_rows(xm, True), xm)
    xp = _load_rows(ref, jnp.where(n2 == last, 0, n2 + 1), rows)
    xp = jnp.where(n2 == last, _shift_rows(xp, False), xp)
    return taps_ref[0:1, :] * xm + taps_ref[1:2, :] * x0 + taps_ref[2:3, :] * xp


def _conv_body(z_ref, g_ref, zt_ref, gt_ref, sk_ref, g1_ref, fc_ref, fch_ref, kf_ref, g4_ref,
               o_ref, a_ref, *, nb, conv_z):
    p = pl.program_id(2)
    ph = HY_PHASES
    nh = nb // 2
    nc = FFT_COLS // ph
    kc = nb // ph

    def load_z(n2):
        if conv_z:
            return _load_conv_rows(z_ref, zt_ref, n2, nh)
        return _load_rows(z_ref, n2, nh)

    @pl.when(p < ph)
    def _():
        def step(j, c):
            n2 = p * nc + j
            xs = load_z(n2).astype(BF16)
            a_ref[pl.ds(pl.multiple_of(n2 * 2 * nb, 2 * nb), 2 * nb), :] = _dot(g1_ref[j], xs)
            return c
        lax.fori_loop(0, nc, step, 0)

    @pl.when(jnp.logical_and(p >= ph, p < 2 * ph))
    def _():
        def step(j, c):
            kl = 2 * j
            k1 = (p - ph) * kc + kl
            cols = []
            for d in range(2):
                br = a_ref[pl.ds(k1 + d, FFT_COLS, stride=2 * nb), :]
                bi = a_ref[pl.ds(nb + k1 + d, FFT_COLS, stride=2 * nb), :]
                cols.append(jnp.concatenate([br, bi], axis=0))
            z = _dot(fc_ref[...], jnp.concatenate(cols, axis=1).astype(BF16))
            ys = []
            for d in range(2):
                zr = z[:FFT_COLS, d * LANES:(d + 1) * LANES]
                zi = z[FFT_COLS:, d * LANES:(d + 1) * LANES]
                kf = kf_ref[kl + d]
                kr, ki = kf[:FFT_COLS], kf[FFT_COLS:]
                ys.append(jnp.concatenate([zr * kr - zi * ki, zr * ki + zi * kr], axis=0))
            cc = _dot(fch_ref[...], jnp.concatenate(ys, axis=1).astype(BF16))
            for d in range(2):
                a_ref[pl.ds(k1 + d, FFT_COLS, stride=2 * nb), :] = cc[:FFT_COLS, d * LANES:(d + 1) * LANES]
                a_ref[pl.ds(nb + k1 + d, FFT_COLS, stride=2 * nb), :] = cc[FFT_COLS:, d * LANES:(d + 1) * LANES]
            return c
        lax.fori_loop(0, kc // 2, step, 0)

    @pl.when(p >= 2 * ph)
    def _():
        def step(j, c):
            n2 = (p - 2 * ph) * nc + j
            cmat = a_ref[pl.ds(pl.multiple_of(n2 * 2 * nb, 2 * nb), 2 * nb), :].astype(BF16)
            y = _dot(g4_ref[j], cmat)
            val = _load_conv_rows(g_ref, gt_ref, n2, nh) * (y + sk_ref[...] * load_z(n2))
            o_ref[pl.ds(n2, nh, stride=FFT_COLS), :] = val
            return c
        lax.fori_loop(0, nc, step, 0)


def _hyena_conv(zsrc, zoff, gsrc, goff, taps, skip, kf, tables, conv_z):
    _, g1_half, g4_half, fc, fch = tables
    bsz, seq, _ = zsrc.shape
    nb = 2 * seq // FFT_COLS
    nh = nb // 2
    ph = HY_PHASES
    nc = FFT_COLS // ph
    kc = nb // ph
    ncb = B_WIDTH // LANES
    one = pl.Buffered(1)
    return pl.pallas_call(
        functools.partial(_conv_body, nb=nb, conv_z=conv_z),
        grid=(ncb, bsz, 3 * ph),
        in_specs=[
            pl.BlockSpec((None, seq, LANES), lambda cb, b, p: (b, 0, zoff + cb), pipeline_mode=one),
            pl.BlockSpec((None, seq, LANES), lambda cb, b, p: (b, 0, goff + cb), pipeline_mode=one),
            pl.BlockSpec((3, LANES), lambda cb, b, p: (0, zoff + cb)),
            pl.BlockSpec((3, LANES), lambda cb, b, p: (0, goff + cb)),
            pl.BlockSpec((1, LANES), lambda cb, b, p: (0, cb)),
            pl.BlockSpec((nc, 2 * nb, nh), lambda cb, b, p: (jnp.minimum(p, ph - 1), 0, 0)),
            pl.BlockSpec((2 * FFT_COLS, 2 * FFT_COLS), lambda cb, b, p: (0, 0)),
            pl.BlockSpec((2 * FFT_COLS, 2 * FFT_COLS), lambda cb, b, p: (0, 0)),
            pl.BlockSpec((kc, 2 * FFT_COLS, LANES), lambda cb, b, p: (jnp.clip(p - ph, 0, ph - 1), 0, cb)),
            pl.BlockSpec((nc, nh, 2 * nb), lambda cb, b, p: (jnp.clip(p - 2 * ph, 0, ph - 1), 0, 0)),
        ],
        out_specs=pl.BlockSpec((None, seq, LANES), lambda cb, b, p: (b, 0, cb), pipeline_mode=one),
        out_shape=jax.ShapeDtypeStruct((bsz, seq, B_WIDTH), F32),
        scratch_shapes=[pltpu.VMEM((FFT_COLS * 2 * nb, LANES), F32)],
        compiler_params=_cparams(("parallel", "parallel", "arbitrary"), 56),
        name="hyena_conv",
    )(zsrc, gsrc, taps, taps, skip, g1_half, fc, fch, kf, g4_half)


def _out_proj_body(*refs, n_in):
    res_ref = refs[0]
    o_ref = refs[-1]
    acc = res_ref[...]
    for i in range(n_in):
        acc = acc + _dot(refs[1 + i][...].astype(BF16), refs[1 + n_in + i][...])
    o_ref[...] = acc


def _out_proj(res, acts, weights):
    t = res.shape[0]
    tm = TM_PROJ
    n_in = len(acts)
    in_specs = [pl.BlockSpec((tm, D_MODEL), lambda i: (i, 0))]
    in_specs += [pl.BlockSpec((tm, a.shape[1]), lambda i: (i, 0)) for a in acts]
    in_specs += [pl.BlockSpec(w.shape, lambda i: (0, 0)) for w in weights]
    return pl.pallas_call(
        functools.partial(_out_proj_body, n_in=n_in),
        grid=(t // tm,),
        in_specs=in_specs,
        out_specs=pl.BlockSpec((tm, D_MODEL), lambda i: (i, 0)),
        out_shape=jax.ShapeDtypeStruct((t, D_MODEL), F32),
        compiler_params=_cparams(("parallel",), 48),
        name="out_proj",
    )(res, *acts, *weights)


def _ffn_body(x_ref, g_ref, wg_ref, wu_ref, wd_ref, o_ref, h_ref, acc_ref):
    j = pl.program_id(1)

    @pl.when(j == 0)
    def _():
        x = x_ref[...]
        h_ref[...] = (_rms_scale(x) * g_ref[...]).astype(BF16)
        acc_ref[...] = x

    h = h_ref[...]
    a = (jax.nn.silu(_dot(h, wg_ref[...])) * _dot(h, wu_ref[...])).astype(BF16)
    acc_ref[...] += _dot(a, wd_ref[...])

    @pl.when(j == pl.num_programs(1) - 1)
    def _():
        o_ref[...] = acc_ref[...]


def _ffn(x2d, g, wg, wu, wd):
    t = x2d.shape[0]
    tm, tf = TM_FFN, TF_FFN
    return pl.pallas_call(
        _ffn_body,
        grid=(t // tm, D_FF // tf),
        in_specs=[
            pl.BlockSpec((tm, D_MODEL), lambda i, j: (i, 0)),
            pl.BlockSpec((1, D_MODEL), lambda i, j: (0, 0)),
            pl.BlockSpec((D_MODEL, tf), lambda i, j: (0, j)),
            pl.BlockSpec((D_MODEL, tf), lambda i, j: (0, j)),
            pl.BlockSpec((tf, D_MODEL), lambda i, j: (j, 0)),
        ],
        out_specs=pl.BlockSpec((tm, D_MODEL), lambda i, j: (i, 0)),
        out_shape=jax.ShapeDtypeStruct((t, D_MODEL), F32),
        scratch_shapes=[pltpu.VMEM((tm, D_MODEL), BF16), pltpu.VMEM((tm, D_MODEL), F32)],
        compiler_params=_cparams(("parallel", "arbitrary"), 56),
        name="ffn_swiglu",
    )(x2d, g, wg, wu, wd)


def _qkv_body(x_ref, g_ref, w_ref, o_ref):
    h = (_rms_scale(x_ref[...]) * g_ref[...]).astype(BF16)
    o_ref[...] = _dot(h, w_ref[...]).astype(o_ref.dtype)


def _qkv_proj(x2d, g, w):
    t = x2d.shape[0]
    tm = TM_PROJ
    n_out = w.shape[1]
    return pl.pallas_call(
        _qkv_body,
        grid=(t // tm,),
        in_specs=[
            pl.BlockSpec((tm, D_MODEL), lambda i: (i, 0)),
            pl.BlockSpec((1, D_MODEL), lambda i: (0, 0)),
            pl.BlockSpec((D_MODEL, n_out), lambda i: (0, 0)),
        ],
        out_specs=pl.BlockSpec((tm, n_out), lambda i: (i, 0)),
        out_shape=jax.ShapeDtypeStruct((t, n_out), BF16),
        compiler_params=_cparams(("parallel",), 48),
        name="qkv_proj",
    )(x2d, g, w)


def _t5_bucket(rel):
    nbk = REL_BUCKETS // 2
    max_exact = nbk // 2
    ret = jnp.where(rel > 0, nbk, 0)
    n = jnp.abs(rel)
    large = max_exact + (jnp.log(jnp.maximum(n, 1).astype(F32) / max_exact)
                         / math.log(REL_MAX_DIST / max_exact) * (nbk - max_exact)).astype(I32)
    large = jnp.minimum(large, nbk - 1)
    return ret + jnp.where(n < max_exact, n, large)


def _attn_body(sink_ref, q_ref, kp_ref, ko_ref, kn_ref, vp_ref, vo_ref, vn_ref, bias_ref, qg_ref, kg_ref,
               o_ref, *, seq):
    i = pl.program_id(1)
    kb = 3 * ATT_BLOCK
    qidx = lax.broadcasted_iota(I32, (ATT_BLOCK, kb), 0)
    kidx = lax.broadcasted_iota(I32, (ATT_BLOCK, kb), 1)
    rel = kidx - ATT_BLOCK - qidx
    kpos = (i - 1) * ATT_BLOCK + kidx
    mask = (jnp.abs(rel) <= WINDOW) & (kpos >= 0) & (kpos < seq)
    k_all = jnp.concatenate([kp_ref[...], ko_ref[...], kn_ref[...]], axis=0).astype(F32)
    v_all = jnp.concatenate([vp_ref[...], vo_ref[...], vn_ref[...]], axis=0)
    q_all = q_ref[...].astype(F32)
    scale = HEAD_DIM ** -0.5
    for hk in range(N_KV_HEADS):
        ks = slice(hk * HEAD_DIM, (hk + 1) * HEAD_DIM)
        khn = (_rms_scale(k_all[:, ks]) * kg_ref[...]).astype(BF16)
        vh = v_all[:, ks]
        for gq in range(GQA_GROUP):
            head = hk * GQA_GROUP + gq
            qs = slice(head * HEAD_DIM, (head + 1) * HEAD_DIM)
            qn = (_rms_scale(q_all[:, qs]) * qg_ref[...] * scale).astype(BF16)
            s = lax.dot_general(qn, khn, (((1,), (1,)), ((), ())), preferred_element_type=F32)
            s = jnp.where(mask, s + bias_ref[head], NEG_INF)
            sk = sink_ref[head]
            m = jnp.maximum(jnp.max(s, axis=-1, keepdims=True), sk)
            pexp = jnp.exp(s - m)
            den = jnp.sum(pexp, axis=-1, keepdims=True) + jnp.exp(sk - m)
            o = _dot(pexp.astype(BF16), vh) / den
            o_ref[:, qs] = o.astype(o_ref.dtype)


def _attention(qkv, sink, rel_bias, q_g, k_g):
    bsz, seq, _ = qkv.shape
    nblk = seq // ATT_BLOCK
    kb = 3 * ATT_BLOCK
    rel = jnp.arange(kb)[None, :] - ATT_BLOCK - jnp.arange(ATT_BLOCK)[:, None]
    bias = rel_bias.astype(F32)[_t5_bucket(rel)].transpose(2, 0, 1)
    kvw = N_KV_HEADS * HEAD_DIM
    kcol = N_HEADS * HEAD_DIM // kvw
    vcol = kcol + 1
    prev = lambda b, i: jnp.maximum(i - 1, 0)
    nxt = lambda b, i: jnp.minimum(i + 1, nblk - 1)
    kv_spec = lambda col, fn: pl.BlockSpec((None, ATT_BLOCK, kvw), lambda b, i: (b, fn(b, i), col))
    own = lambda b, i: i
    return pl.pallas_call(
        functools.partial(_attn_body, seq=seq),
        grid=(bsz, nblk),
        in_specs=[
            pl.BlockSpec(memory_space=pltpu.SMEM),
            pl.BlockSpec((None, ATT_BLOCK, N_HEADS * HEAD_DIM), lambda b, i: (b, i, 0)),
            kv_spec(kcol, prev), kv_spec(kcol, own), kv_spec(kcol, nxt),
            kv_spec(vcol, prev), kv_spec(vcol, own), kv_spec(vcol, nxt),
            pl.BlockSpec((N_HEADS, ATT_BLOCK, kb), lambda b, i: (0, 0, 0)),
            pl.BlockSpec((1, HEAD_DIM), lambda b, i: (0, 0)),
            pl.BlockSpec((1, HEAD_DIM), lambda b, i: (0, 0)),
        ],
        out_specs=pl.BlockSpec((None, ATT_BLOCK, N_HEADS * HEAD_DIM), lambda b, i: (b, i, 0)),
        out_shape=jax.ShapeDtypeStruct((bsz, seq, N_HEADS * HEAD_DIM), BF16),
        compiler_params=_cparams(("parallel", "parallel"), 48),
        name="window_attention",
    )(sink.astype(F32), qkv, qkv, qkv, qkv, qkv, qkv, qkv, bias, q_g.astype(F32)[None, :], k_g.astype(F32)[None, :])


def _router_body(x_ref, g_ref, wr_ref, tri_ref, hp_ref, idx_ref, rank_ref, gate_ref, cnt_ref, run_ref):
    @pl.when(pl.program_id(0) == 0)
    def _():
        run_ref[...] = jnp.zeros_like(run_ref)

    h = _rms_scale(x_ref[...]) * g_ref[...]
    half = D_MODEL // 2
    bits = lax.bitcast_convert_type(h.astype(BF16).astype(F32), U32)
    hp_ref[...] = (bits[:, half:] & jnp.uint32(0xFFFF0000)) | (bits[:, :half] >> 16)

    logits = lax.dot_general(wr_ref[...], h, (((1,), (1,)), ((), ())),
                             precision=lax.Precision.HIGHEST, preferred_element_type=F32)
    eid = lax.broadcasted_iota(I32, logits.shape, 0)
    m1 = jnp.max(logits, axis=0, keepdims=True)
    i1 = jnp.min(jnp.where(logits == m1, eid, N_EXPERTS), axis=0, keepdims=True)
    rest = jnp.where(eid == i1, -jnp.inf, logits)
    m2 = jnp.max(rest, axis=0, keepdims=True)
    i2 = jnp.min(jnp.where(rest == m2, eid, N_EXPERTS), axis=0, keepdims=True)
    e2 = jnp.exp(m2 - m1)
    gate_ref[0:1, :] = 1.0 / (1.0 + e2)
    gate_ref[1:2, :] = e2 / (1.0 + e2)
    idx_ref[0:1, :] = i1
    idx_ref[1:2, :] = i2

    sel1 = eid == i1
    sel2 = eid == i2
    onehot = jnp.where(sel1 | sel2, 1.0, 0.0)
    before = _dot(onehot.astype(BF16), tri_ref[...]) + run_ref[:, 0:1]
    rank_ref[0:1, :] = jnp.sum(jnp.where(sel1, before, 0.0), axis=0, keepdims=True).astype(I32)
    rank_ref[1:2, :] = jnp.sum(jnp.where(sel2, before, 0.0), axis=0, keepdims=True).astype(I32)
    run_ref[...] += jnp.sum(onehot, axis=1, keepdims=True)
    cnt_ref[...] = run_ref[...].astype(I32)


def _router(x2d, g, w_router):
    t = x2d.shape[0]
    tm = TM_ROUTER
    tri = (jnp.arange(tm)[:, None] < jnp.arange(tm)[None, :]).astype(BF16)
    two = lambda dt: jax.ShapeDtypeStruct((2, t), dt)
    return pl.pallas_call(
        _router_body,
        grid=(t // tm,),
        in_specs=[
            pl.BlockSpec((tm, D_MODEL), lambda i: (i, 0)),
            pl.BlockSpec((1, D_MODEL), lambda i: (0, 0)),
            pl.BlockSpec((N_EXPERTS, D_MODEL), lambda i: (0, 0)),
            pl.BlockSpec((tm, tm), lambda i: (0, 0)),
        ],
        out_specs=[
            pl.BlockSpec((tm, D_MODEL // 2), lambda i: (i, 0)),
            pl.BlockSpec((2, tm), lambda i: (0, i)),
            pl.BlockSpec((2, tm), lambda i: (0, i)),
            pl.BlockSpec((2, tm), lambda i: (0, i)),
            pl.BlockSpec((N_EXPERTS, LANES), lambda i: (0, 0)),
        ],
        out_shape=[
            jax.ShapeDtypeStruct((t, D_MODEL // 2), U32),
            two(I32), two(I32), two(F32),
            jax.ShapeDtypeStruct((N_EXPERTS, LANES), I32),
        ],
        scratch_shapes=[pltpu.VMEM((N_EXPERTS, LANES), F32)],
        compiler_params=_cparams(("arbitrary",), 48),
        name="moe_router",
    )(x2d, g, w_router.astype(F32).T, tri)


def _gather_body(tok_ref, src_ref, dst_ref, sem, *, rows):
    base = pl.program_id(0) * rows

    def issue(r, c):
        pltpu.make_async_copy(src_ref.at[pl.ds(tok_ref[0, 0, r], 1)], dst_ref.at[pl.ds(base + r, 1)], sem).start()
        return c
    lax.fori_loop(0, rows, issue, 0)

    def drain(r, c):
        pltpu.make_async_copy(src_ref.at[pl.ds(0, 1)], dst_ref.at[pl.ds(base + r, 1)], sem).wait()
        return c
    lax.fori_loop(0, rows, drain, 0)


def _gather_rows(src, tok, rows):
    n = tok.shape[0]
    return pl.pallas_call(
        functools.partial(_gather_body, rows=rows),
        grid=(n // rows,),
        in_specs=[
            pl.BlockSpec((1, 1, rows), lambda i: (i, 0, 0), memory_space=pltpu.SMEM),
            pl.BlockSpec(memory_space=pl.ANY),
        ],
        out_specs=pl.BlockSpec(memory_space=pl.ANY),
        out_shape=jax.ShapeDtypeStruct((n, src.shape[1]), src.dtype),
        scratch_shapes=[pltpu.SemaphoreType.DMA(())],
        compiler_params=pltpu.CompilerParams(dimension_semantics=("arbitrary",), has_side_effects=True),
        name="moe_gather",
    )(tok.reshape(n // rows, 1, rows), src)


def _expert_body(te_ref, nu_ref, xs_ref, wg_ref, wu_ref, wd_ref, o_ref, xb_ref, acc_ref):
    i = pl.program_id(0)
    j = pl.program_id(1)
    half = D_MODEL // 2

    @pl.when(j == 0)
    def _():
        w = xs_ref[...]
        xb_ref[:, :half] = lax.bitcast_convert_type(w << 16, F32).astype(BF16)
        xb_ref[:, half:] = lax.bitcast_convert_type(w & jnp.uint32(0xFFFF0000), F32).astype(BF16)
        acc_ref[...] = jnp.zeros_like(acc_ref)

    @pl.when(i < nu_ref[0])
    def _():
        xb = xb_ref[...]
        a = (jax.nn.silu(_dot(xb, wg_ref[...])) * _dot(xb, wu_ref[...])).astype(BF16)
        acc_ref[...] += _dot(a, wd_ref[...])

    @pl.when(j == pl.num_programs(1) - 1)
    def _():
        o_ref[...] = acc_ref[...]


def _experts(xs, tile_expert, n_used, wg, wu, wd):
    p = xs.shape[0]
    tm, tf = TM_MOE, TF_MOE
    grid_spec = pltpu.PrefetchScalarGridSpec(
        num_scalar_prefetch=2,
        grid=(p // tm, D_FF_EXPERT // tf),
        in_specs=[
            pl.BlockSpec((tm, D_MODEL // 2), lambda i, j, te, nu: (i, 0)),
            pl.BlockSpec((None, D_MODEL, tf), lambda i, j, te, nu: (te[i], 0, j)),
            pl.BlockSpec((None, D_MODEL, tf), lambda i, j, te, nu: (te[i], 0, j)),
            pl.BlockSpec((None, tf, D_MODEL), lambda i, j, te, nu: (te[i], j, 0)),
        ],
        out_specs=pl.BlockSpec((tm, D_MODEL), lambda i, j, te, nu: (i, 0)),
        scratch_shapes=[pltpu.VMEM((tm, D_MODEL), BF16), pltpu.VMEM((tm, D_MODEL), F32)],
    )
    return pl.pallas_call(
        _expert_body,
        grid_spec=grid_spec,
        out_shape=jax.ShapeDtypeStruct((p, D_MODEL), F32),
        compiler_params=_cparams(("parallel", "arbitrary"), 56),
        name="moe_experts",
    )(tile_expert, n_used, xs, wg, wu, wd)


def _combine_body(d1_ref, d2_ref, x_ref, g1_ref, g2_ref, ys_ref, o_ref, y1_ref, y2_ref, sem, *, rows):
    def issue(r, c):
        pltpu.make_async_copy(ys_ref.at[pl.ds(d1_ref[0, 0, r], 1)], y1_ref.at[pl.ds(r, 1)], sem.at[0]).start()
        pltpu.make_async_copy(ys_ref.at[pl.ds(d2_ref[0, 0, r], 1)], y2_ref.at[pl.ds(r, 1)], sem.at[1]).start()
        return c
    lax.fori_loop(0, rows, issue, 0)

    def drain(r, c):
        pltpu.make_async_copy(ys_ref.at[pl.ds(0, 1)], y1_ref.at[pl.ds(r, 1)], sem.at[0]).wait()
        pltpu.make_async_copy(ys_ref.at[pl.ds(0, 1)], y2_ref.at[pl.ds(r, 1)], sem.at[1]).wait()
        return c
    lax.fori_loop(0, rows, drain, 0)
    o_ref[...] = x_ref[...] + g1_ref[...] * y1_ref[...] + g2_ref[...] * y2_ref[...]


def _combine(x2d, ys, dest, gates):
    t = x2d.shape[0]
    rows = TM_COMBINE
    idx_spec = pl.BlockSpec((1, 1, rows), lambda i: (i, 0, 0), memory_space=pltpu.SMEM)
    gate_spec = pl.BlockSpec((rows, 1), lambda i: (i, 0))
    row_spec = pl.BlockSpec((rows, D_MODEL), lambda i: (i, 0))
    return pl.pallas_call(
        functools.partial(_combine_body, rows=rows),
        grid=(t // rows,),
        in_specs=[idx_spec, idx_spec, row_spec, gate_spec, gate_spec, pl.BlockSpec(memory_space=pl.ANY)],
        out_specs=row_spec,
        out_shape=jax.ShapeDtypeStruct((t, D_MODEL), F32),
        scratch_shapes=[pltpu.VMEM((rows, D_MODEL), F32), pltpu.VMEM((rows, D_MODEL), F32),
                        pltpu.SemaphoreType.DMA((2,))],
        compiler_params=_cparams(("arbitrary",), 48),
        name="moe_combine",
    )(dest[0].reshape(t // rows, 1, rows), dest[1].reshape(t // rows, 1, rows), x2d,
      gates[0][:, None], gates[1][:, None], ys)


def _moe(x2d, g, w_router, wg, wu, wd):
    t = x2d.shape[0]
    tm = TM_MOE
    hp, idx, rank, gates, cnt = _router(x2d, g, w_router)
    counts = cnt[:, 0]
    tiles = (counts + tm - 1) // tm
    tile_end = jnp.cumsum(tiles)
    row_start = (tile_end - tiles) * tm
    n_tiles = (2 * t) // tm + N_EXPERTS
    dest = row_start[idx] + rank
    tile_expert = jnp.minimum(jnp.searchsorted(tile_end, jnp.arange(n_tiles), side="right"), N_EXPERTS - 1)
    tok = jnp.zeros((n_tiles * tm,), I32).at[dest.reshape(-1)].set(jnp.tile(jnp.arange(t, dtype=I32), 2))
    xs = _gather_rows(hp, tok, tm)
    ys = _experts(xs, tile_expert.astype(I32), tile_end[-1:].astype(I32), wg, wu, wd)
    return _combine(x2d, ys, dest, gates)


def _even_layer(x, norm_mix, norm_ffn, w_in, w_out, ln_g, ln_b, sgu_w, sgu_b, hy_conv,
                f_w1, f_b1, f_w2, f_b2, f_w3, f_b3, f_freq, hy_skip, wg, wu, wd):
    bsz, seq, _ = x.shape
    x2d = x.reshape(bsz * seq, D_MODEL)
    sgu_bb = jnp.broadcast_to(sgu_b.astype(F32)[:, :, None], (A_GROUPS, CHUNK, LANES))
    ya, hb = _even_in(x2d, norm_mix[None, :], w_in.astype(BF16), ln_g[None, :], ln_b[None, :],
                      sgu_w.astype(BF16), sgu_bb)
    hb = hb.reshape(bsz, seq, -1)

    filt = _hyena_filters(seq, f_w1, f_b1, f_w2, f_b2, f_w3, f_b3, f_freq)
    filt = filt.reshape(seq, 2, HYENA_ORDER, B_WIDTH)
    h_fwd, h_bwd = filt[:, 0], filt[:, 1]
    kfull = jnp.concatenate([h_fwd, jnp.zeros_like(h_fwd[:1]), h_bwd[1:][::-1]], axis=0)
    kfull = kfull.transpose(1, 0, 2)
    tables = _dft_tables(seq)
    kf = _filter_spectrum(kfull, tables[0], tables[3])

    taps = hy_conv.astype(F32)
    skip = hy_skip.astype(F32)
    ncb = B_WIDTH // LANES
    z1 = _hyena_conv(hb, 0, hb, ncb, taps, skip[0:1], kf[0], tables, conv_z=True)
    yb = _hyena_conv(z1, 0, hb, 2 * ncb, taps, skip[1:2], kf[1], tables, conv_z=False)

    w_out = w_out.astype(BF16)
    x2d = _out_proj(x2d, [ya, yb.reshape(bsz * seq, B_WIDTH)], [w_out[:A_WIDTH], w_out[A_WIDTH:]])
    x2d = _ffn(x2d, norm_ffn[None, :], wg.astype(BF16), wu.astype(BF16), wd.astype(BF16))
    return x2d.reshape(bsz, seq, D_MODEL)


def _odd_layer(x, norm_mix, norm_ffn, w_qkv, q_g, k_g, sink, w_out, rel_bias, w_router, wg, wu, wd):
    bsz, seq, _ = x.shape
    x2d = x.reshape(bsz * seq, D_MODEL)
    qkv = _qkv_proj(x2d, norm_mix[None, :], w_qkv.astype(BF16)).reshape(bsz, seq, -1)
    o = _attention(qkv, sink, rel_bias, q_g, k_g)
    x2d = _out_proj(x2d, [o.reshape(bsz * seq, -1)], [w_out.astype(BF16)])
    x2d = _moe(x2d, norm_ffn[None, :], w_router, wg.astype(BF16), wu.astype(BF16), wd.astype(BF16))
    return x2d.reshape(bsz, seq, D_MODEL)


def _trunk(x, norm_mix, norm_ffn, even_p, odd_p, rel_bias):
    depth = norm_mix.shape[0]
    for i in range(depth):
        j = i // 2
        if i % 2 == 0:
            x = _even_layer(x, norm_mix[i], norm_ffn[i], *[p[j] for p in even_p])
        else:
            x = _odd_layer(x, norm_mix[i], norm_ffn[i], *[p[j] for p in odd_p[:5]], rel_bias,
                           *[p[j] for p in odd_p[5:]])
    return x


def kernel(x_prompt, x_sample, norm_mix, norm_ffn, ev_w_in, ev_w_out, sgu_ln_g, sgu_ln_b, sgu_w, sgu_b,
           hy_conv, hy_f_w1, hy_f_b1, hy_f_w2, hy_f_b2, hy_f_w3, hy_f_b3, hy_f_freq, hy_skip,
           ffn_w_gate, ffn_w_up, ffn_w_down, at_w_qkv, at_q_norm, at_k_norm, at_sink, at_w_out,
           rel_bias, moe_router, moe_w_gate, moe_w_up, moe_w_down):
    even_p = (ev_w_in, ev_w_out, sgu_ln_g, sgu_ln_b, sgu_w, sgu_b, hy_conv, hy_f_w1, hy_f_b1, hy_f_w2,
              hy_f_b2, hy_f_w3, hy_f_b3, hy_f_freq, hy_skip, ffn_w_gate, ffn_w_up, ffn_w_down)
    odd_p = (at_w_qkv, at_q_norm, at_k_norm, at_sink, at_w_out, moe_router, moe_w_gate, moe_w_up, moe_w_down)
    assert x_prompt.shape[1:] == x_sample.shape[1:]
    nb_prompt = x_prompt.shape[0]
    x = jnp.concatenate([x_prompt, x_sample], axis=0)
    y = _trunk(x, norm_mix, norm_ffn, even_p, odd_p, rel_bias)
    return (y[:nb_prompt], y[nb_prompt:])
```

```python
import functools
import math

import jax
import jax.numpy as jnp
from jax import lax
from jax.experimental import pallas as pl
from jax.experimental.pallas import tpu as pltpu

F32 = jnp.float32
BF16 = jnp.bfloat16
U32 = jnp.uint32
I32 = jnp.int32

D_MODEL = 1024
A_GROUPS = 4
A_WIDTH = D_MODEL // 2
CHUNK = 128
B_WIDTH = D_MODEL // 2
HYENA_ORDER = 2
FILTER_BANDS = 16
FILTER_EMB = 1 + 2 * FILTER_BANDS
FILTER_HIDDEN = 64
DECAY_TARGET = 1e-2
FAST_DECAY_PCT = 0.3
SLOW_DECAY_PCT = 1.5
HEAD_DIM = 64
N_HEADS = D_MODEL // HEAD_DIM
N_KV_HEADS = N_HEADS // 4
GQA_GROUP = N_HEADS // N_KV_HEADS
WINDOW = 128
ATT_BLOCK = 128
REL_BUCKETS = 32
REL_MAX_DIST = 128
NEG_INF = -1e30
D_FF = 2816
N_EXPERTS = 8
D_FF_EXPERT = 3584
EPS = 1e-6

LANES = 128
FFT_COLS = 128
HY_PHASES = 4
MIB = 1024 * 1024

TM_PROJ = 512
TM_FFN = 512
TF_FFN = D_FF // 2
TM_MOE = 512
TF_MOE = D_FF_EXPERT // 2
TM_ROUTER = 512
TM_COMBINE = 256


def _cparams(sem, vmem_mib):
    return pltpu.CompilerParams(dimension_semantics=sem, vmem_limit_bytes=vmem_mib * MIB)


def _rms_scale(x):
    return x * lax.rsqrt(jnp.mean(x * x, axis=-1, keepdims=True) + EPS)


def _dot(a, b):
    return jnp.dot(a, b, preferred_element_type=F32)


def _even_in_body(x_ref, g_ref, w_ref, lng_ref, lnb_ref, sw_ref, sb_ref, ya_ref, hb_ref):
    x = x_ref[...]
    h = (_rms_scale(x) * g_ref[...]).astype(BF16)
    u = jax.nn.gelu(_dot(h, w_ref[:, 0:A_WIDTH]))
    v = jax.nn.gelu(_dot(h, w_ref[:, A_WIDTH:2 * A_WIDTH]))
    hb_ref[...] = _dot(h, w_ref[:, 2 * A_WIDTH:])
    tm = x.shape[0]
    for gi in range(A_GROUPS):
        cs = slice(gi * LANES, (gi + 1) * LANES)
        vg = v[:, cs]
        xc = vg - jnp.mean(vg, axis=-1, keepdims=True)
        var = jnp.mean(xc * xc, axis=-1, keepdims=True)
        vn = (xc * lax.rsqrt(var + EPS) * lng_ref[:, cs] + lnb_ref[:, cs]).astype(BF16)
        for c in range(tm // CHUNK):
            rs = slice(c * CHUNK, (c + 1) * CHUNK)
            mixed = _dot(sw_ref[gi], vn[rs]) + sb_ref[gi]
            ya_ref[rs, cs] = (u[rs, cs] * mixed).astype(ya_ref.dtype)


def _even_in(x2d, g, w_in, ln_g, ln_b, sgu_w, sgu_b):
    t = x2d.shape[0]
    tm = TM_PROJ
    n_in = w_in.shape[1]
    const = lambda *shape: pl.BlockSpec(shape, lambda i: (0,) * len(shape))
    return pl.pallas_call(
        _even_in_body,
        grid=(t // tm,),
        in_specs=[
            pl.BlockSpec((tm, D_MODEL), lambda i: (i, 0)),
            const(1, D_MODEL),
            const(D_MODEL, n_in),
            const(1, A_WIDTH),
            const(1, A_WIDTH),
            const(A_GROUPS, CHUNK, CHUNK),
            const(A_GROUPS, CHUNK, LANES),
        ],
        out_specs=[
            pl.BlockSpec((tm, A_WIDTH), lambda i: (i, 0)),
            pl.BlockSpec((tm, n_in - 2 * A_WIDTH), lambda i: (i, 0)),
        ],
        out_shape=[
            jax.ShapeDtypeStruct((t, A_WIDTH), BF16),
            jax.ShapeDtypeStruct((t, n_in - 2 * A_WIDTH), F32),
        ],
        compiler_params=_cparams(("parallel",), 48),
        name="even_in",
    )(x2d, g, w_in, ln_g, ln_b, sgu_w, sgu_b)


def _filter_body(ft_ref, w1_ref, b1_ref, w2_ref, b2_ref, w3_ref, b3_ref, fr0_ref, fr1_ref, absd_ref, o_ref):
    hp = lax.Precision.HIGHEST
    ft = ft_ref[...]
    h = jnp.sin(fr0_ref[...] * (jnp.dot(ft, w1_ref[...], precision=hp, preferred_element_type=F32) + b1_ref[...]))
    h = jnp.sin(fr1_ref[...] * (jnp.dot(h, w2_ref[...], precision=hp, preferred_element_type=F32) + b2_ref[...]))
    h = jnp.dot(h, w3_ref[...], precision=hp, preferred_element_type=F32) + b3_ref[...]
    decay = jnp.exp(-ft[:, 0:1] * absd_ref[...])
    reps = h.shape[1] // decay.shape[1]
    o_ref[...] = h * jnp.concatenate([decay] * reps, axis=1)


def _hyena_filters(seq, w1, b1, w2, b2, w3, b3, freq):
    t = jnp.linspace(0.0, 1.0, seq, dtype=F32)[:, None]
    w = 2.0 * math.pi * jnp.arange(seq, dtype=F32)[:, None] / seq
    bands = jnp.linspace(1e-4, FILTER_BANDS - 1, FILTER_BANDS, dtype=F32)[None, :]
    feats = jnp.concatenate([t, jnp.cos(bands * w), jnp.sin(bands * w)], axis=-1)
    feats = jnp.pad(feats, ((0, 0), (0, LANES - FILTER_EMB)))
    hpad = LANES - FILTER_HIDDEN
    w1p = jnp.pad(w1.astype(F32), ((0, LANES - FILTER_EMB), (0, hpad)))
    w2p = jnp.pad(w2.astype(F32), ((0, hpad), (0, hpad)))
    w3p = jnp.pad(w3.astype(F32), ((0, hpad), (0, 0)))
    b1p = jnp.pad(b1.astype(F32), (0, hpad))[None, :]
    b2p = jnp.pad(b2.astype(F32), (0, hpad))[None, :]
    fr = jnp.pad(freq.astype(F32), ((0, 0), (0, hpad)))
    deltas = jnp.linspace(math.log(DECAY_TARGET) / SLOW_DECAY_PCT,
                          math.log(DECAY_TARGET) / FAST_DECAY_PCT, B_WIDTH, dtype=F32)
    absd = jnp.abs(deltas)[None, :]
    n_out = w3.shape[1]
    tl = min(seq, 1024)
    const = lambda *shape: pl.BlockSpec(shape, lambda i: (0,) * len(shape))
    return pl.pallas_call(
        _filter_body,
        grid=(seq // tl,),
        in_specs=[
            pl.BlockSpec((tl, LANES), lambda i: (i, 0)),
            const(LANES, LANES), const(1, LANES), const(LANES, LANES), const(1, LANES),
            const(LANES, n_out), const(1, n_out), const(1, LANES), const(1, LANES), const(1, B_WIDTH),
        ],
        out_specs=pl.BlockSpec((tl, n_out), lambda i: (i, 0)),
        out_shape=jax.ShapeDtypeStruct((seq, n_out), F32),
        compiler_params=_cparams(("parallel",), 48),
        name="hyena_filter",
    )(feats, w1p, b1p, w2p, b2p, w3p, b3.astype(F32)[None, :], fr[0:1], fr[1:2], absd)


def _dft_tables(seq):
    n = 2 * seq
    nb = n // FFT_COLS
    n2 = jnp.arange(FFT_COLS, dtype=I32)[:, None, None]
    k1 = jnp.arange(nb, dtype=I32)[None, :, None]
    n1 = jnp.arange(nb, dtype=I32)[None, None, :]
    ang = (2.0 * math.pi / n) * ((k1 * (FFT_COLS * n1 + n2)) % n).astype(F32)
    g1_full = jnp.concatenate([jnp.cos(ang), -jnp.sin(ang)], axis=1)
    g1_half = g1_full[:, :, : nb // 2]
    g4_half = jnp.swapaxes(g1_half, 1, 2)
    a = jnp.arange(FFT_COLS, dtype=I32)
    ang2 = (2.0 * math.pi / FFT_COLS) * ((a[:, None] * a[None, :]) % FFT_COLS).astype(F32)
    cr, ci = jnp.cos(ang2), -jnp.sin(ang2)
    fc = jnp.concatenate([jnp.concatenate([cr, -ci], axis=1), jnp.concatenate([ci, cr], axis=1)], axis=0)
    fch = jnp.concatenate([jnp.concatenate([cr, ci], axis=1), jnp.concatenate([-ci, cr], axis=1)], axis=0)
    return (g1_full.astype(BF16), g1_half.astype(BF16), g4_half.astype(BF16), fc.astype(BF16), fch.astype(BF16))


def _spec_body(k_ref, g1_ref, fc_ref, o_ref, a_ref, *, nb):
    p = pl.program_id(2)
    nc = FFT_COLS // HY_PHASES
    kc = nb // HY_PHASES
    inv_n = 1.0 / (nb * FFT_COLS)

    @pl.when(p < HY_PHASES)
    def _():
        def step(j, c):
            n2 = p * nc + j
            xs = k_ref[pl.ds(n2, nb, stride=FFT_COLS), :].astype(BF16)
            a_ref[pl.ds(pl.multiple_of(n2 * 2 * nb, 2 * nb), 2 * nb), :] = _dot(g1_ref[j], xs)
            return c
        lax.fori_loop(0, nc, step, 0)

    @pl.when(p >= HY_PHASES)
    def _():
        def step(j, c):
            kl = 2 * j
            k1 = (p - HY_PHASES) * kc + kl
            cols = []
            for d in range(2):
                br = a_ref[pl.ds(k1 + d, FFT_COLS, stride=2 * nb), :]
                bi = a_ref[pl.ds(nb + k1 + d, FFT_COLS, stride=2 * nb), :]
                cols.append(jnp.concatenate([br, bi], axis=0))
            z = _dot(fc_ref[...], jnp.concatenate(cols, axis=1).astype(BF16)) * inv_n
            o_ref[kl] = z[:, :LANES]
            o_ref[kl + 1] = z[:, LANES:]
            return c
        lax.fori_loop(0, kc // 2, step, 0)


def _filter_spectrum(kfull, g1_full, fc):
    orders, n, c = kfull.shape
    nb = n // FFT_COLS
    kc = nb // HY_PHASES
    nc = FFT_COLS // HY_PHASES
    ph = HY_PHASES
    return pl.pallas_call(
        functools.partial(_spec_body, nb=nb),
        grid=(orders, c // LANES, 2 * ph),
        in_specs=[
            pl.BlockSpec((None, n, LANES), lambda o, cb, p: (o, 0, cb)),
            pl.BlockSpec((nc, 2 * nb, nb), lambda o, cb, p: (jnp.minimum(p, ph - 1), 0, 0)),
            pl.BlockSpec((2 * FFT_COLS, 2 * FFT_COLS), lambda o, cb, p: (0, 0)),
        ],
        out_specs=pl.BlockSpec((None, kc, 2 * FFT_COLS, LANES),
                               lambda o, cb, p: (o, jnp.clip(p - ph, 0, ph - 1), 0, cb)),
        out_shape=jax.ShapeDtypeStruct((orders, nb, 2 * FFT_COLS, c), F32),
        scratch_shapes=[pltpu.VMEM((FFT_COLS * 2 * nb, LANES), F32)],
        compiler_params=_cparams(("parallel", "parallel", "arbitrary"), 56),
        name="hyena_filter_spectrum",
    )(kfull, g1_full, fc)


def _shift_rows(x, down):
    rows = x.shape[0]
    idx = lax.broadcasted_iota(I32, x.shape, 0)
    if down:
        return jnp.where(idx == 0, 0.0, pltpu.roll(x, 1, axis=0))
    return jnp.where(idx == rows - 1, 0.0, pltpu.roll(x, rows - 1, axis=0))


def _load_rows(ref, n2, rows):
    return ref[pl.ds(n2, rows, stride=FFT_COLS), :]


def _load_conv_rows(ref, taps_ref, n2, rows):
    last = FFT_COLS - 1
    x0 = _load_rows(ref, n2, rows)
    xm = _load_rows(ref, jnp.where(n2 == 0, last, n2 - 1), rows)
    xm = jnp.where(n2 == 0, _shift_rows(xm, True), xm)
    xp = _load_rows(ref, jnp.where(n2 == last, 0, n2 + 1), rows)
    xp = jnp.where(n2 == last, _shift_rows(xp, False), xp)
    return taps_ref[0:1, :] * xm + taps_ref[1:2, :] * x0 + taps_ref[2:3, :] * xp


def _conv_body(z_ref, g_ref, zt_ref, gt_ref, sk_ref, g1_ref, fc_ref, fch_ref, kf_ref, g4_ref,
               o_ref, a_ref, *, nb, conv_z):
    p = pl.program_id(2)
    ph = HY_PHASES
    nh = nb // 2
    nc = FFT_COLS // ph
    kc = nb // ph

    def load_z(n2):
        if conv_z:
            return _load_conv_rows(z_ref, zt_ref, n2, nh)
        return _load_rows(z_ref, n2, nh)

    @pl.when(p < ph)
    def _():
        def step(j, c):
            n2 = p * nc + j
            xs = load_z(n2).astype(BF16)
            a_ref[pl.ds(pl.multiple_of(n2 * 2 * nb, 2 * nb), 2 * nb), :] = _dot(g1_ref[j], xs)
            return c
        lax.fori_loop(0, nc, step, 0)

    @pl.when(jnp.logical_and(p >= ph, p < 2 * ph))
    def _():
        def step(j, c):
            kl = 2 * j
            k1 = (p - ph) * kc + kl
            cols = []
            for d in range(2):
                br = a_ref[pl.ds(k1 + d, FFT_COLS, stride=2 * nb), :]
                bi = a_ref[pl.ds(nb + k1 + d, FFT_COLS, stride=2 * nb), :]
                cols.append(jnp.concatenate([br, bi], axis=0))
            z = _dot(fc_ref[...], jnp.concatenate(cols, axis=1).astype(BF16))
            ys = []
            for d in range(2):
                zr = z[:FFT_COLS, d * LANES:(d + 1) * LANES]
                zi = z[FFT_COLS:, d * LANES:(d + 1) * LANES]
                kf = kf_ref[kl + d]
                kr, ki = kf[:FFT_COLS], kf[FFT_COLS:]
                ys.append(jnp.concatenate([zr * kr - zi * ki, zr * ki + zi * kr], axis=0))
            cc = _dot(fch_ref[...], jnp.concatenate(ys, axis=1).astype(BF16))
            for d in range(2):
                a_ref[pl.ds(k1 + d, FFT_COLS, stride=2 * nb), :] = cc[:FFT_COLS, d * LANES:(d + 1) * LANES]
                a_ref[pl.ds(nb + k1 + d, FFT_COLS, stride=2 * nb), :] = cc[FFT_COLS:, d * LANES:(d + 1) * LANES]
            return c
        lax.fori_loop(0, kc // 2, step, 0)

    @pl.when(p >= 2 * ph)
    def _():
        def step(j, c):
            n2 = (p - 2 * ph) * nc + j
            cmat = a_ref[pl.ds(pl.multiple_of(n2 * 2 * nb, 2 * nb), 2 * nb), :].astype(BF16)
            y = _dot(g4_ref[j], cmat)
            val = _load_conv_rows(g_ref, gt_ref, n2, nh) * (y + sk_ref[...] * load_z(n2))
            o_ref[pl.ds(n2, nh, stride=FFT_COLS), :] = val
            return c
        lax.fori_loop(0, nc, step, 0)


def _hyena_conv(zsrc, zoff, gsrc, goff, taps, skip, kf, tables, conv_z):
    _, g1_half, g4_half, fc, fch = tables
    bsz, seq, _ = zsrc.shape
    nb = 2 * seq // FFT_COLS
    nh = nb // 2
    ph = HY_PHASES
    nc = FFT_COLS // ph
    kc = nb // ph
    ncb = B_WIDTH // LANES
    one = pl.Buffered(1)
    return pl.pallas_call(
        functools.partial(_conv_body, nb=nb, conv_z=conv_z),
        grid=(ncb, bsz, 3 * ph),
        in_specs=[
            pl.BlockSpec((None, seq, LANES), lambda cb, b, p: (b, 0, zoff + cb), pipeline_mode=one),
            pl.BlockSpec((None, seq, LANES), lambda cb, b, p: (b, 0, goff + cb), pipeline_mode=one),
            pl.BlockSpec((3, LANES), lambda cb, b, p: (0, zoff + cb)),
            pl.BlockSpec((3, LANES), lambda cb, b, p: (0, goff + cb)),
            pl.BlockSpec((1, LANES), lambda cb, b, p: (0, cb)),
            pl.BlockSpec((nc, 2 * nb, nh), lambda cb, b, p: (jnp.minimum(p, ph - 1), 0, 0)),
            pl.BlockSpec((2 * FFT_COLS, 2 * FFT_COLS), lambda cb, b, p: (0, 0)),
            pl.BlockSpec((2 * FFT_COLS, 2 * FFT_COLS), lambda cb, b, p: (0, 0)),
            pl.BlockSpec((kc, 2 * FFT_COLS, LANES), lambda cb, b, p: (jnp.clip(p - ph, 0, ph - 1), 0, cb)),
            pl.BlockSpec((nc, nh, 2 * nb), lambda cb, b, p: (jnp.clip(p - 2 * ph, 0, ph - 1), 0, 0)),
        ],
        out_specs=pl.BlockSpec((None, seq, LANES), lambda cb, b, p: (b, 0, cb), pipeline_mode=one),
        out_shape=jax.ShapeDtypeStruct((bsz, seq, B_WIDTH), F32),
        scratch_shapes=[pltpu.VMEM((FFT_COLS * 2 * nb, LANES), F32)],
        compiler_params=_cparams(("parallel", "parallel", "arbitrary"), 56),
        name="hyena_conv",
    )(zsrc, gsrc, taps, taps, skip, g1_half, fc, fch, kf, g4_half)


def _out_proj_body(*refs, n_in):
    res_ref = refs[0]
    o_ref = refs[-1]
    acc = res_ref[...]
    for i in range(n_in):
        acc = acc + _dot(refs[1 + i][...].astype(BF16), refs[1 + n_in + i][...])
    o_ref[...] = acc


def _out_proj(res, acts, weights):
    t = res.shape[0]
    tm = TM_PROJ
    n_in = len(acts)
    in_specs = [pl.BlockSpec((tm, D_MODEL), lambda i: (i, 0))]
    in_specs += [pl.BlockSpec((tm, a.shape[1]), lambda i: (i, 0)) for a in acts]
    in_specs += [pl.BlockSpec(w.shape, lambda i: (0, 0)) for w in weights]
    return pl.pallas_call(
        functools.partial(_out_proj_body, n_in=n_in),
        grid=(t // tm,),
        in_specs=in_specs,
        out_specs=pl.BlockSpec((tm, D_MODEL), lambda i: (i, 0)),
        out_shape=jax.ShapeDtypeStruct((t, D_MODEL), F32),
        compiler_params=_cparams(("parallel",), 48),
        name="out_proj",
    )(res, *acts, *weights)


def _ffn_body(x_ref, g_ref, wg_ref, wu_ref, wd_ref, o_ref, h_ref, acc_ref):
    j = pl.program_id(1)

    @pl.when(j == 0)
    def _():
        x = x_ref[...]
        h_ref[...] = (_rms_scale(x) * g_ref[...]).astype(BF16)
        acc_ref[...] = x

    h = h_ref[...]
    a = (jax.nn.silu(_dot(h, wg_ref[...])) * _dot(h, wu_ref[...])).astype(BF16)
    acc_ref[...] += _dot(a, wd_ref[...])

    @pl.when(j == pl.num_programs(1) - 1)
    def _():
        o_ref[...] = acc_ref[...]


def _ffn(x2d, g, wg, wu, wd):
    t = x2d.shape[0]
    tm, tf = TM_FFN, TF_FFN
    return pl.pallas_call(
        _ffn_body,
        grid=(t // tm, D_FF // tf),
        in_specs=[
            pl.BlockSpec((tm, D_MODEL), lambda i, j: (i, 0)),
            pl.BlockSpec((1, D_MODEL), lambda i, j: (0, 0)),
            pl.BlockSpec((D_MODEL, tf), lambda i, j: (0, j)),
            pl.BlockSpec((D_MODEL, tf), lambda i, j: (0, j)),
            pl.BlockSpec((tf, D_MODEL), lambda i, j: (j, 0)),
        ],
        out_specs=pl.BlockSpec((tm, D_MODEL), lambda i, j: (i, 0)),
        out_shape=jax.ShapeDtypeStruct((t, D_MODEL), F32),
        scratch_shapes=[pltpu.VMEM((tm, D_MODEL), BF16), pltpu.VMEM((tm, D_MODEL), F32)],
        compiler_params=_cparams(("parallel", "arbitrary"), 56),
        name="ffn_swiglu",
    )(x2d, g, wg, wu, wd)


def _qkv_body(x_ref, g_ref, w_ref, o_ref):
    h = (_rms_scale(x_ref[...]) * g_ref[...]).astype(BF16)
    o_ref[...] = _dot(h, w_ref[...]).astype(o_ref.dtype)


def _qkv_proj(x2d, g, w):
    t = x2d.shape[0]
    tm = TM_PROJ
    n_out = w.shape[1]
    return pl.pallas_call(
        _qkv_body,
        grid=(t // tm,),
        in_specs=[
            pl.BlockSpec((tm, D_MODEL), lambda i: (i, 0)),
            pl.BlockSpec((1, D_MODEL), lambda i: (0, 0)),
            pl.BlockSpec((D_MODEL, n_out), lambda i: (0, 0)),
        ],
        out_specs=pl.BlockSpec((tm, n_out), lambda i: (i, 0)),
        out_shape=jax.ShapeDtypeStruct((t, n_out), BF16),
        compiler_params=_cparams(("parallel",), 48),
        name="qkv_proj",
    )(x2d, g, w)


def _t5_bucket(rel):
    nbk = REL_BUCKETS // 2
    max_exact = nbk // 2
    ret = jnp.where(rel > 0, nbk, 0)
    n = jnp.abs(rel)
    large = max_exact + (jnp.log(jnp.maximum(n, 1).astype(F32) / max_exact)
                         / math.log(REL_MAX_DIST / max_exact) * (nbk - max_exact)).astype(I32)
    large = jnp.minimum(large, nbk - 1)
    return ret + jnp.where(n < max_exact, n, large)


def _attn_body(sink_ref, q_ref, kp_ref, ko_ref, kn_ref, vp_ref, vo_ref, vn_ref, bias_ref, qg_ref, kg_ref,
               o_ref, *, seq):
    i = pl.program_id(1)
    kb = 3 * ATT_BLOCK
    qidx = lax.broadcasted_iota(I32, (ATT_BLOCK, kb), 0)
    kidx = lax.broadcasted_iota(I32, (ATT_BLOCK, kb), 1)
    rel = kidx - ATT_BLOCK - qidx
    kpos = (i - 1) * ATT_BLOCK + kidx
    mask = (jnp.abs(rel) <= WINDOW) & (kpos >= 0) & (kpos < seq)
    k_all = jnp.concatenate([kp_ref[...], ko_ref[...], kn_ref[...]], axis=0).astype(F32)
    v_all = jnp.concatenate([vp_ref[...], vo_ref[...], vn_ref[...]], axis=0)
    q_all = q_ref[...].astype(F32)
    scale = HEAD_DIM ** -0.5
    for hk in range(N_KV_HEADS):
        ks = slice(hk * HEAD_DIM, (hk + 1) * HEAD_DIM)
        khn = (_rms_scale(k_all[:, ks]) * kg_ref[...]).astype(BF16)
        vh = v_all[:, ks]
        for gq in range(GQA_GROUP):
            head = hk * GQA_GROUP + gq
            qs = slice(head * HEAD_DIM, (head + 1) * HEAD_DIM)
            qn = (_rms_scale(q_all[:, qs]) * qg_ref[...] * scale).astype(BF16)
            s = lax.dot_general(qn, khn, (((1,), (1,)), ((), ())), preferred_element_type=F32)
            s = jnp.where(mask, s + bias_ref[head], NEG_INF)
            sk = sink_ref[head]
            m = jnp.maximum(jnp.max(s, axis=-1, keepdims=True), sk)
            pexp = jnp.exp(s - m)
            den = jnp.sum(pexp, axis=-1, keepdims=True) + jnp.exp(sk - m)
            o = _dot(pexp.astype(BF16), vh) / den
            o_ref[:, qs] = o.astype(o_ref.dtype)


def _attention(qkv, sink, rel_bias, q_g, k_g):
    bsz, seq, _ = qkv.shape
    nblk = seq // ATT_BLOCK
    kb = 3 * ATT_BLOCK
    rel = jnp.arange(kb)[None, :] - ATT_BLOCK - jnp.arange(ATT_BLOCK)[:, None]
    bucket = _t5_bucket(rel)
    rb = rel_bias.astype(F32)
    bias = sum(jnp.where(bucket[None] == b, rb[b][:, None, None], 0.0) for b in range(REL_BUCKETS))
    kvw = N_KV_HEADS * HEAD_DIM
    kcol = N_HEADS * HEAD_DIM // kvw
    vcol = kcol + 1
    prev = lambda b, i: jnp.maximum(i - 1, 0)
    nxt = lambda b, i: jnp.minimum(i + 1, nblk - 1)
    kv_spec = lambda col, fn: pl.BlockSpec((None, ATT_BLOCK, kvw), lambda b, i: (b, fn(b, i), col))
    own = lambda b, i: i
    return pl.pallas_call(
        functools.partial(_attn_body, seq=seq),
        grid=(bsz, nblk),
        in_specs=[
            pl.BlockSpec(memory_space=pltpu.SMEM),
            pl.BlockSpec((None, ATT_BLOCK, N_HEADS * HEAD_DIM), lambda b, i: (b, i, 0)),
            kv_spec(kcol, prev), kv_spec(kcol, own), kv_spec(kcol, nxt),
            kv_spec(vcol, prev), kv_spec(vcol, own), kv_spec(vcol, nxt),
            pl.BlockSpec((N_HEADS, ATT_BLOCK, kb), lambda b, i: (0, 0, 0)),
            pl.BlockSpec((1, HEAD_DIM), lambda b, i: (0, 0)),
            pl.BlockSpec((1, HEAD_DIM), lambda b, i: (0, 0)),
        ],
        out_specs=pl.BlockSpec((None, ATT_BLOCK, N_HEADS * HEAD_DIM), lambda b, i: (b, i, 0)),
        out_shape=jax.ShapeDtypeStruct((bsz, seq, N_HEADS * HEAD_DIM), BF16),
        compiler_params=_cparams(("parallel", "parallel"), 48),
        name="window_attention",
    )(sink.astype(F32), qkv, qkv, qkv, qkv, qkv, qkv, qkv, bias, q_g.astype(F32)[None, :], k_g.astype(F32)[None, :])


def _router_body(x_ref, g_ref, wr_ref, tri_ref, hp_ref, idx_ref, rank_ref, gate_ref, cnt_ref, run_ref):
    @pl.when(pl.program_id(0) == 0)
    def _():
        run_ref[...] = jnp.zeros_like(run_ref)

    h = _rms_scale(x_ref[...]) * g_ref[...]
    half = D_MODEL // 2
    bits = lax.bitcast_convert_type(h.astype(BF16).astype(F32), U32)
    hp_ref[...] = (bits[:, half:] & jnp.uint32(0xFFFF0000)) | (bits[:, :half] >> 16)

    logits = lax.dot_general(wr_ref[...], h, (((1,), (1,)), ((), ())),
                             precision=lax.Precision.HIGHEST, preferred_element_type=F32)
    eid = lax.broadcasted_iota(I32, logits.shape, 0)
    m1 = jnp.max(logits, axis=0, keepdims=True)
    i1 = jnp.min(jnp.where(logits == m1, eid, N_EXPERTS), axis=0, keepdims=True)
    rest = jnp.where(eid == i1, -jnp.inf, logits)
    m2 = jnp.max(rest, axis=0, keepdims=True)
    i2 = jnp.min(jnp.where(rest == m2, eid, N_EXPERTS), axis=0, keepdims=True)
    e2 = jnp.exp(m2 - m1)
    gate_ref[0:1, :] = 1.0 / (1.0 + e2)
    gate_ref[1:2, :] = e2 / (1.0 + e2)
    idx_ref[0:1, :] = i1
    idx_ref[1:2, :] = i2

    sel1 = eid == i1
    sel2 = eid == i2
    onehot = jnp.where(sel1 | sel2, 1.0, 0.0)
    before = _dot(onehot.astype(BF16), tri_ref[...]) + run_ref[:, 0:1]
    rank_ref[0:1, :] = jnp.sum(jnp.where(sel1, before, 0.0), axis=0, keepdims=True).astype(I32)
    rank_ref[1:2, :] = jnp.sum(jnp.where(sel2, before, 0.0), axis=0, keepdims=True).astype(I32)
    run_ref[...] += jnp.sum(onehot, axis=1, keepdims=True)
    cnt_ref[...] = run_ref[...].astype(I32)


def _router(x2d, g, w_router):
    t = x2d.shape[0]
    tm = TM_ROUTER
    tri = (jnp.arange(tm)[:, None] < jnp.arange(tm)[None, :]).astype(BF16)
    two = lambda dt: jax.ShapeDtypeStruct((2, t), dt)
    return pl.pallas_call(
        _router_body,
        grid=(t // tm,),
        in_specs=[
            pl.BlockSpec((tm, D_MODEL), lambda i: (i, 0)),
            pl.BlockSpec((1, D_MODEL), lambda i: (0, 0)),
            pl.BlockSpec((N_EXPERTS, D_MODEL), lambda i: (0, 0)),
            pl.BlockSpec((tm, tm), lambda i: (0, 0)),
        ],
        out_specs=[
            pl.BlockSpec((tm, D_MODEL // 2), lambda i: (i, 0)),
            pl.BlockSpec((2, tm), lambda i: (0, i)),
            pl.BlockSpec((2, tm), lambda i: (0, i)),
            pl.BlockSpec((2, tm), lambda i: (0, i)),
            pl.BlockSpec((N_EXPERTS, LANES), lambda i: (0, 0)),
        ],
        out_shape=[
            jax.ShapeDtypeStruct((t, D_MODEL // 2), U32),
            two(I32), two(I32), two(F32),
            jax.ShapeDtypeStruct((N_EXPERTS, LANES), I32),
        ],
        scratch_shapes=[pltpu.VMEM((N_EXPERTS, LANES), F32)],
        compiler_params=_cparams(("arbitrary",), 48),
        name="moe_router",
    )(x2d, g, w_router.astype(F32).T, tri)


def _expert_body(te_ref, nu_ref, tok_cur_ref, tok_nxt_ref, hp_ref, wg_ref, wu_ref, wd_ref, o_ref,
                 xs_ref, xb_ref, acc_ref, sem):
    i = pl.program_id(0)
    j = pl.program_id(1)
    half = D_MODEL // 2
    rows = xb_ref.shape[0]
    n_used = nu_ref[0]
    slot = i % 2

    def gather(tok_ref, dst_slot):
        def issue(r, c):
            pltpu.make_async_copy(hp_ref.at[pl.ds(tok_ref[0, 0, r], 1)], xs_ref.at[dst_slot, pl.ds(r, 1)],
                                  sem.at[dst_slot]).start()
            return c
        lax.fori_loop(0, rows, issue, 0, unroll=8)

    @pl.when(jnp.logical_and(j == 0, i < n_used))
    def _():
        @pl.when(i == 0)
        def _():
            gather(tok_cur_ref, 0)

        pltpu.make_async_copy(hp_ref.at[pl.ds(0, rows)], xs_ref.at[slot], sem.at[slot]).wait()
        w = xs_ref[slot]
        xb_ref[:, :half] = lax.bitcast_convert_type(w << 16, F32).astype(BF16)
        xb_ref[:, half:] = lax.bitcast_convert_type(w & jnp.uint32(0xFFFF0000), F32).astype(BF16)

        @pl.when(i + 1 < n_used)
        def _():
            gather(tok_nxt_ref, 1 - slot)

    @pl.when(j == 0)
    def _():
        acc_ref[...] = jnp.zeros_like(acc_ref)

    @pl.when(i < n_used)
    def _():
        xb = xb_ref[...]
        a = (jax.nn.silu(_dot(xb, wg_ref[...])) * _dot(xb, wu_ref[...])).astype(BF16)
        acc_ref[...] += _dot(a, wd_ref[...])

    @pl.when(j == pl.num_programs(1) - 1)
    def _():
        o_ref[...] = acc_ref[...]


def _experts(hp, tok, tile_expert, n_used, wg, wu, wd):
    tm, tf = TM_MOE, TF_MOE
    n_tiles = tok.shape[0] // tm
    tok3 = tok.reshape(n_tiles, 1, tm)
    grid_spec = pltpu.PrefetchScalarGridSpec(
        num_scalar_prefetch=2,
        grid=(n_tiles, D_FF_EXPERT // tf),
        in_specs=[
            pl.BlockSpec((1, 1, tm), lambda i, j, te, nu: (i, 0, 0), memory_space=pltpu.SMEM),
            pl.BlockSpec((1, 1, tm), lambda i, j, te, nu: (jnp.minimum(i + 1, n_tiles - 1), 0, 0),
                         memory_space=pltpu.SMEM),
            pl.BlockSpec(memory_space=pl.ANY),
            pl.BlockSpec((None, D_MODEL, tf), lambda i, j, te, nu: (te[i], 0, j)),
            pl.BlockSpec((None, D_MODEL, tf), lambda i, j, te, nu: (te[i], 0, j)),
            pl.BlockSpec((None, tf, D_MODEL), lambda i, j, te, nu: (te[i], j, 0)),
        ],
        out_specs=pl.BlockSpec((tm, D_MODEL), lambda i, j, te, nu: (i, 0)),
        scratch_shapes=[pltpu.VMEM((2, tm, D_MODEL // 2), U32), pltpu.VMEM((tm, D_MODEL), BF16),
                        pltpu.VMEM((tm, D_MODEL), F32), pltpu.SemaphoreType.DMA((2,))],
    )
    return pl.pallas_call(
        _expert_body,
        grid_spec=grid_spec,
        out_shape=jax.ShapeDtypeStruct((n_tiles * tm, D_MODEL), F32),
        compiler_params=_cparams(("arbitrary", "arbitrary"), 56),
        name="moe_experts",
    )(tile_expert, n_used, tok3, tok3, hp, wg, wu, wd)


def _combine_body(d1_ref, d2_ref, x_ref, g1_ref, g2_ref, ys_ref, o_ref, y1_ref, y2_ref, sem, *, rows):
    def issue(r, c):
        pltpu.make_async_copy(ys_ref.at[pl.ds(d1_ref[0, 0, r], 1)], y1_ref.at[pl.ds(r, 1)], sem.at[0]).start()
        pltpu.make_async_copy(ys_ref.at[pl.ds(d2_ref[0, 0, r], 1)], y2_ref.at[pl.ds(r, 1)], sem.at[1]).start()
        return c
    lax.fori_loop(0, rows, issue, 0, unroll=8)
    pltpu.make_async_copy(ys_ref.at[pl.ds(0, rows)], y1_ref, sem.at[0]).wait()
    pltpu.make_async_copy(ys_ref.at[pl.ds(0, rows)], y2_ref, sem.at[1]).wait()
    o_ref[...] = x_ref[...] + g1_ref[...] * y1_ref[...] + g2_ref[...] * y2_ref[...]


def _combine(x2d, ys, dest, gates):
    t = x2d.shape[0]
    rows = TM_COMBINE
    idx_spec = pl.BlockSpec((1, 1, rows), lambda i: (i, 0, 0), memory_space=pltpu.SMEM)
    gate_spec = pl.BlockSpec((rows, 1), lambda i: (i, 0))
    row_spec = pl.BlockSpec((rows, D_MODEL), lambda i: (i, 0))
    return pl.pallas_call(
        functools.partial(_combine_body, rows=rows),
        grid=(t // rows,),
        in_specs=[idx_spec, idx_spec, row_spec, gate_spec, gate_spec, pl.BlockSpec(memory_space=pl.ANY)],
        out_specs=row_spec,
        out_shape=jax.ShapeDtypeStruct((t, D_MODEL), F32),
        scratch_shapes=[pltpu.VMEM((rows, D_MODEL), F32), pltpu.VMEM((rows, D_MODEL), F32),
                        pltpu.SemaphoreType.DMA((2,))],
        compiler_params=_cparams(("arbitrary",), 48),
        name="moe_combine",
    )(dest[0].reshape(t // rows, 1, rows), dest[1].reshape(t // rows, 1, rows), x2d,
      gates[0][:, None], gates[1][:, None], ys)


def _moe(x2d, g, w_router, wg, wu, wd):
    t = x2d.shape[0]
    tm = TM_MOE
    hp, idx, rank, gates, cnt = _router(x2d, g, w_router)
    counts = cnt[:, 0]
    tiles = (counts + tm - 1) // tm
    tile_end = jnp.cumsum(tiles)
    row_start = (tile_end - tiles) * tm
    n_tiles = (2 * t) // tm + N_EXPERTS
    eid = jnp.arange(N_EXPERTS, dtype=I32)[:, None, None]
    dest = jnp.sum(jnp.where(idx[None] == eid, row_start[:, None, None], 0), axis=0) + rank
    tile_expert = jnp.minimum(jnp.searchsorted(tile_end, jnp.arange(n_tiles), side="right"), N_EXPERTS - 1)
    tok = jnp.zeros((n_tiles * tm,), I32).at[dest.reshape(-1)].set(jnp.tile(jnp.arange(t, dtype=I32), 2))
    ys = _experts(hp, tok, tile_expert.astype(I32), tile_end[-1:].astype(I32), wg, wu, wd)
    return _combine(x2d, ys, dest, gates)


def _even_layer(x, norm_mix, norm_ffn, w_in, w_out, ln_g, ln_b, sgu_w, sgu_b, hy_conv,
                f_w1, f_b1, f_w2, f_b2, f_w3, f_b3, f_freq, hy_skip, wg, wu, wd):
    bsz, seq, _ = x.shape
    x2d = x.reshape(bsz * seq, D_MODEL)
    sgu_bb = jnp.broadcast_to(sgu_b.astype(F32)[:, :, None], (A_GROUPS, CHUNK, LANES))
    ya, hb = _even_in(x2d, norm_mix[None, :], w_in.astype(BF16), ln_g[None, :], ln_b[None, :],
                      sgu_w.astype(BF16), sgu_bb)
    hb = hb.reshape(bsz, seq, -1)

    filt = _hyena_filters(seq, f_w1, f_b1, f_w2, f_b2, f_w3, f_b3, f_freq)
    filt = filt.reshape(seq, 2, HYENA_ORDER, B_WIDTH)
    h_fwd, h_bwd = filt[:, 0], filt[:, 1]
    kfull = jnp.concatenate([h_fwd, jnp.zeros_like(h_fwd[:1]), h_bwd[1:][::-1]], axis=0)
    kfull = kfull.transpose(1, 0, 2)
    tables = _dft_tables(seq)
    kf = _filter_spectrum(kfull, tables[0], tables[3])

    taps = hy_conv.astype(F32)
    skip = hy_skip.astype(F32)
    ncb = B_WIDTH // LANES
    z1 = _hyena_conv(hb, 0, hb, ncb, taps, skip[0:1], kf[0], tables, conv_z=True)
    yb = _hyena_conv(z1, 0, hb, 2 * ncb, taps, skip[1:2], kf[1], tables, conv_z=False)

    w_out = w_out.astype(BF16)
    x2d = _out_proj(x2d, [ya, yb.reshape(bsz * seq, B_WIDTH)], [w_out[:A_WIDTH], w_out[A_WIDTH:]])
    x2d = _ffn(x2d, norm_ffn[None, :], wg.astype(BF16), wu.astype(BF16), wd.astype(BF16))
    return x2d.reshape(bsz, seq, D_MODEL)


def _odd_layer(x, norm_mix, norm_ffn, w_qkv, q_g, k_g, sink, w_out, rel_bias, w_router, wg, wu, wd):
    bsz, seq, _ = x.shape
    x2d = x.reshape(bsz * seq, D_MODEL)
    qkv = _qkv_proj(x2d, norm_mix[None, :], w_qkv.astype(BF16)).reshape(bsz, seq, -1)
    o = _attention(qkv, sink, rel_bias, q_g, k_g)
    x2d = _out_proj(x2d, [o.reshape(bsz * seq, -1)], [w_out.astype(BF16)])
    x2d = _moe(x2d, norm_ffn[None, :], w_router, wg.astype(BF16), wu.astype(BF16), wd.astype(BF16))
    return x2d.reshape(bsz, seq, D_MODEL)


def _trunk(x, norm_mix, norm_ffn, even_p, odd_p, rel_bias):
    depth = norm_mix.shape[0]
    for i in range(depth):
        j = i // 2
        if i % 2 == 0:
            x = _even_layer(x, norm_mix[i], norm_ffn[i], *[p[j] for p in even_p])
        else:
            x = _odd_layer(x, norm_mix[i], norm_ffn[i], *[p[j] for p in odd_p[:5]], rel_bias,
                           *[p[j] for p in odd_p[5:]])
    return x


def kernel(x_prompt, x_sample, norm_mix, norm_ffn, ev_w_in, ev_w_out, sgu_ln_g, sgu_ln_b, sgu_w, sgu_b,
           hy_conv, hy_f_w1, hy_f_b1, hy_f_w2, hy_f_b2, hy_f_w3, hy_f_b3, hy_f_freq, hy_skip,
           ffn_w_gate, ffn_w_up, ffn_w_down, at_w_qkv, at_q_norm, at_k_norm, at_sink, at_w_out,
           rel_bias, moe_router, moe_w_gate, moe_w_up, moe_w_down):
    even_p = (ev_w_in, ev_w_out, sgu_ln_g, sgu_ln_b, sgu_w, sgu_b, hy_conv, hy_f_w1, hy_f_b1, hy_f_w2,
              hy_f_b2, hy_f_w3, hy_f_b3, hy_f_freq, hy_skip, ffn_w_gate, ffn_w_up, ffn_w_down)
    odd_p = (at_w_qkv, at_q_norm, at_k_norm, at_sink, at_w_out, moe_router, moe_w_gate, moe_w_up, moe_w_down)
    assert x_prompt.shape[1:] == x_sample.shape[1:]
    nb_prompt = x_prompt.shape[0]
    x = jnp.concatenate([x_prompt, x_sample], axis=0)
    y = _trunk(x, norm_mix, norm_ffn, even_p, odd_p, rel_bias)
    return (y[:nb_prompt], y[nb_prompt:])
```

```python
import functools
import math

import jax
import jax.numpy as jnp
from jax import lax
from jax.experimental import pallas as pl
from jax.experimental.pallas import tpu as pltpu

F32 = jnp.float32
BF16 = jnp.bfloat16
U32 = jnp.uint32
I32 = jnp.int32

D_MODEL = 1024
A_GROUPS = 4
A_WIDTH = D_MODEL // 2
CHUNK = 128
B_WIDTH = D_MODEL // 2
HYENA_ORDER = 2
FILTER_DIRS = 2
FILTER_BANDS = 16
FILTER_EMB = 1 + 2 * FILTER_BANDS
FILTER_HIDDEN = 64
DECAY_TARGET = 1e-2
FAST_DECAY_PCT = 0.3
SLOW_DECAY_PCT = 1.5
HEAD_DIM = 64
N_HEADS = D_MODEL // HEAD_DIM
N_KV_HEADS = N_HEADS // 4
GQA_GROUP = N_HEADS // N_KV_HEADS
WINDOW = 128
ATT_BLOCK = 128
REL_BUCKETS = 32
REL_MAX_DIST = 128
NEG_INF = -1e30
D_FF = 2816
N_EXPERTS = 8
D_FF_EXPERT = 3584
EPS = 1e-6

LANES = 128
MIB = 1024 * 1024

TM_PROJ = 512
TM_FFN = 512
TF_FFN = D_FF // 2
TM_MOE = 512
TF_MOE = D_FF_EXPERT // 2
TM_ROUTER = 512
TM_COMBINE = 256
HY_CB = 32
HY_PAIRS = 2

NT_DIMS = (((1,), (1,)), ((), ()))
TN_DIMS = (((0,), (0,)), ((), ()))


def _cparams(sem, vmem_mib):
    return pltpu.CompilerParams(dimension_semantics=sem, vmem_limit_bytes=vmem_mib * MIB)


def _rms_scale(x):
    return x * lax.rsqrt(jnp.mean(x * x, axis=-1, keepdims=True) + EPS)


def _dot(a, b):
    return jnp.dot(a, b, preferred_element_type=F32)


def _even_in_body(x_ref, g_ref, w_ref, wht_ref, lng_ref, lnb_ref, sw_ref, sb_ref, ya_ref, hbt_ref):
    x = x_ref[...]
    h = (_rms_scale(x) * g_ref[...]).astype(BF16)
    u = jax.nn.gelu(_dot(h, w_ref[:, 0:A_WIDTH]))
    v = jax.nn.gelu(_dot(h, w_ref[:, A_WIDTH:2 * A_WIDTH]))
    hbt_ref[...] = lax.dot_general(wht_ref[...], h, NT_DIMS, preferred_element_type=F32)
    tm = x.shape[0]
    for gi in range(A_GROUPS):
        cs = slice(gi * LANES, (gi + 1) * LANES)
        vg = v[:, cs]
        xc = vg - jnp.mean(vg, axis=-1, keepdims=True)
        var = jnp.mean(xc * xc, axis=-1, keepdims=True)
        vn = (xc * lax.rsqrt(var + EPS) * lng_ref[:, cs] + lnb_ref[:, cs]).astype(BF16)
        for c in range(tm // CHUNK):
            rs = slice(c * CHUNK, (c + 1) * CHUNK)
            mixed = _dot(sw_ref[gi], vn[rs]) + sb_ref[gi]
            ya_ref[rs, cs] = (u[rs, cs] * mixed).astype(ya_ref.dtype)


def _even_in(x, g, w_uv, w_hb_t, ln_g, ln_b, sgu_w, sgu_b):
    bsz, seq, _ = x.shape
    tm = TM_PROJ
    n_hb = w_hb_t.shape[0]
    const = lambda *shape: pl.BlockSpec(shape, lambda b, i: (0,) * len(shape))
    return pl.pallas_call(
        _even_in_body,
        grid=(bsz, seq // tm),
        in_specs=[
            pl.BlockSpec((None, tm, D_MODEL), lambda b, i: (b, i, 0)),
            const(1, D_MODEL),
            const(D_MODEL, 2 * A_WIDTH),
            const(n_hb, D_MODEL),
            const(1, A_WIDTH),
            const(1, A_WIDTH),
            const(A_GROUPS, CHUNK, CHUNK),
            const(A_GROUPS, CHUNK, LANES),
        ],
        out_specs=[
            pl.BlockSpec((None, tm, A_WIDTH), lambda b, i: (b, i, 0)),
            pl.BlockSpec((None, n_hb, tm), lambda b, i: (b, 0, i)),
        ],
        out_shape=[
            jax.ShapeDtypeStruct((bsz, seq, A_WIDTH), BF16),
            jax.ShapeDtypeStruct((bsz, n_hb, seq), F32),
        ],
        compiler_params=_cparams(("parallel", "parallel"), 48),
        name="even_in",
    )(x, g, w_uv, w_hb_t, ln_g, ln_b, sgu_w, sgu_b)


def _filter_body(ft_ref, t_ref, w1_ref, b1_ref, w2_ref, b2_ref, w3t_ref, b3_ref, fr0_ref, fr1_ref, absd_ref,
                 o_ref, *, seq):
    hp = lax.Precision.HIGHEST
    tl = ft_ref.shape[0]
    h = jnp.sin(fr0_ref[...] * (jnp.dot(ft_ref[...], w1_ref[...], precision=hp, preferred_element_type=F32)
                                + b1_ref[...]))
    h = jnp.sin(fr1_ref[...] * (jnp.dot(h, w2_ref[...], precision=hp, preferred_element_type=F32) + b2_ref[...]))
    out = lax.dot_general(w3t_ref[...], h, NT_DIMS, precision=hp, preferred_element_type=F32) + b3_ref[...]
    decay = jnp.exp(-absd_ref[...] * t_ref[...])
    pos = pl.program_id(0) * tl + lax.broadcasted_iota(I32, (1, tl), 1)
    for o in range(HYENA_ORDER):
        o_ref[o] = jnp.where(pos == seq, 0.0, out[o * B_WIDTH:(o + 1) * B_WIDTH] * decay)


def _hyena_kernels(seq, w1, b1, w2, b2, w3, b3, freq):
    t = jnp.linspace(0.0, 1.0, seq, dtype=F32)[:, None]
    w = 2.0 * math.pi * jnp.arange(seq, dtype=F32)[:, None] / seq
    bands = jnp.linspace(1e-4, FILTER_BANDS - 1, FILTER_BANDS, dtype=F32)[None, :]
    feats = jnp.concatenate([t, jnp.cos(bands * w), jnp.sin(bands * w)], axis=-1)
    feats = jnp.concatenate([feats, feats[:1], feats[:0:-1]], axis=0)
    t_row = feats[:, 0][None, :]
    feats = jnp.pad(feats, ((0, 0), (0, LANES - FILTER_EMB)))
    hpad = LANES - FILTER_HIDDEN
    w1p = jnp.pad(w1.astype(F32), ((0, LANES - FILTER_EMB), (0, hpad)))
    w2p = jnp.pad(w2.astype(F32), ((0, hpad), (0, hpad)))
    n_dir = HYENA_ORDER * B_WIDTH
    w3t = jnp.pad(w3.astype(F32), ((0, hpad), (0, 0))).T.reshape(FILTER_DIRS, n_dir, LANES)
    b3c = b3.astype(F32).reshape(FILTER_DIRS, n_dir, 1)
    b1p = jnp.pad(b1.astype(F32), (0, hpad))[None, :]
    b2p = jnp.pad(b2.astype(F32), (0, hpad))[None, :]
    fr = jnp.pad(freq.astype(F32), ((0, 0), (0, hpad)))
    deltas = jnp.linspace(math.log(DECAY_TARGET) / SLOW_DECAY_PCT,
                          math.log(DECAY_TARGET) / FAST_DECAY_PCT, B_WIDTH, dtype=F32)
    absd = jnp.abs(deltas)[:, None]
    tl = min(seq, 1024)
    nhalf = seq // tl
    const = lambda *shape: pl.BlockSpec(shape, lambda i: (0,) * len(shape))
    return pl.pallas_call(
        functools.partial(_filter_body, seq=seq),
        grid=(2 * nhalf,),
        in_specs=[
            pl.BlockSpec((tl, LANES), lambda i: (i, 0)),
            pl.BlockSpec((1, tl), lambda i: (0, i)),
            const(LANES, LANES), const(1, LANES), const(LANES, LANES), const(1, LANES),
            pl.BlockSpec((None, n_dir, LANES), lambda i: (i // nhalf, 0, 0)),
            pl.BlockSpec((None, n_dir, 1), lambda i: (i // nhalf, 0, 0)),
            const(1, LANES), const(1, LANES), const(B_WIDTH, 1),
        ],
        out_specs=pl.BlockSpec((HYENA_ORDER, B_WIDTH, tl), lambda i: (0, 0, i)),
        out_shape=jax.ShapeDtypeStruct((HYENA_ORDER, B_WIDTH, 2 * seq), F32),
        compiler_params=_cparams(("parallel",), 48),
        name="hyena_filter",
    )(feats, t_row, w1p, b1p, w2p, b2p, w3t, b3c, fr[0:1], fr[1:2], absd)


def _dft_tables(seq):
    n = 2 * seq
    nb = n // LANES
    k1 = jnp.arange(nb, dtype=I32)[:, None]
    ang1 = (2.0 * math.pi / nb) * ((k1 * jnp.arange(nb, dtype=I32)[None, :]) % nb).astype(F32)
    f1_full = jnp.concatenate([jnp.cos(ang1), -jnp.sin(ang1)], axis=0)
    f1_half = f1_full[:, : nb // 2]
    f4_half = f1_half.T
    angt = (2.0 * math.pi / n) * ((k1 * jnp.arange(LANES, dtype=I32)[None, :]) % n).astype(F32)
    twr, twi = jnp.cos(angt), -jnp.sin(angt)
    a = jnp.arange(LANES, dtype=I32)
    ang2 = (2.0 * math.pi / LANES) * ((a[:, None] * a[None, :]) % LANES).astype(F32)
    cr, ci = jnp.cos(ang2), -jnp.sin(ang2)
    m2 = jnp.concatenate([jnp.concatenate([cr, ci], axis=1), jnp.concatenate([-ci, cr], axis=1)], axis=0)
    m3 = jnp.concatenate([jnp.concatenate([cr, -ci], axis=1), jnp.concatenate([ci, cr], axis=1)], axis=0)
    return dict(f1_full=f1_full.astype(BF16), f1_half=f1_half.astype(BF16), f4_half=f4_half.astype(BF16),
                twr=twr, twi=twi, m2=m2.astype(BF16), m3=m3.astype(BF16))


def _fwd_spectrum(pairs, f1_ref, twr_ref, twi_ref, m2_ref):
    nb = twr_ref.shape[0]
    a_all = [_dot(f1_ref[...], jnp.concatenate(xs, axis=1).astype(BF16)) for xs in pairs]
    twr, twi = twr_ref[...], twi_ref[...]
    out = []
    for a in a_all:
        lhs = []
        for d in range(2):
            ar = a[:nb, d * LANES:(d + 1) * LANES]
            ai = a[nb:, d * LANES:(d + 1) * LANES]
            lhs.append(jnp.concatenate([ar * twr - ai * twi, ar * twi + ai * twr], axis=1))
        out.append(_dot(jnp.concatenate(lhs, axis=0).astype(BF16), m2_ref[...]))
    return out


def _spec_body(k_ref, f1_ref, twr_ref, twi_ref, m2_ref, o_ref):
    nb = twr_ref.shape[0]
    inv_n = 1.0 / (nb * LANES)

    def group(it, carry):
        c0 = 2 * HY_PAIRS * it
        pairs = [[k_ref[c0 + 2 * g], k_ref[c0 + 2 * g + 1]] for g in range(HY_PAIRS)]
        for g, z in enumerate(_fwd_spectrum(pairs, f1_ref, twr_ref, twi_ref, m2_ref)):
            o_ref[c0 + 2 * g] = z[:nb] * inv_n
            o_ref[c0 + 2 * g + 1] = z[nb:] * inv_n
        return carry
    lax.fori_loop(0, k_ref.shape[0] // (2 * HY_PAIRS), group, 0)


def _filter_spectrum(kt, tb):
    orders, c, nb, _ = kt.shape
    cb = HY_CB
    const = lambda *shape: pl.BlockSpec(shape, lambda o, j: (0,) * len(shape))
    return pl.pallas_call(
        _spec_body,
        grid=(orders, c // cb),
        in_specs=[
            pl.BlockSpec((None, cb, nb, LANES), lambda o, j: (o, j, 0, 0)),
            const(2 * nb, nb), const(nb, LANES), const(nb, LANES), const(2 * LANES, 2 * LANES),
        ],
        out_specs=pl.BlockSpec((None, cb, nb, 2 * LANES), lambda o, j: (o, j, 0, 0)),
        out_shape=jax.ShapeDtypeStruct((orders, c, nb, 2 * LANES), F32),
        compiler_params=_cparams(("parallel", "parallel"), 48),
        name="hyena_filter_spectrum",
    )(kt, tb["f1_full"], tb["twr"], tb["twi"], tb["m2"])


def _shift_rows(x, down):
    rows = x.shape[0]
    idx = lax.broadcasted_iota(I32, x.shape, 0)
    if down:
        return jnp.where(idx == 0, 0.0, pltpu.roll(x, 1, axis=0))
    return jnp.where(idx == rows - 1, 0.0, pltpu.roll(x, rows - 1, axis=0))


def _short_conv(x, taps_ref, ch):
    lane = lax.broadcasted_iota(I32, x.shape, 1)
    prev = pltpu.roll(jnp.where(lane == LANES - 1, _shift_rows(x, True), x), 1, axis=1)
    nxt = pltpu.roll(jnp.where(lane == 0, _shift_rows(x, False), x), LANES - 1, axis=1)
    return taps_ref[0, ch] * prev + taps_ref[1, ch] * x + taps_ref[2, ch] * nxt


def _conv_body(taps_ref, skip_ref, z_ref, g_ref, kf_ref, f1_ref, f4_ref, twr_ref, twi_ref, m2_ref, m3_ref,
               o_ref, *, zch, gch, conv_z):
    nb = twr_ref.shape[0]
    cb = z_ref.shape[0]
    base = pl.program_id(0) * cb

    def group(it, carry):
        c0 = 2 * HY_PAIRS * it
        chans = [[c0 + 2 * g, c0 + 2 * g + 1] for g in range(HY_PAIRS)]
        xs = [[_short_conv(z_ref[c], taps_ref, zch + base + c) if conv_z else z_ref[c] for c in pr] for pr in chans]
        zs = _fwd_spectrum(xs, f1_ref, twr_ref, twi_ref, m2_ref)
        ccs = []
        for pr, z in zip(chans, zs):
            ys = []
            for d, c in enumerate(pr):
                zr = z[d * nb:(d + 1) * nb, :LANES]
                zi = z[d * nb:(d + 1) * nb, LANES:]
                kf = kf_ref[c]
                kr, ki = kf[:, :LANES], kf[:, LANES:]
                ys.append(jnp.concatenate([zr * kr - zi * ki, zr * ki + zi * kr], axis=1))
            ccs.append(_dot(jnp.concatenate(ys, axis=0).astype(BF16), m3_ref[...]))
        twr, twi = twr_ref[...], twi_ref[...]
        for pr, x2, cc in zip(chans, xs, ccs):
            drs, dis = [], []
            for d in range(2):
                ccr = cc[d * nb:(d + 1) * nb, :LANES]
                cci = cc[d * nb:(d + 1) * nb, LANES:]
                drs.append(ccr * twr + cci * twi)
                dis.append(cci * twr - ccr * twi)
            rhs = jnp.concatenate([jnp.concatenate(drs, axis=1), jnp.concatenate(dis, axis=1)], axis=0)
            y = _dot(f4_ref[...], rhs.astype(BF16))
            for d, c in enumerate(pr):
                gate = _short_conv(g_ref[c], taps_ref, gch + base + c)
                o_ref[c] = gate * (y[:, d * LANES:(d + 1) * LANES] + skip_ref[base + c] * x2[d])
        return carry
    lax.fori_loop(0, cb // (2 * HY_PAIRS), group, 0)


def _hyena_conv(zsrc, zch, gsrc, gch, taps, skip, kf, tb, conv_z):
    bsz, _, nh, _ = zsrc.shape
    nb = 2 * nh
    cb = HY_CB
    smem = pl.BlockSpec(memory_space=pltpu.SMEM)
    const = lambda *shape: pl.BlockSpec(shape, lambda j, b: (0,) * len(shape))
    zblk, gblk = zch // cb, gch // cb
    return pl.pallas_call(
        functools.partial(_conv_body, zch=zch, gch=gch, conv_z=conv_z),
        grid=(B_WIDTH // cb, bsz),
        in_specs=[
            smem, smem,
            pl.BlockSpec((None, cb, nh, LANES), lambda j, b: (b, zblk + j, 0, 0)),
            pl.BlockSpec((None, cb, nh, LANES), lambda j, b: (b, gblk + j, 0, 0)),
            pl.BlockSpec((cb, nb, 2 * LANES), lambda j, b: (j, 0, 0)),
            const(2 * nb, nh), const(nh, 2 * nb), const(nb, LANES), const(nb, LANES),
            const(2 * LANES, 2 * LANES), const(2 * LANES, 2 * LANES),
        ],
        out_specs=pl.BlockSpec((None, cb, nh, LANES), lambda j, b: (b, j, 0, 0)),
        out_shape=jax.ShapeDtypeStruct((bsz, B_WIDTH, nh, LANES), F32),
        compiler_params=_cparams(("parallel", "parallel"), 48),
        name="hyena_conv",
    )(taps, skip, zsrc, gsrc, kf, tb["f1_half"], tb["f4_half"], tb["twr"], tb["twi"], tb["m2"], tb["m3"])


def _out_proj_body(*refs, with_t):
    if with_t:
        res_ref, a_ref, bt_ref, wa_ref, wb_ref, o_ref = refs
    else:
        res_ref, a_ref, wa_ref, o_ref = refs
    acc = res_ref[...] + _dot(a_ref[...].astype(BF16), wa_ref[...])
    if with_t:
        acc = acc + lax.dot_general(bt_ref[...].astype(BF16), wb_ref[...], TN_DIMS, preferred_element_type=F32)
    o_ref[...] = acc


def _out_proj(res, a, wa, bt=None, wb=None):
    bsz, seq, _ = res.shape
    tm = TM_PROJ
    with_t = bt is not None
    row = lambda width: pl.BlockSpec((None, tm, width), lambda b, i: (b, i, 0))
    full = lambda arr: pl.BlockSpec(arr.shape, lambda b, i: (0, 0))
    in_specs = [row(D_MODEL), row(a.shape[2])]
    args = [res, a]
    if with_t:
        in_specs.append(pl.BlockSpec((None, bt.shape[1], tm), lambda b, i: (b, 0, i)))
        args.append(bt)
    in_specs.append(full(wa))
    args.append(wa)
    if with_t:
        in_specs.append(full(wb))
        args.append(wb)
    return pl.pallas_call(
        functools.partial(_out_proj_body, with_t=with_t),
        grid=(bsz, seq // tm),
        in_specs=in_specs,
        out_specs=row(D_MODEL),
        out_shape=jax.ShapeDtypeStruct((bsz, seq, D_MODEL), F32),
        compiler_params=_cparams(("parallel", "parallel"), 48),
        name="out_proj",
    )(*args)


def _ffn_body(x_ref, g_ref, wg_ref, wu_ref, wd_ref, o_ref, h_ref, acc_ref):
    j = pl.program_id(1)

    @pl.when(j == 0)
    def _():
        x = x_ref[...]
        h_ref[...] = (_rms_scale(x) * g_ref[...]).astype(BF16)
        acc_ref[...] = x

    h = h_ref[...]
    a = (jax.nn.silu(_dot(h, wg_ref[...])) * _dot(h, wu_ref[...])).astype(BF16)
    acc_ref[...] += _dot(a, wd_ref[...])

    @pl.when(j == pl.num_programs(1) - 1)
    def _():
        o_ref[...] = acc_ref[...]


def _ffn(x2d, g, wg, wu, wd):
    t = x2d.shape[0]
    tm, tf = TM_FFN, TF_FFN
    return pl.pallas_call(
        _ffn_body,
        grid=(t // tm, D_FF // tf),
        in_specs=[
            pl.BlockSpec((tm, D_MODEL), lambda i, j: (i, 0)),
            pl.BlockSpec((1, D_MODEL), lambda i, j: (0, 0)),
            pl.BlockSpec((D_MODEL, tf), lambda i, j: (0, j)),
            pl.BlockSpec((D_MODEL, tf), lambda i, j: (0, j)),
            pl.BlockSpec((tf, D_MODEL), lambda i, j: (j, 0)),
        ],
        out_specs=pl.BlockSpec((tm, D_MODEL), lambda i, j: (i, 0)),
        out_shape=jax.ShapeDtypeStruct((t, D_MODEL), F32),
        scratch_shapes=[pltpu.VMEM((tm, D_MODEL), BF16), pltpu.VMEM((tm, D_MODEL), F32)],
        compiler_params=_cparams(("parallel", "arbitrary"), 56),
        name="ffn_swiglu",
    )(x2d, g, wg, wu, wd)


def _qkv_body(x_ref, g_ref, w_ref, qg_ref, kg_ref, q_ref, k_ref, v_ref):
    h = (_rms_scale(x_ref[...]) * g_ref[...]).astype(BF16)
    nq = N_HEADS * HEAD_DIM
    nk = N_KV_HEADS * LANES
    lo = lax.broadcasted_iota(I32, (1, LANES), 1) < HEAD_DIM
    qgain = qg_ref[...] * (HEAD_DIM ** -0.5)
    for c in range(nq // LANES):
        x = _dot(h, w_ref[:, c * LANES:(c + 1) * LANES])
        x2 = x * x
        s_lo = jnp.sum(jnp.where(lo, x2, 0.0), axis=-1, keepdims=True)
        s_hi = jnp.sum(jnp.where(lo, 0.0, x2), axis=-1, keepdims=True)
        r = jnp.where(lo, lax.rsqrt(s_lo / HEAD_DIM + EPS), lax.rsqrt(s_hi / HEAD_DIM + EPS))
        q_ref[:, c * LANES:(c + 1) * LANES] = (x * r * qgain).astype(q_ref.dtype)
    for c in range(N_KV_HEADS):
        x = _dot(h, w_ref[:, nq + c * LANES:nq + (c + 1) * LANES])
        k_ref[:, c * LANES:(c + 1) * LANES] = (_rms_scale(x) * kg_ref[...]).astype(k_ref.dtype)
    v_ref[...] = _dot(h, w_ref[:, nq + nk:]).astype(v_ref.dtype)


def _qkv_proj(x, g, w_qkv, q_g, k_g):
    bsz, seq, _ = x.shape
    tm = TM_PROJ
    nq = N_HEADS * HEAD_DIM
    nkv = N_KV_HEADS * HEAD_DIM
    dup = lambda w: jnp.tile(w.reshape(D_MODEL, N_KV_HEADS, 1, HEAD_DIM), (1, 1, 2, 1)).reshape(D_MODEL, 2 * nkv)
    w = jnp.concatenate([w_qkv[:, :nq], dup(w_qkv[:, nq:nq + nkv]), dup(w_qkv[:, nq + nkv:])], axis=1).astype(BF16)
    two = lambda v: jnp.tile(v.astype(F32), 2)[None, :]
    row = lambda width: pl.BlockSpec((None, tm, width), lambda b, i: (b, i, 0))
    const = lambda *shape: pl.BlockSpec(shape, lambda b, i: (0,) * len(shape))
    return pl.pallas_call(
        _qkv_body,
        grid=(bsz, seq // tm),
        in_specs=[row(D_MODEL), const(1, D_MODEL), const(D_MODEL, w.shape[1]), const(1, LANES), const(1, LANES)],
        out_specs=[row(nq), row(2 * nkv), row(2 * nkv)],
        out_shape=[jax.ShapeDtypeStruct((bsz, seq, nq), BF16), jax.ShapeDtypeStruct((bsz, seq, 2 * nkv), BF16),
                   jax.ShapeDtypeStruct((bsz, seq, 2 * nkv), BF16)],
        compiler_params=_cparams(("parallel", "parallel"), 48),
        name="qkv_proj",
    )(x, g, w, two(q_g), two(k_g))


def _t5_bucket(rel):
    nbk = REL_BUCKETS // 2
    max_exact = nbk // 2
    ret = jnp.where(rel > 0, nbk, 0)
    n = jnp.abs(rel)
    large = max_exact + (jnp.log(jnp.maximum(n, 1).astype(F32) / max_exact)
                         / math.log(REL_MAX_DIST / max_exact) * (nbk - max_exact)).astype(I32)
    large = jnp.minimum(large, nbk - 1)
    return ret + jnp.where(n < max_exact, n, large)


def _attn_body(sink_ref, q_ref, kp_ref, ko_ref, kn_ref, vp_ref, vo_ref, vn_ref, bm_ref, o_ref, *, seq):
    i = pl.program_id(1)
    kb = 3 * ATT_BLOCK
    kpos = (i - 1) * ATT_BLOCK + lax.broadcasted_iota(I32, (ATT_BLOCK, kb), 1)
    edge = jnp.where((kpos >= 0) & (kpos < seq), 0.0, NEG_INF)
    lo = lax.broadcasted_iota(I32, (1, LANES), 1) < HEAD_DIM
    first = lax.broadcasted_iota(I32, (2 * ATT_BLOCK, 1), 0) < ATT_BLOCK
    for hk in range(N_KV_HEADS):
        ks = slice(hk * LANES, (hk + 1) * LANES)
        kk = jnp.concatenate([kp_ref[:, ks], ko_ref[:, ks], kn_ref[:, ks]], axis=0)
        vv = jnp.concatenate([vp_ref[:, ks], vo_ref[:, ks], vn_ref[:, ks]], axis=0)
        zero = jnp.zeros_like(kk)
        c0 = 2 * hk
        ql = jnp.concatenate([q_ref[:, c0 * LANES:(c0 + 1) * LANES], q_ref[:, (c0 + 1) * LANES:(c0 + 2) * LANES]],
                             axis=0)
        acc = None
        for par in range(2):
            keep = lo if par == 0 else jnp.logical_not(lo)
            ha, hb = GQA_GROUP * hk + par, GQA_GROUP * hk + 2 + par
            s = lax.dot_general(ql, jnp.where(keep, kk, zero), NT_DIMS, preferred_element_type=F32)
            s = s + jnp.concatenate([bm_ref[ha] + edge, bm_ref[hb] + edge], axis=0)
            sk = jnp.where(first, sink_ref[ha], sink_ref[hb])
            m = jnp.maximum(jnp.max(s, axis=-1, keepdims=True), sk)
            pexp = jnp.exp(s - m)
            den = jnp.sum(pexp, axis=-1, keepdims=True) + jnp.exp(sk - m)
            o = _dot(pexp.astype(BF16), jnp.where(keep, vv, zero)) * (1.0 / den)
            acc = o if acc is None else acc + o
        o_ref[:, c0 * LANES:(c0 + 1) * LANES] = acc[:ATT_BLOCK].astype(o_ref.dtype)
        o_ref[:, (c0 + 1) * LANES:(c0 + 2) * LANES] = acc[ATT_BLOCK:].astype(o_ref.dtype)


def _attention(q, k2, v2, sink, rel_bias):
    bsz, seq, _ = q.shape
    nblk = seq // ATT_BLOCK
    kb = 3 * ATT_BLOCK
    rel = jnp.arange(kb)[None, :] - ATT_BLOCK - jnp.arange(ATT_BLOCK)[:, None]
    bucket = _t5_bucket(rel)
    rb = rel_bias.astype(F32)
    bias = sum(jnp.where(bucket[None] == b, rb[b][:, None, None], 0.0) for b in range(REL_BUCKETS))
    bm = jnp.where((jnp.abs(rel) <= WINDOW)[None], bias, NEG_INF)
    kvw = k2.shape[2]
    prev = lambda b, i: (b, jnp.maximum(i - 1, 0), 0)
    own = lambda b, i: (b, i, 0)
    nxt = lambda b, i: (b, jnp.minimum(i + 1, nblk - 1), 0)
    kv_spec = lambda fn: pl.BlockSpec((None, ATT_BLOCK, kvw), fn)
    return pl.pallas_call(
        functools.partial(_attn_body, seq=seq),
        grid=(bsz, nblk),
        in_specs=[
            pl.BlockSpec(memory_space=pltpu.SMEM),
            pl.BlockSpec((None, ATT_BLOCK, N_HEADS * HEAD_DIM), own),
            kv_spec(prev), kv_spec(own), kv_spec(nxt),
            kv_spec(prev), kv_spec(own), kv_spec(nxt),
            pl.BlockSpec((N_HEADS, ATT_BLOCK, kb), lambda b, i: (0, 0, 0)),
        ],
        out_specs=pl.BlockSpec((None, ATT_BLOCK, N_HEADS * HEAD_DIM), own),
        out_shape=jax.ShapeDtypeStruct((bsz, seq, N_HEADS * HEAD_DIM), BF16),
        compiler_params=_cparams(("parallel", "parallel"), 48),
        name="window_attention",
    )(sink.astype(F32), q, k2, k2, k2, v2, v2, v2, bm)


def _router_body(x_ref, g_ref, wr_ref, tri_ref, hp_ref, idx_ref, rank_ref, gate_ref, cnt_ref, run_ref):
    @pl.when(pl.program_id(0) == 0)
    def _():
        run_ref[...] = jnp.zeros_like(run_ref)

    h = _rms_scale(x_ref[...]) * g_ref[...]
    half = D_MODEL // 2
    bits = lax.bitcast_convert_type(h.astype(BF16).astype(F32), U32)
    hp_ref[...] = (bits[:, half:] & jnp.uint32(0xFFFF0000)) | (bits[:, :half] >> 16)

    logits = lax.dot_general(wr_ref[...], h, NT_DIMS, precision=lax.Precision.HIGHEST,
                             preferred_element_type=F32)
    eid = lax.broadcasted_iota(I32, logits.shape, 0)
    m1 = jnp.max(logits, axis=0, keepdims=True)
    i1 = jnp.min(jnp.where(logits == m1, eid, N_EXPERTS), axis=0, keepdims=True)
    rest = jnp.where(eid == i1, -jnp.inf, logits)
    m2 = jnp.max(rest, axis=0, keepdims=True)
    i2 = jnp.min(jnp.where(rest == m2, eid, N_EXPERTS), axis=0, keepdims=True)
    e2 = jnp.exp(m2 - m1)
    gate_ref[0:1, :] = 1.0 / (1.0 + e2)
    gate_ref[1:2, :] = e2 / (1.0 + e2)
    idx_ref[0:1, :] = i1
    idx_ref[1:2, :] = i2

    sel1 = eid == i1
    sel2 = eid == i2
    onehot = jnp.where(sel1 | sel2, 1.0, 0.0)
    before = _dot(onehot.astype(BF16), tri_ref[...]) + run_ref[:, 0:1]
    rank_ref[0:1, :] = jnp.sum(jnp.where(sel1, before, 0.0), axis=0, keepdims=True).astype(I32)
    rank_ref[1:2, :] = jnp.sum(jnp.where(sel2, before, 0.0), axis=0, keepdims=True).astype(I32)
    run_ref[...] += jnp.sum(onehot, axis=1, keepdims=True)
    cnt_ref[...] = run_ref[...].astype(I32)


def _router(x2d, g, w_router):
    t = x2d.shape[0]
    tm = TM_ROUTER
    tri = (jnp.arange(tm)[:, None] < jnp.arange(tm)[None, :]).astype(BF16)
    two = lambda dt: jax.ShapeDtypeStruct((2, t), dt)
    return pl.pallas_call(
        _router_body,
        grid=(t // tm,),
        in_specs=[
            pl.BlockSpec((tm, D_MODEL), lambda i: (i, 0)),
            pl.BlockSpec((1, D_MODEL), lambda i: (0, 0)),
            pl.BlockSpec((N_EXPERTS, D_MODEL), lambda i: (0, 0)),
            pl.BlockSpec((tm, tm), lambda i: (0, 0)),
        ],
        out_specs=[
            pl.BlockSpec((tm, D_MODEL // 2), lambda i: (i, 0)),
            pl.BlockSpec((2, tm), lambda i: (0, i)),
            pl.BlockSpec((2, tm), lambda i: (0, i)),
            pl.BlockSpec((2, tm), lambda i: (0, i)),
            pl.BlockSpec((N_EXPERTS, LANES), lambda i: (0, 0)),
        ],
        out_shape=[
            jax.ShapeDtypeStruct((t, D_MODEL // 2), U32),
            two(I32), two(I32), two(F32),
            jax.ShapeDtypeStruct((N_EXPERTS, LANES), I32),
        ],
        scratch_shapes=[pltpu.VMEM((N_EXPERTS, LANES), F32)],
        compiler_params=_cparams(("arbitrary",), 48),
        name="moe_router",
    )(x2d, g, w_router.astype(F32).T, tri)


def _expert_body(te_ref, nu_ref, tok_cur_ref, tok_nxt_ref, hp_ref, wg_ref, wu_ref, wd_ref, o_ref,
                 xs_ref, xb_ref, acc_ref, sem):
    i = pl.program_id(0)
    j = pl.program_id(1)
    half = D_MODEL // 2
    rows = xb_ref.shape[0]
    n_used = nu_ref[0]
    slot = i % 2

    def gather(tok_ref, dst_slot):
        def issue(r, c):
            pltpu.make_async_copy(hp_ref.at[pl.ds(tok_ref[0, 0, r], 1)], xs_ref.at[dst_slot, pl.ds(r, 1)],
                                  sem.at[dst_slot]).start()
            return c
        lax.fori_loop(0, rows, issue, 0, unroll=8)

    @pl.when(jnp.logical_and(j == 0, i < n_used))
    def _():
        @pl.when(i == 0)
        def _():
            gather(tok_cur_ref, 0)

        pltpu.make_async_copy(hp_ref.at[pl.ds(0, rows)], xs_ref.at[slot], sem.at[slot]).wait()
        w = xs_ref[slot]
        xb_ref[:, :half] = lax.bitcast_convert_type(w << 16, F32).astype(BF16)
        xb_ref[:, half:] = lax.bitcast_convert_type(w & jnp.uint32(0xFFFF0000), F32).astype(BF16)

        @pl.when(i + 1 < n_used)
        def _():
            gather(tok_nxt_ref, 1 - slot)

    @pl.when(j == 0)
    def _():
        acc_ref[...] = jnp.zeros_like(acc_ref)

    @pl.when(i < n_used)
    def _():
        xb = xb_ref[...]
        a = (jax.nn.silu(_dot(xb, wg_ref[...])) * _dot(xb, wu_ref[...])).astype(BF16)
        acc_ref[...] += _dot(a, wd_ref[...])

    @pl.when(j == pl.num_programs(1) - 1)
    def _():
        o_ref[...] = acc_ref[...]


def _experts(hp, tok, tile_expert, n_used, wg, wu, wd):
    tm, tf = TM_MOE, TF_MOE
    n_tiles = tok.shape[0] // tm
    tok3 = tok.reshape(n_tiles, 1, tm)
    grid_spec = pltpu.PrefetchScalarGridSpec(
        num_scalar_prefetch=2,
        grid=(n_tiles, D_FF_EXPERT // tf),
        in_specs=[
            pl.BlockSpec((1, 1, tm), lambda i, j, te, nu: (i, 0, 0), memory_space=pltpu.SMEM),
            pl.BlockSpec((1, 1, tm), lambda i, j, te, nu: (jnp.minimum(i + 1, n_tiles - 1), 0, 0),
                         memory_space=pltpu.SMEM),
            pl.BlockSpec(memory_space=pl.ANY),
            pl.BlockSpec((None, D_MODEL, tf), lambda i, j, te, nu: (te[i], 0, j)),
            pl.BlockSpec((None, D_MODEL, tf), lambda i, j, te, nu: (te[i], 0, j)),
            pl.BlockSpec((None, tf, D_MODEL), lambda i, j, te, nu: (te[i], j, 0)),
        ],
        out_specs=pl.BlockSpec((tm, D_MODEL), lambda i, j, te, nu: (i, 0)),
        scratch_shapes=[pltpu.VMEM((2, tm, D_MODEL // 2), U32), pltpu.VMEM((tm, D_MODEL), BF16),
                        pltpu.VMEM((tm, D_MODEL), F32), pltpu.SemaphoreType.DMA((2,))],
    )
    return pl.pallas_call(
        _expert_body,
        grid_spec=grid_spec,
        out_shape=jax.ShapeDtypeStruct((n_tiles * tm, D_MODEL), F32),
        compiler_params=_cparams(("arbitrary", "arbitrary"), 56),
        name="moe_experts",
    )(tile_expert, n_used, tok3, tok3, hp, wg, wu, wd)


def _combine_body(d1_ref, d2_ref, x_ref, g1_ref, g2_ref, ys_ref, o_ref, y1_ref, y2_ref, sem, *, rows):
    def issue(r, c):
        pltpu.make_async_copy(ys_ref.at[pl.ds(d1_ref[0, 0, r], 1)], y1_ref.at[pl.ds(r, 1)], sem.at[0]).start()
        pltpu.make_async_copy(ys_ref.at[pl.ds(d2_ref[0, 0, r], 1)], y2_ref.at[pl.ds(r, 1)], sem.at[1]).start()
        return c
    lax.fori_loop(0, rows, issue, 0, unroll=8)
    pltpu.make_async_copy(ys_ref.at[pl.ds(0, rows)], y1_ref, sem.at[0]).wait()
    pltpu.make_async_copy(ys_ref.at[pl.ds(0, rows)], y2_ref, sem.at[1]).wait()
    o_ref[...] = x_ref[...] + g1_ref[...] * y1_ref[...] + g2_ref[...] * y2_ref[...]


def _combine(x2d, ys, dest, gates):
    t = x2d.shape[0]
    rows = TM_COMBINE
    idx_spec = pl.BlockSpec((1, 1, rows), lambda i: (i, 0, 0), memory_space=pltpu.SMEM)
    gate_spec = pl.BlockSpec((rows, 1), lambda i: (i, 0))
    row_spec = pl.BlockSpec((rows, D_MODEL), lambda i: (i, 0))
    return pl.pallas_call(
        functools.partial(_combine_body, rows=rows),
        grid=(t // rows,),
        in_specs=[idx_spec, idx_spec, row_spec, gate_spec, gate_spec, pl.BlockSpec(memory_space=pl.ANY)],
        out_specs=row_spec,
        out_shape=jax.ShapeDtypeStruct((t, D_MODEL), F32),
        scratch_shapes=[pltpu.VMEM((rows, D_MODEL), F32), pltpu.VMEM((rows, D_MODEL), F32),
                        pltpu.SemaphoreType.DMA((2,))],
        compiler_params=_cparams(("arbitrary",), 48),
        name="moe_combine",
    )(dest[0].reshape(t // rows, 1, rows), dest[1].reshape(t // rows, 1, rows), x2d,
      gates[0][:, None], gates[1][:, None], ys)


def _moe(x2d, g, w_router, wg, wu, wd):
    t = x2d.shape[0]
    tm = TM_MOE
    hp, idx, rank, gates, cnt = _router(x2d, g, w_router)
    counts = cnt[:, 0]
    tiles = (counts + tm - 1) // tm
    tile_end = jnp.cumsum(tiles)
    row_start = (tile_end - tiles) * tm
    n_tiles = (2 * t) // tm + N_EXPERTS
    eid = jnp.arange(N_EXPERTS, dtype=I32)[:, None, None]
    dest = jnp.sum(jnp.where(idx[None] == eid, row_start[:, None, None], 0), axis=0) + rank
    tile_expert = jnp.minimum(jnp.searchsorted(tile_end, jnp.arange(n_tiles), side="right"), N_EXPERTS - 1)
    tok = jnp.zeros((n_tiles * tm,), I32).at[dest.reshape(-1)].set(jnp.tile(jnp.arange(t, dtype=I32), 2))
    ys = _experts(hp, tok, tile_expert.astype(I32), tile_end[-1:].astype(I32), wg, wu, wd)
    return _combine(x2d, ys, dest, gates)


def _even_layer(x, norm_mix, norm_ffn, w_in, w_out, ln_g, ln_b, sgu_w, sgu_b, hy_conv,
                f_w1, f_b1, f_w2, f_b2, f_w3, f_b3, f_freq, hy_skip, wg, wu, wd):
    bsz, seq, _ = x.shape
    nh = seq // LANES
    sgu_bb = jnp.broadcast_to(sgu_b.astype(F32)[:, :, None], (A_GROUPS, CHUNK, LANES))
    w_in = w_in.astype(BF16)
    ya, hbt = _even_in(x, norm_mix[None, :], w_in[:, :2 * A_WIDTH], w_in[:, 2 * A_WIDTH:].T,
                       ln_g[None, :], ln_b[None, :], sgu_w.astype(BF16), sgu_bb)
    hbt = hbt.reshape(bsz, -1, nh, LANES)

    tb = _dft_tables(seq)
    kt = _hyena_kernels(seq, f_w1, f_b1, f_w2, f_b2, f_w3, f_b3, f_freq)
    kf = _filter_spectrum(kt.reshape(HYENA_ORDER, B_WIDTH, 2 * nh, LANES), tb)

    taps = hy_conv.astype(F32)
    skip = hy_skip.astype(F32)
    z1 = _hyena_conv(hbt, 0, hbt, B_WIDTH, taps, skip[0], kf[0], tb, conv_z=True)
    ybt = _hyena_conv(z1, 0, hbt, 2 * B_WIDTH, taps, skip[1], kf[1], tb, conv_z=False)

    w_out = w_out.astype(BF16)
    x = _out_proj(x, ya, w_out[:A_WIDTH], ybt.reshape(bsz, B_WIDTH, seq), w_out[A_WIDTH:])
    x2d = _ffn(x.reshape(bsz * seq, D_MODEL), norm_ffn[None, :], wg.astype(BF16), wu.astype(BF16), wd.astype(BF16))
    return x2d.reshape(bsz, seq, D_MODEL)


def _odd_layer(x, norm_mix, norm_ffn, w_qkv, q_g, k_g, sink, w_out, rel_bias, w_router, wg, wu, wd):
    bsz, seq, _ = x.shape
    q, k2, v2 = _qkv_proj(x, norm_mix[None, :], w_qkv, q_g, k_g)
    o = _attention(q, k2, v2, sink, rel_bias)
    x = _out_proj(x, o, w_out.astype(BF16))
    x2d = _moe(x.reshape(bsz * seq, D_MODEL), norm_ffn[None, :], w_router,
               wg.astype(BF16), wu.astype(BF16), wd.astype(BF16))
    return x2d.reshape(bsz, seq, D_MODEL)


def _trunk(x, norm_mix, norm_ffn, even_p, odd_p, rel_bias):
    depth = norm_mix.shape[0]
    for i in range(depth):
        j = i // 2
        if i % 2 == 0:
            x = _even_layer(x, norm_mix[i], norm_ffn[i], *[p[j] for p in even_p])
        else:
            x = _odd_layer(x, norm_mix[i], norm_ffn[i], *[p[j] for p in odd_p[:5]], rel_bias,
                           *[p[j] for p in odd_p[5:]])
    return x


def kernel(x_prompt, x_sample, norm_mix, norm_ffn, ev_w_in, ev_w_out, sgu_ln_g, sgu_ln_b, sgu_w, sgu_b,
           hy_conv, hy_f_w1, hy_f_b1, hy_f_w2, hy_f_b2, hy_f_w3, hy_f_b3, hy_f_freq, hy_skip,
           ffn_w_gate, ffn_w_up, ffn_w_down, at_w_qkv, at_q_norm, at_k_norm, at_sink, at_w_out,
           rel_bias, moe_router, moe_w_gate, moe_w_up, moe_w_down):
    even_p = (ev_w_in, ev_w_out, sgu_ln_g, sgu_ln_b, sgu_w, sgu_b, hy_conv, hy_f_w1, hy_f_b1, hy_f_w2,
              hy_f_b2, hy_f_w3, hy_f_b3, hy_f_freq, hy_skip, ffn_w_gate, ffn_w_up, ffn_w_down)
    odd_p = (at_w_qkv, at_q_norm, at_k_norm, at_sink, at_w_out, moe_router, moe_w_gate, moe_w_up, moe_w_down)
    assert x_prompt.shape[1:] == x_sample.shape[1:]
    nb_prompt = x_prompt.shape[0]
    x = jnp.concatenate([x_prompt, x_sample], axis=0)
    y = _trunk(x, norm_mix, norm_ffn, even_p, odd_p, rel_bias)
    return (y[:nb_prompt], y[nb_prompt:])
```

```python
import functools
import math

import jax
import jax.numpy as jnp
from jax import lax
from jax.experimental import pallas as pl
from jax.experimental.pallas import tpu as pltpu

F32 = jnp.float32
BF16 = jnp.bfloat16
U32 = jnp.uint32
I32 = jnp.int32

D_MODEL = 1024
A_GROUPS = 4
A_WIDTH = D_MODEL // 2
CHUNK = 128
B_WIDTH = D_MODEL // 2
HYENA_ORDER = 2
FILTER_DIRS = 2
FILTER_BANDS = 16
FILTER_EMB = 1 + 2 * FILTER_BANDS
FILTER_HIDDEN = 64
DECAY_TARGET = 1e-2
FAST_DECAY_PCT = 0.3
SLOW_DECAY_PCT = 1.5
HEAD_DIM = 64
N_HEADS = D_MODEL // HEAD_DIM
N_KV_HEADS = N_HEADS // 4
GQA_GROUP = N_HEADS // N_KV_HEADS
WINDOW = 128
ATT_BLOCK = 128
REL_BUCKETS = 32
REL_MAX_DIST = 128
NEG_INF = -1e30
D_FF = 2816
N_EXPERTS = 8
D_FF_EXPERT = 3584
EPS = 1e-6

LANES = 128
MIB = 1024 * 1024

TM_PROJ = 512
TM_FFN = 512
TF_FFN = D_FF // 2
TM_MOE = 512
TF_MOE = D_FF_EXPERT // 2
TM_ROUTER = 512
TM_COMBINE = 256
TM_DISPATCH = 512
HY_CB = 32
HY_PAIRS = 4

NT_DIMS = (((1,), (1,)), ((), ()))
TN_DIMS = (((0,), (0,)), ((), ()))


def _cparams(sem, vmem_mib):
    return pltpu.CompilerParams(dimension_semantics=sem, vmem_limit_bytes=vmem_mib * MIB)


def _rms_scale(x):
    return x * lax.rsqrt(jnp.mean(x * x, axis=-1, keepdims=True) + EPS)


def _dot(a, b):
    return jnp.dot(a, b, preferred_element_type=F32)


def _group_specs(xa, xb, tm):
    na, seq = xa.shape[0], xa.shape[1]
    last = seq // tm - 1
    width = xa.shape[2]
    spec_a = pl.BlockSpec((None, tm, width), lambda b, i: (jnp.minimum(b, na - 1), jnp.where(b < na, i, last), 0))
    spec_b = pl.BlockSpec((None, tm, width), lambda b, i: (jnp.maximum(b - na, 0), jnp.where(b < na, 0, i), 0))
    return spec_a, spec_b


def _group_tile(xa_ref, xb_ref, na):
    return jnp.where(pl.program_id(0) < na, xa_ref[...], xb_ref[...])


def _even_in_body(xa_ref, xb_ref, g_ref, w_ref, wht_ref, lng_ref, lnb_ref, sw_ref, sb_ref, ya_ref, hbt_ref, *, na):
    x = _group_tile(xa_ref, xb_ref, na)
    h = (_rms_scale(x) * g_ref[...]).astype(BF16)
    u = jax.nn.gelu(_dot(h, w_ref[:, 0:A_WIDTH]))
    v = jax.nn.gelu(_dot(h, w_ref[:, A_WIDTH:2 * A_WIDTH]))
    hbt_ref[...] = lax.dot_general(wht_ref[...], h, NT_DIMS, preferred_element_type=F32)
    tm = x.shape[0]
    for gi in range(A_GROUPS):
        cs = slice(gi * LANES, (gi + 1) * LANES)
        vg = v[:, cs]
        xc = vg - jnp.mean(vg, axis=-1, keepdims=True)
        var = jnp.mean(xc * xc, axis=-1, keepdims=True)
        vn = (xc * lax.rsqrt(var + EPS) * lng_ref[:, cs] + lnb_ref[:, cs]).astype(BF16)
        for c in range(tm // CHUNK):
            rs = slice(c * CHUNK, (c + 1) * CHUNK)
            mixed = _dot(sw_ref[gi], vn[rs]) + sb_ref[gi]
            ya_ref[rs, cs] = (u[rs, cs] * mixed).astype(ya_ref.dtype)


def _even_in(xa, xb, g, w_uv, w_hb_t, ln_g, ln_b, sgu_w, sgu_b):
    na, seq, _ = xa.shape
    bsz = na + xb.shape[0]
    tm = TM_PROJ
    n_hb = w_hb_t.shape[0]
    const = lambda *shape: pl.BlockSpec(shape, lambda b, i: (0,) * len(shape))
    return pl.pallas_call(
        functools.partial(_even_in_body, na=na),
        grid=(bsz, seq // tm),
        in_specs=[
            *_group_specs(xa, xb, tm),
            const(1, D_MODEL),
            const(D_MODEL, 2 * A_WIDTH),
            const(n_hb, D_MODEL),
            const(1, A_WIDTH),
            const(1, A_WIDTH),
            const(A_GROUPS, CHUNK, CHUNK),
            const(A_GROUPS, CHUNK, LANES),
        ],
        out_specs=[
            pl.BlockSpec((None, tm, A_WIDTH), lambda b, i: (b, i, 0)),
            pl.BlockSpec((None, n_hb, tm), lambda b, i: (b, 0, i)),
        ],
        out_shape=[
            jax.ShapeDtypeStruct((bsz, seq, A_WIDTH), BF16),
            jax.ShapeDtypeStruct((bsz, n_hb, seq), F32),
        ],
        compiler_params=_cparams(("parallel", "parallel"), 48),
        name="even_in",
    )(xa, xb, g, w_uv, w_hb_t, ln_g, ln_b, sgu_w, sgu_b)


def _filter_body(ft_ref, t_ref, w1_ref, b1_ref, w2_ref, b2_ref, w3t_ref, b3_ref, fr0_ref, fr1_ref, absd_ref,
                 o_ref, *, seq):
    hp = lax.Precision.HIGHEST
    tl = ft_ref.shape[0]
    h = jnp.sin(fr0_ref[...] * (jnp.dot(ft_ref[...], w1_ref[...], precision=hp, preferred_element_type=F32)
                                + b1_ref[...]))
    h = jnp.sin(fr1_ref[...] * (jnp.dot(h, w2_ref[...], precision=hp, preferred_element_type=F32) + b2_ref[...]))
    out = lax.dot_general(w3t_ref[...], h, NT_DIMS, precision=hp, preferred_element_type=F32) + b3_ref[...]
    decay = jnp.exp(-absd_ref[...] * t_ref[...])
    pos = pl.program_id(0) * tl + lax.broadcasted_iota(I32, (1, tl), 1)
    for o in range(HYENA_ORDER):
        o_ref[o] = jnp.where(pos == seq, 0.0, out[o * B_WIDTH:(o + 1) * B_WIDTH] * decay)


def _hyena_kernels(seq, w1, b1, w2, b2, w3, b3, freq):
    t = jnp.linspace(0.0, 1.0, seq, dtype=F32)[:, None]
    w = 2.0 * math.pi * jnp.arange(seq, dtype=F32)[:, None] / seq
    bands = jnp.linspace(1e-4, FILTER_BANDS - 1, FILTER_BANDS, dtype=F32)[None, :]
    feats = jnp.concatenate([t, jnp.cos(bands * w), jnp.sin(bands * w)], axis=-1)
    feats = jnp.concatenate([feats, feats[:1], feats[:0:-1]], axis=0)
    t_row = feats[:, 0][None, :]
    feats = jnp.pad(feats, ((0, 0), (0, LANES - FILTER_EMB)))
    hpad = LANES - FILTER_HIDDEN
    w1p = jnp.pad(w1.astype(F32), ((0, LANES - FILTER_EMB), (0, hpad)))
    w2p = jnp.pad(w2.astype(F32), ((0, hpad), (0, hpad)))
    n_dir = HYENA_ORDER * B_WIDTH
    w3t = jnp.pad(w3.astype(F32), ((0, hpad), (0, 0))).T.reshape(FILTER_DIRS, n_dir, LANES)
    b3c = b3.astype(F32).reshape(FILTER_DIRS, n_dir, 1)
    b1p = jnp.pad(b1.astype(F32), (0, hpad))[None, :]
    b2p = jnp.pad(b2.astype(F32), (0, hpad))[None, :]
    fr = jnp.pad(freq.astype(F32), ((0, 0), (0, hpad)))
    deltas = jnp.linspace(math.log(DECAY_TARGET) / SLOW_DECAY_PCT,
                          math.log(DECAY_TARGET) / FAST_DECAY_PCT, B_WIDTH, dtype=F32)
    absd = jnp.abs(deltas)[:, None]
    tl = min(seq, 1024)
    nhalf = seq // tl
    const = lambda *shape: pl.BlockSpec(shape, lambda i: (0,) * len(shape))
    return pl.pallas_call(
        functools.partial(_filter_body, seq=seq),
        grid=(2 * nhalf,),
        in_specs=[
            pl.BlockSpec((tl, LANES), lambda i: (i, 0)),
            pl.BlockSpec((1, tl), lambda i: (0, i)),
            const(LANES, LANES), const(1, LANES), const(LANES, LANES), const(1, LANES),
            pl.BlockSpec((None, n_dir, LANES), lambda i: (i // nhalf, 0, 0)),
            pl.BlockSpec((None, n_dir, 1), lambda i: (i // nhalf, 0, 0)),
            const(1, LANES), const(1, LANES), const(B_WIDTH, 1),
        ],
        out_specs=pl.BlockSpec((HYENA_ORDER, B_WIDTH, tl), lambda i: (0, 0, i)),
        out_shape=jax.ShapeDtypeStruct((HYENA_ORDER, B_WIDTH, 2 * seq), F32),
        compiler_params=_cparams(("parallel",), 48),
        name="hyena_filter",
    )(feats, t_row, w1p, b1p, w2p, b2p, w3t, b3c, fr[0:1], fr[1:2], absd)


def _dft_tables(seq):
    n = 2 * seq
    nb = n // LANES
    k1 = jnp.arange(nb, dtype=I32)[:, None]
    ang1 = (2.0 * math.pi / nb) * ((k1 * jnp.arange(nb, dtype=I32)[None, :]) % nb).astype(F32)
    f1_full = jnp.concatenate([jnp.cos(ang1), -jnp.sin(ang1)], axis=0)
    f1_half = f1_full[:, : nb // 2]
    f4_half = f1_half.T
    angt = (2.0 * math.pi / n) * ((k1 * jnp.arange(LANES, dtype=I32)[None, :]) % n).astype(F32)
    twr, twi = jnp.cos(angt), -jnp.sin(angt)
    a = jnp.arange(LANES, dtype=I32)
    ang2 = (2.0 * math.pi / LANES) * ((a[:, None] * a[None, :]) % LANES).astype(F32)
    cr, ci = jnp.cos(ang2), -jnp.sin(ang2)
    m2 = jnp.concatenate([jnp.concatenate([cr, ci], axis=1), jnp.concatenate([-ci, cr], axis=1)], axis=0)
    m3 = jnp.concatenate([jnp.concatenate([cr, -ci], axis=1), jnp.concatenate([ci, cr], axis=1)], axis=0)
    return dict(f1_full=f1_full.astype(BF16), f1_half=f1_half.astype(BF16), f4_half=f4_half.astype(BF16),
                twr=twr, twi=twi, m2=m2.astype(BF16), m3=m3.astype(BF16))


def _fwd_spectrum(pairs, f1_ref, twr_ref, twi_ref, m2_ref):
    nb = twr_ref.shape[0]
    a_all = [_dot(f1_ref[...], jnp.concatenate(xs, axis=1).astype(BF16)) for xs in pairs]
    twr, twi = twr_ref[...], twi_ref[...]
    out = []
    for a in a_all:
        lhs = []
        for d in range(2):
            ar = a[:nb, d * LANES:(d + 1) * LANES]
            ai = a[nb:, d * LANES:(d + 1) * LANES]
            lhs.append(jnp.concatenate([ar * twr - ai * twi, ar * twi + ai * twr], axis=1))
        out.append(_dot(jnp.concatenate(lhs, axis=0).astype(BF16), m2_ref[...]))
    return out


def _spec_body(k_ref, f1_ref, twr_ref, twi_ref, m2_ref, o_ref):
    nb = twr_ref.shape[0]
    inv_n = 1.0 / (nb * LANES)

    def group(it, carry):
        c0 = 2 * HY_PAIRS * it
        pairs = [[k_ref[c0 + 2 * g], k_ref[c0 + 2 * g + 1]] for g in range(HY_PAIRS)]
        for g, z in enumerate(_fwd_spectrum(pairs, f1_ref, twr_ref, twi_ref, m2_ref)):
            o_ref[c0 + 2 * g] = z[:nb] * inv_n
            o_ref[c0 + 2 * g + 1] = z[nb:] * inv_n
        return carry
    lax.fori_loop(0, k_ref.shape[0] // (2 * HY_PAIRS), group, 0)


def _filter_spectrum(kt, tb):
    orders, c, nb, _ = kt.shape
    cb = HY_CB
    const = lambda *shape: pl.BlockSpec(shape, lambda o, j: (0,) * len(shape))
    return pl.pallas_call(
        _spec_body,
        grid=(orders, c // cb),
        in_specs=[
            pl.BlockSpec((None, cb, nb, LANES), lambda o, j: (o, j, 0, 0)),
            const(2 * nb, nb), const(nb, LANES), const(nb, LANES), const(2 * LANES, 2 * LANES),
        ],
        out_specs=pl.BlockSpec((None, cb, nb, 2 * LANES), lambda o, j: (o, j, 0, 0)),
        out_shape=jax.ShapeDtypeStruct((orders, c, nb, 2 * LANES), F32),
        compiler_params=_cparams(("parallel", "parallel"), 48),
        name="hyena_filter_spectrum",
    )(kt, tb["f1_full"], tb["twr"], tb["twi"], tb["m2"])


def _shift_rows(x, down):
    rows = x.shape[0]
    idx = lax.broadcasted_iota(I32, x.shape, 0)
    if down:
        return jnp.where(idx == 0, 0.0, pltpu.roll(x, 1, axis=0))
    return jnp.where(idx == rows - 1, 0.0, pltpu.roll(x, rows - 1, axis=0))


def _short_conv(x, taps_ref, ch):
    lane = lax.broadcasted_iota(I32, x.shape, 1)
    prev = pltpu.roll(jnp.where(lane == LANES - 1, _shift_rows(x, True), x), 1, axis=1)
    nxt = pltpu.roll(jnp.where(lane == 0, _shift_rows(x, False), x), LANES - 1, axis=1)
    return taps_ref[0, ch] * prev + taps_ref[1, ch] * x + taps_ref[2, ch] * nxt


def _conv_body(taps_ref, skip_ref, z_ref, g_ref, kf_ref, f1_ref, f4_ref, twr_ref, twi_ref, m2_ref, m3_ref,
               o_ref, *, zch, gch, conv_z):
    nb = twr_ref.shape[0]
    cb = z_ref.shape[0]
    base = pl.program_id(0) * cb

    def group(it, carry):
        c0 = 2 * HY_PAIRS * it
        chans = [[c0 + 2 * g, c0 + 2 * g + 1] for g in range(HY_PAIRS)]
        xs = [[_short_conv(z_ref[c], taps_ref, zch + base + c) if conv_z else z_ref[c] for c in pr] for pr in chans]
        zs = _fwd_spectrum(xs, f1_ref, twr_ref, twi_ref, m2_ref)
        ccs = []
        for pr, z in zip(chans, zs):
            ys = []
            for d, c in enumerate(pr):
                zr = z[d * nb:(d + 1) * nb, :LANES]
                zi = z[d * nb:(d + 1) * nb, LANES:]
                kf = kf_ref[c]
                kr, ki = kf[:, :LANES], kf[:, LANES:]
                ys.append(jnp.concatenate([zr * kr - zi * ki, zr * ki + zi * kr], axis=1))
            ccs.append(_dot(jnp.concatenate(ys, axis=0).astype(BF16), m3_ref[...]))
        twr, twi = twr_ref[...], twi_ref[...]
        for pr, x2, cc in zip(chans, xs, ccs):
            drs, dis = [], []
            for d in range(2):
                ccr = cc[d * nb:(d + 1) * nb, :LANES]
                cci = cc[d * nb:(d + 1) * nb, LANES:]
                drs.append(ccr * twr + cci * twi)
                dis.append(cci * twr - ccr * twi)
            rhs = jnp.concatenate([jnp.concatenate(drs, axis=1), jnp.concatenate(dis, axis=1)], axis=0)
            y = _dot(f4_ref[...], rhs.astype(BF16))
            for d, c in enumerate(pr):
                gate = _short_conv(g_ref[c], taps_ref, gch + base + c)
                o_ref[c] = gate * (y[:, d * LANES:(d + 1) * LANES] + skip_ref[base + c] * x2[d])
        return carry
    lax.fori_loop(0, cb // (2 * HY_PAIRS), group, 0)


def _hyena_conv(zsrc, zch, gsrc, gch, taps, skip, kf, tb, conv_z):
    bsz, _, nh, _ = zsrc.shape
    nb = 2 * nh
    cb = HY_CB
    smem = pl.BlockSpec(memory_space=pltpu.SMEM)
    const = lambda *shape: pl.BlockSpec(shape, lambda j, b: (0,) * len(shape))
    zblk, gblk = zch // cb, gch // cb
    return pl.pallas_call(
        functools.partial(_conv_body, zch=zch, gch=gch, conv_z=conv_z),
        grid=(B_WIDTH // cb, bsz),
        in_specs=[
            smem, smem,
            pl.BlockSpec((None, cb, nh, LANES), lambda j, b: (b, zblk + j, 0, 0)),
            pl.BlockSpec((None, cb, nh, LANES), lambda j, b: (b, gblk + j, 0, 0)),
            pl.BlockSpec((cb, nb, 2 * LANES), lambda j, b: (j, 0, 0)),
            const(2 * nb, nh), const(nh, 2 * nb), const(nb, LANES), const(nb, LANES),
            const(2 * LANES, 2 * LANES), const(2 * LANES, 2 * LANES),
        ],
        out_specs=pl.BlockSpec((None, cb, nh, LANES), lambda j, b: (b, j, 0, 0)),
        out_shape=jax.ShapeDtypeStruct((bsz, B_WIDTH, nh, LANES), F32),
        compiler_params=_cparams(("parallel", "parallel"), 48),
        name="hyena_conv",
    )(taps, skip, zsrc, gsrc, kf, tb["f1_half"], tb["f4_half"], tb["twr"], tb["twi"], tb["m2"], tb["m3"])


def _out_proj_body(*refs, n_res, has_a, has_t):
    refs = list(refs)
    o_ref = refs.pop()
    res = refs.pop(0)[...] if n_res == 0 else _group_tile(refs.pop(0), refs.pop(0), n_res)
    acts = refs[:has_a + has_t]
    ws = refs[has_a + has_t:]
    acc = res
    if has_a:
        acc = acc + _dot(acts[0][...].astype(BF16), ws[0][...])
    if has_t:
        acc = acc + lax.dot_general(acts[-1][...].astype(BF16), ws[-1][...], TN_DIMS, preferred_element_type=F32)
    o_ref[...] = acc


def _out_proj(res, a=None, wa=None, bt=None, wb=None):
    tm = TM_PROJ
    row = lambda width: pl.BlockSpec((None, tm, width), lambda b, i: (b, i, 0))
    full = lambda arr: pl.BlockSpec(arr.shape, lambda b, i: (0, 0))
    if isinstance(res, tuple):
        n_res = res[0].shape[0]
        bsz, seq = n_res + res[1].shape[0], res[0].shape[1]
        in_specs, args = list(_group_specs(res[0], res[1], tm)), list(res)
    else:
        n_res = 0
        bsz, seq = res.shape[0], res.shape[1]
        in_specs, args = [row(D_MODEL)], [res]
    if a is not None:
        in_specs.append(row(a.shape[2]))
        args.append(a)
    if bt is not None:
        in_specs.append(pl.BlockSpec((None, bt.shape[1], tm), lambda b, i: (b, 0, i)))
        args.append(bt)
    for w in (wa, wb):
        if w is not None:
            in_specs.append(full(w))
            args.append(w)
    return pl.pallas_call(
        functools.partial(_out_proj_body, n_res=n_res, has_a=a is not None, has_t=bt is not None),
        grid=(bsz, seq // tm),
        in_specs=in_specs,
        out_specs=row(D_MODEL),
        out_shape=jax.ShapeDtypeStruct((bsz, seq, D_MODEL), F32),
        compiler_params=_cparams(("parallel", "parallel"), 48),
        name="out_proj",
    )(*args)


def _ffn_body(x_ref, g_ref, wg_ref, wu_ref, wd_ref, o_ref, h_ref, acc_ref):
    j = pl.program_id(1)

    @pl.when(j == 0)
    def _():
        x = x_ref[...]
        h_ref[...] = (_rms_scale(x) * g_ref[...]).astype(BF16)
        acc_ref[...] = x

    h = h_ref[...]
    a = (jax.nn.silu(_dot(h, wg_ref[...])) * _dot(h, wu_ref[...])).astype(BF16)
    acc_ref[...] += _dot(a, wd_ref[...])

    @pl.when(j == pl.num_programs(1) - 1)
    def _():
        o_ref[...] = acc_ref[...]


def _ffn(x2d, g, wg, wu, wd):
    t = x2d.shape[0]
    tm, tf = TM_FFN, TF_FFN
    return pl.pallas_call(
        _ffn_body,
        grid=(t // tm, D_FF // tf),
        in_specs=[
            pl.BlockSpec((tm, D_MODEL), lambda i, j: (i, 0)),
            pl.BlockSpec((1, D_MODEL), lambda i, j: (0, 0)),
            pl.BlockSpec((D_MODEL, tf), lambda i, j: (0, j)),
            pl.BlockSpec((D_MODEL, tf), lambda i, j: (0, j)),
            pl.BlockSpec((tf, D_MODEL), lambda i, j: (j, 0)),
        ],
        out_specs=pl.BlockSpec((tm, D_MODEL), lambda i, j: (i, 0)),
        out_shape=jax.ShapeDtypeStruct((t, D_MODEL), F32),
        scratch_shapes=[pltpu.VMEM((tm, D_MODEL), BF16), pltpu.VMEM((tm, D_MODEL), F32)],
        compiler_params=_cparams(("parallel", "arbitrary"), 56),
        name="ffn_swiglu",
    )(x2d, g, wg, wu, wd)


def _qkv_body(x_ref, g_ref, w_ref, qg_ref, kg_ref, q_ref, k_ref, v_ref):
    h = (_rms_scale(x_ref[...]) * g_ref[...]).astype(BF16)
    nq = N_HEADS * HEAD_DIM
    nk = N_KV_HEADS * LANES
    lo = lax.broadcasted_iota(I32, (1, LANES), 1) < HEAD_DIM
    qgain = qg_ref[...] * (HEAD_DIM ** -0.5)
    for c in range(nq // LANES):
        x = _dot(h, w_ref[:, c * LANES:(c + 1) * LANES])
        x2 = x * x
        s_lo = jnp.sum(jnp.where(lo, x2, 0.0), axis=-1, keepdims=True)
        s_hi = jnp.sum(jnp.where(lo, 0.0, x2), axis=-1, keepdims=True)
        r = jnp.where(lo, lax.rsqrt(s_lo / HEAD_DIM + EPS), lax.rsqrt(s_hi / HEAD_DIM + EPS))
        q_ref[:, c * LANES:(c + 1) * LANES] = (x * r * qgain).astype(q_ref.dtype)
    for c in range(N_KV_HEADS):
        x = _dot(h, w_ref[:, nq + c * LANES:nq + (c + 1) * LANES])
        k_ref[:, c * LANES:(c + 1) * LANES] = (_rms_scale(x) * kg_ref[...]).astype(k_ref.dtype)
    v_ref[...] = _dot(h, w_ref[:, nq + nk:]).astype(v_ref.dtype)


def _qkv_proj(x, g, w_qkv, q_g, k_g):
    bsz, seq, _ = x.shape
    tm = TM_PROJ
    nq = N_HEADS * HEAD_DIM
    nkv = N_KV_HEADS * HEAD_DIM
    dup = lambda w: jnp.tile(w.reshape(D_MODEL, N_KV_HEADS, 1, HEAD_DIM), (1, 1, 2, 1)).reshape(D_MODEL, 2 * nkv)
    w = jnp.concatenate([w_qkv[:, :nq], dup(w_qkv[:, nq:nq + nkv]), dup(w_qkv[:, nq + nkv:])], axis=1).astype(BF16)
    two = lambda v: jnp.tile(v.astype(F32), 2)[None, :]
    row = lambda width: pl.BlockSpec((None, tm, width), lambda b, i: (b, i, 0))
    const = lambda *shape: pl.BlockSpec(shape, lambda b, i: (0,) * len(shape))
    return pl.pallas_call(
        _qkv_body,
        grid=(bsz, seq // tm),
        in_specs=[row(D_MODEL), const(1, D_MODEL), const(D_MODEL, w.shape[1]), const(1, LANES), const(1, LANES)],
        out_specs=[row(nq), row(2 * nkv), row(2 * nkv)],
        out_shape=[jax.ShapeDtypeStruct((bsz, seq, nq), BF16), jax.ShapeDtypeStruct((bsz, seq, 2 * nkv), BF16),
                   jax.ShapeDtypeStruct((bsz, seq, 2 * nkv), BF16)],
        compiler_params=_cparams(("parallel", "parallel"), 48),
        name="qkv_proj",
    )(x, g, w, two(q_g), two(k_g))


def _t5_bucket(rel):
    nbk = REL_BUCKETS // 2
    max_exact = nbk // 2
    ret = jnp.where(rel > 0, nbk, 0)
    n = jnp.abs(rel)
    large = max_exact + (jnp.log(jnp.maximum(n, 1).astype(F32) / max_exact)
                         / math.log(REL_MAX_DIST / max_exact) * (nbk - max_exact)).astype(I32)
    large = jnp.minimum(large, nbk - 1)
    return ret + jnp.where(n < max_exact, n, large)


def _attn_body(sink_ref, q_ref, kp_ref, ko_ref, kn_ref, vp_ref, vo_ref, vn_ref, bm_ref, o_ref, *, seq):
    i = pl.program_id(1)
    kb = 3 * ATT_BLOCK
    kpos = (i - 1) * ATT_BLOCK + lax.broadcasted_iota(I32, (ATT_BLOCK, kb), 1)
    edge = jnp.where((kpos >= 0) & (kpos < seq), 0.0, NEG_INF)
    lo = lax.broadcasted_iota(I32, (1, LANES), 1) < HEAD_DIM
    first = lax.broadcasted_iota(I32, (2 * ATT_BLOCK, 1), 0) < ATT_BLOCK
    for hk in range(N_KV_HEADS):
        ks = slice(hk * LANES, (hk + 1) * LANES)
        kk = jnp.concatenate([kp_ref[:, ks], ko_ref[:, ks], kn_ref[:, ks]], axis=0)
        vv = jnp.concatenate([vp_ref[:, ks], vo_ref[:, ks], vn_ref[:, ks]], axis=0)
        zero = jnp.zeros_like(kk)
        c0 = 2 * hk
        ql = jnp.concatenate([q_ref[:, c0 * LANES:(c0 + 1) * LANES], q_ref[:, (c0 + 1) * LANES:(c0 + 2) * LANES]],
                             axis=0)
        acc = None
        for par in range(2):
            keep = lo if par == 0 else jnp.logical_not(lo)
            ha, hb = GQA_GROUP * hk + par, GQA_GROUP * hk + 2 + par
            s = lax.dot_general(ql, jnp.where(keep, kk, zero), NT_DIMS, preferred_element_type=F32)
            s = s + jnp.concatenate([bm_ref[ha] + edge, bm_ref[hb] + edge], axis=0)
            sk = jnp.where(first, sink_ref[ha], sink_ref[hb])
            m = jnp.maximum(jnp.max(s, axis=-1, keepdims=True), sk)
            pexp = jnp.exp(s - m)
            den = jnp.sum(pexp, axis=-1, keepdims=True) + jnp.exp(sk - m)
            o = _dot(pexp.astype(BF16), jnp.where(keep, vv, zero)) * (1.0 / den)
            acc = o if acc is None else acc + o
        o_ref[:, c0 * LANES:(c0 + 1) * LANES] = acc[:ATT_BLOCK].astype(o_ref.dtype)
        o_ref[:, (c0 + 1) * LANES:(c0 + 2) * LANES] = acc[ATT_BLOCK:].astype(o_ref.dtype)


def _attention(q, k2, v2, sink, rel_bias):
    bsz, seq, _ = q.shape
    nblk = seq // ATT_BLOCK
    kb = 3 * ATT_BLOCK
    rel = jnp.arange(kb)[None, :] - ATT_BLOCK - jnp.arange(ATT_BLOCK)[:, None]
    bucket = _t5_bucket(rel)
    rb = rel_bias.astype(F32)
    bias = sum(jnp.where(bucket[None] == b, rb[b][:, None, None], 0.0) for b in range(REL_BUCKETS))
    bm = jnp.where((jnp.abs(rel) <= WINDOW)[None], bias, NEG_INF)
    kvw = k2.shape[2]
    prev = lambda b, i: (b, jnp.maximum(i - 1, 0), 0)
    own = lambda b, i: (b, i, 0)
    nxt = lambda b, i: (b, jnp.minimum(i + 1, nblk - 1), 0)
    kv_spec = lambda fn: pl.BlockSpec((None, ATT_BLOCK, kvw), fn)
    return pl.pallas_call(
        functools.partial(_attn_body, seq=seq),
        grid=(bsz, nblk),
        in_specs=[
            pl.BlockSpec(memory_space=pltpu.SMEM),
            pl.BlockSpec((None, ATT_BLOCK, N_HEADS * HEAD_DIM), own),
            kv_spec(prev), kv_spec(own), kv_spec(nxt),
            kv_spec(prev), kv_spec(own), kv_spec(nxt),
            pl.BlockSpec((N_HEADS, ATT_BLOCK, kb), lambda b, i: (0, 0, 0)),
        ],
        out_specs=pl.BlockSpec((None, ATT_BLOCK, N_HEADS * HEAD_DIM), own),
        out_shape=jax.ShapeDtypeStruct((bsz, seq, N_HEADS * HEAD_DIM), BF16),
        compiler_params=_cparams(("parallel", "parallel"), 48),
        name="window_attention",
    )(sink.astype(F32), q, k2, k2, k2, v2, v2, v2, bm)


def _router_body(x_ref, g_ref, wr_ref, tri_ref, hp_ref, idx_ref, rank_ref, gate_ref, cnt_ref, run_ref):
    @pl.when(pl.program_id(0) == 0)
    def _():
        run_ref[...] = jnp.zeros_like(run_ref)

    h = _rms_scale(x_ref[...]) * g_ref[...]
    half = D_MODEL // 2
    bits = lax.bitcast_convert_type(h.astype(BF16).astype(F32), U32)
    hp_ref[...] = (bits[:, half:] & jnp.uint32(0xFFFF0000)) | (bits[:, :half] >> 16)

    logits = lax.dot_general(wr_ref[...], h, NT_DIMS, precision=lax.Precision.HIGHEST,
                             preferred_element_type=F32)
    eid = lax.broadcasted_iota(I32, logits.shape, 0)
    m1 = jnp.max(logits, axis=0, keepdims=True)
    i1 = jnp.min(jnp.where(logits == m1, eid, N_EXPERTS), axis=0, keepdims=True)
    rest = jnp.where(eid == i1, -jnp.inf, logits)
    m2 = jnp.max(rest, axis=0, keepdims=True)
    i2 = jnp.min(jnp.where(rest == m2, eid, N_EXPERTS), axis=0, keepdims=True)
    e2 = jnp.exp(m2 - m1)
    gate_ref[0:1, :] = 1.0 / (1.0 + e2)
    gate_ref[1:2, :] = e2 / (1.0 + e2)
    idx_ref[0:1, :] = i1
    idx_ref[1:2, :] = i2

    sel1 = eid == i1
    sel2 = eid == i2
    onehot = jnp.where(sel1 | sel2, 1.0, 0.0)
    before = _dot(onehot.astype(BF16), tri_ref[...]) + run_ref[:, 0:1]
    rank_ref[0:1, :] = jnp.sum(jnp.where(sel1, before, 0.0), axis=0, keepdims=True).astype(I32)
    rank_ref[1:2, :] = jnp.sum(jnp.where(sel2, before, 0.0), axis=0, keepdims=True).astype(I32)
    run_ref[...] += jnp.sum(onehot, axis=1, keepdims=True)
    cnt_ref[...] = run_ref[...].astype(I32)


def _router(x2d, g, w_router):
    t = x2d.shape[0]
    tm = TM_ROUTER
    tri = (jnp.arange(tm)[:, None] < jnp.arange(tm)[None, :]).astype(BF16)
    two = lambda dt: jax.ShapeDtypeStruct((2, t), dt)
    return pl.pallas_call(
        _router_body,
        grid=(t // tm,),
        in_specs=[
            pl.BlockSpec((tm, D_MODEL), lambda i: (i, 0)),
            pl.BlockSpec((1, D_MODEL), lambda i: (0, 0)),
            pl.BlockSpec((N_EXPERTS, D_MODEL), lambda i: (0, 0)),
            pl.BlockSpec((tm, tm), lambda i: (0, 0)),
        ],
        out_specs=[
            pl.BlockSpec((tm, D_MODEL // 2), lambda i: (i, 0)),
            pl.BlockSpec((2, tm), lambda i: (0, i)),
            pl.BlockSpec((2, tm), lambda i: (0, i)),
            pl.BlockSpec((2, tm), lambda i: (0, i)),
            pl.BlockSpec((N_EXPERTS, LANES), lambda i: (0, 0)),
        ],
        out_shape=[
            jax.ShapeDtypeStruct((t, D_MODEL // 2), U32),
            two(I32), two(I32), two(F32),
            jax.ShapeDtypeStruct((N_EXPERTS, LANES), I32),
        ],
        scratch_shapes=[pltpu.VMEM((N_EXPERTS, LANES), F32)],
        compiler_params=_cparams(("arbitrary",), 48),
        name="moe_router",
    )(x2d, g, w_router.astype(F32).T, tri)


def _dispatch_body(d1_ref, d2_ref, hp_ref, init_ref, xs_ref, sem, *, rows):
    del init_ref

    def issue(r, c):
        pltpu.make_async_copy(hp_ref.at[pl.ds(r, 1)], xs_ref.at[pl.ds(d1_ref[0, 0, r], 1)], sem.at[0]).start()
        pltpu.make_async_copy(hp_ref.at[pl.ds(r, 1)], xs_ref.at[pl.ds(d2_ref[0, 0, r], 1)], sem.at[1]).start()
        return c
    lax.fori_loop(0, rows, issue, 0, unroll=8)
    pltpu.make_async_copy(hp_ref, xs_ref.at[pl.ds(0, rows)], sem.at[0]).wait()
    pltpu.make_async_copy(hp_ref, xs_ref.at[pl.ds(0, rows)], sem.at[1]).wait()


def _dispatch(hp, dest, n_rows):
    t, width = hp.shape
    rows = TM_DISPATCH
    idx_spec = pl.BlockSpec((1, 1, rows), lambda i: (i, 0, 0), memory_space=pltpu.SMEM)
    return pl.pallas_call(
        functools.partial(_dispatch_body, rows=rows),
        grid=(t // rows,),
        in_specs=[idx_spec, idx_spec, pl.BlockSpec((rows, width), lambda i: (i, 0)),
                  pl.BlockSpec(memory_space=pl.ANY)],
        out_specs=pl.BlockSpec(memory_space=pl.ANY),
        out_shape=jax.ShapeDtypeStruct((n_rows, width), hp.dtype),
        input_output_aliases={3: 0},
        scratch_shapes=[pltpu.SemaphoreType.DMA((2,))],
        compiler_params=_cparams(("arbitrary",), 32),
        name="moe_dispatch",
    )(dest[0].reshape(t // rows, 1, rows), dest[1].reshape(t // rows, 1, rows), hp,
      jnp.zeros((n_rows, width), hp.dtype))


def _expert_body(te_ref, nu_ref, xs_ref, wg_ref, wu_ref, wd_ref, o_ref, xb_ref, acc_ref):
    i = pl.program_id(0)
    j = pl.program_id(1)
    half = D_MODEL // 2

    @pl.when(j == 0)
    def _():
        w = xs_ref[...]
        xb_ref[:, :half] = lax.bitcast_convert_type(w << 16, F32).astype(BF16)
        xb_ref[:, half:] = lax.bitcast_convert_type(w & jnp.uint32(0xFFFF0000), F32).astype(BF16)
        acc_ref[...] = jnp.zeros_like(acc_ref)

    @pl.when(i < nu_ref[0])
    def _():
        xb = xb_ref[...]
        a = (jax.nn.silu(_dot(xb, wg_ref[...])) * _dot(xb, wu_ref[...])).astype(BF16)
        acc_ref[...] += _dot(a, wd_ref[...])

    @pl.when(j == pl.num_programs(1) - 1)
    def _():
        o_ref[...] = acc_ref[...]


def _experts(xs, tile_expert, n_used, wg, wu, wd):
    p = xs.shape[0]
    tm, tf = TM_MOE, TF_MOE
    grid_spec = pltpu.PrefetchScalarGridSpec(
        num_scalar_prefetch=2,
        grid=(p // tm, D_FF_EXPERT // tf),
        in_specs=[
            pl.BlockSpec((tm, D_MODEL // 2), lambda i, j, te, nu: (i, 0)),
            pl.BlockSpec((None, D_MODEL, tf), lambda i, j, te, nu: (te[i], 0, j)),
            pl.BlockSpec((None, D_MODEL, tf), lambda i, j, te, nu: (te[i], 0, j)),
            pl.BlockSpec((None, tf, D_MODEL), lambda i, j, te, nu: (te[i], j, 0)),
        ],
        out_specs=pl.BlockSpec((tm, D_MODEL), lambda i, j, te, nu: (i, 0)),
        scratch_shapes=[pltpu.VMEM((tm, D_MODEL), BF16), pltpu.VMEM((tm, D_MODEL), F32)],
    )
    return pl.pallas_call(
        _expert_body,
        grid_spec=grid_spec,
        out_shape=jax.ShapeDtypeStruct((p, D_MODEL), F32),
        compiler_params=_cparams(("parallel", "arbitrary"), 56),
        name="moe_experts",
    )(tile_expert, n_used, xs, wg, wu, wd)


def _combine_body(d1c_ref, d2c_ref, d1n_ref, d2n_ref, x_ref, g1_ref, g2_ref, ys_ref, oa_ref, ob_ref,
                  y1_ref, y2_ref, sem, *, rows, na_blocks):
    i = pl.program_id(0)
    slot = i % 2

    def gather(d1_ref, d2_ref, s):
        def issue(r, c):
            pltpu.make_async_copy(ys_ref.at[pl.ds(d1_ref[0, 0, r], 1)], y1_ref.at[s, pl.ds(r, 1)], sem.at[0, s]).start()
            pltpu.make_async_copy(ys_ref.at[pl.ds(d2_ref[0, 0, r], 1)], y2_ref.at[s, pl.ds(r, 1)], sem.at[1, s]).start()
            return c
        lax.fori_loop(0, rows, issue, 0, unroll=8)

    @pl.when(i == 0)
    def _():
        gather(d1c_ref, d2c_ref, 0)

    @pl.when(i + 1 < pl.num_programs(0))
    def _():
        gather(d1n_ref, d2n_ref, 1 - slot)

    pltpu.make_async_copy(ys_ref.at[pl.ds(0, rows)], y1_ref.at[slot], sem.at[0, slot]).wait()
    pltpu.make_async_copy(ys_ref.at[pl.ds(0, rows)], y2_ref.at[slot], sem.at[1, slot]).wait()
    val = x_ref[...] + g1_ref[...] * y1_ref[slot] + g2_ref[...] * y2_ref[slot]

    @pl.when(i < na_blocks)
    def _():
        oa_ref[...] = val

    @pl.when(i >= na_blocks)
    def _():
        ob_ref[...] = val


def _combine(x2d, ys, dest, gates, t_a):
    t = x2d.shape[0]
    rows = TM_COMBINE
    n = t // rows
    na = t_a // rows
    cur = pl.BlockSpec((1, 1, rows), lambda i: (i, 0, 0), memory_space=pltpu.SMEM)
    nxt = pl.BlockSpec((1, 1, rows), lambda i: (jnp.minimum(i + 1, n - 1), 0, 0), memory_space=pltpu.SMEM)
    gate_spec = pl.BlockSpec((rows, 1), lambda i: (i, 0))
    row_spec = pl.BlockSpec((rows, D_MODEL), lambda i: (i, 0))
    d1 = dest[0].reshape(n, 1, rows)
    d2 = dest[1].reshape(n, 1, rows)
    return pl.pallas_call(
        functools.partial(_combine_body, rows=rows, na_blocks=na),
        grid=(n,),
        in_specs=[cur, cur, nxt, nxt, row_spec, gate_spec, gate_spec, pl.BlockSpec(memory_space=pl.ANY)],
        out_specs=[pl.BlockSpec((rows, D_MODEL), lambda i: (jnp.minimum(i, na - 1), 0)),
                   pl.BlockSpec((rows, D_MODEL), lambda i: (jnp.maximum(i - na, 0), 0))],
        out_shape=[jax.ShapeDtypeStruct((t_a, D_MODEL), F32), jax.ShapeDtypeStruct((t - t_a, D_MODEL), F32)],
        scratch_shapes=[pltpu.VMEM((2, rows, D_MODEL), F32), pltpu.VMEM((2, rows, D_MODEL), F32),
                        pltpu.SemaphoreType.DMA((2, 2))],
        compiler_params=_cparams(("arbitrary",), 48),
        name="moe_combine",
    )(d1, d2, d1, d2, x2d, gates[0][:, None], gates[1][:, None], ys)


def _moe(x2d, g, w_router, wg, wu, wd, t_a):
    t = x2d.shape[0]
    tm = TM_MOE
    hp, idx, rank, gates, cnt = _router(x2d, g, w_router)
    counts = cnt[:, 0]
    tiles = (counts + tm - 1) // tm
    tile_end = jnp.cumsum(tiles)
    row_start = (tile_end - tiles) * tm
    n_tiles = (2 * t) // tm + N_EXPERTS
    eid = jnp.arange(N_EXPERTS, dtype=I32)[:, None, None]
    dest = jnp.sum(jnp.where(idx[None] == eid, row_start[:, None, None], 0), axis=0) + rank
    tile_expert = jnp.minimum(jnp.searchsorted(tile_end, jnp.arange(n_tiles), side="right"), N_EXPERTS - 1)
    xs = _dispatch(hp, dest, n_tiles * tm)
    ys = _experts(xs, tile_expert.astype(I32), tile_end[-1:].astype(I32), wg, wu, wd)
    return _combine(x2d, ys, dest, gates, t_a)


def _even_layer(xa, xb, norm_mix, norm_ffn, w_in, w_out, ln_g, ln_b, sgu_w, sgu_b, hy_conv,
                f_w1, f_b1, f_w2, f_b2, f_w3, f_b3, f_freq, hy_skip, wg, wu, wd):
    seq = xa.shape[1]
    bsz = xa.shape[0] + xb.shape[0]
    nh = seq // LANES
    sgu_bb = jnp.broadcast_to(sgu_b.astype(F32)[:, :, None], (A_GROUPS, CHUNK, LANES))
    w_in = w_in.astype(BF16)
    ya, hbt = _even_in(xa, xb, norm_mix[None, :], w_in[:, :2 * A_WIDTH], w_in[:, 2 * A_WIDTH:].T,
                       ln_g[None, :], ln_b[None, :], sgu_w.astype(BF16), sgu_bb)
    hbt = hbt.reshape(bsz, -1, nh, LANES)

    tb = _dft_tables(seq)
    kt = _hyena_kernels(seq, f_w1, f_b1, f_w2, f_b2, f_w3, f_b3, f_freq)
    kf = _filter_spectrum(kt.reshape(HYENA_ORDER, B_WIDTH, 2 * nh, LANES), tb)

    taps = hy_conv.astype(F32)
    skip = hy_skip.astype(F32)
    z1 = _hyena_conv(hbt, 0, hbt, B_WIDTH, taps, skip[0], kf[0], tb, conv_z=True)
    ybt = _hyena_conv(z1, 0, hbt, 2 * B_WIDTH, taps, skip[1], kf[1], tb, conv_z=False)

    w_out = w_out.astype(BF16)
    x = _out_proj((xa, xb), ya, w_out[:A_WIDTH], ybt.reshape(bsz, B_WIDTH, seq), w_out[A_WIDTH:])
    x2d = _ffn(x.reshape(bsz * seq, D_MODEL), norm_ffn[None, :], wg.astype(BF16), wu.astype(BF16), wd.astype(BF16))
    return x2d.reshape(bsz, seq, D_MODEL)


def _odd_layer(x, n_a, norm_mix, norm_ffn, w_qkv, q_g, k_g, sink, w_out, rel_bias, w_router, wg, wu, wd):
    bsz, seq, _ = x.shape
    q, k2, v2 = _qkv_proj(x, norm_mix[None, :], w_qkv, q_g, k_g)
    o = _attention(q, k2, v2, sink, rel_bias)
    x = _out_proj(x, o, w_out.astype(BF16))
    ya, yb = _moe(x.reshape(bsz * seq, D_MODEL), norm_ffn[None, :], w_router,
                  wg.astype(BF16), wu.astype(BF16), wd.astype(BF16), n_a * seq)
    return ya.reshape(n_a, seq, D_MODEL), yb.reshape(bsz - n_a, seq, D_MODEL)


def kernel(x_prompt, x_sample, norm_mix, norm_ffn, ev_w_in, ev_w_out, sgu_ln_g, sgu_ln_b, sgu_w, sgu_b,
           hy_conv, hy_f_w1, hy_f_b1, hy_f_w2, hy_f_b2, hy_f_w3, hy_f_b3, hy_f_freq, hy_skip,
           ffn_w_gate, ffn_w_up, ffn_w_down, at_w_qkv, at_q_norm, at_k_norm, at_sink, at_w_out,
           rel_bias, moe_router, moe_w_gate, moe_w_up, moe_w_down):
    even_p = (ev_w_in, ev_w_out, sgu_ln_g, sgu_ln_b, sgu_w, sgu_b, hy_conv, hy_f_w1, hy_f_b1, hy_f_w2,
              hy_f_b2, hy_f_w3, hy_f_b3, hy_f_freq, hy_skip, ffn_w_gate, ffn_w_up, ffn_w_down)
    odd_p = (at_w_qkv, at_q_norm, at_k_norm, at_sink, at_w_out, moe_router, moe_w_gate, moe_w_up, moe_w_down)
    assert x_prompt.shape[1:] == x_sample.shape[1:]
    n_a = x_prompt.shape[0]
    depth = norm_mix.shape[0]
    assert depth % 2 == 0, "layers come in (even, odd) pairs"
    xa, xb = x_prompt, x_sample
    for i in range(0, depth, 2):
        j = i // 2
        x = _even_layer(xa, xb, norm_mix[i], norm_ffn[i], *[p[j] for p in even_p])
        xa, xb = _odd_layer(x, n_a, norm_mix[i + 1], norm_ffn[i + 1], *[p[j] for p in odd_p[:5]], rel_bias,
                            *[p[j] for p in odd_p[5:]])
    return (xa, xb)
```

```python
import functools
import math

import jax
import jax.numpy as jnp
from jax import lax
from jax.experimental import pallas as pl
from jax.experimental.pallas import tpu as pltpu

F32 = jnp.float32
BF16 = jnp.bfloat16
U32 = jnp.uint32
I32 = jnp.int32

D_MODEL = 1024
A_GROUPS = 4
A_WIDTH = D_MODEL // 2
CHUNK = 128
B_WIDTH = D_MODEL // 2
HYENA_ORDER = 2
FILTER_DIRS = 2
FILTER_BANDS = 16
FILTER_EMB = 1 + 2 * FILTER_BANDS
FILTER_HIDDEN = 64
DECAY_TARGET = 1e-2
FAST_DECAY_PCT = 0.3
SLOW_DECAY_PCT = 1.5
HEAD_DIM = 64
N_HEADS = D_MODEL // HEAD_DIM
N_KV_HEADS = N_HEADS // 4
GQA_GROUP = N_HEADS // N_KV_HEADS
WINDOW = 128
ATT_BLOCK = 128
REL_BUCKETS = 32
REL_MAX_DIST = 128
NEG_INF = -1e30
D_FF = 2816
N_EXPERTS = 8
D_FF_EXPERT = 3584
EPS = 1e-6

LANES = 128
SUBLANES = 8
MIB = 1024 * 1024

TM_PROJ = 512
TM_FFN = 512
FF_CHUNK = 256
TM_MOE = 512
TF_MOE = D_FF_EXPERT // 2
TM_ROUTER = 512
TM_COMBINE = 256
TM_DISPATCH = 512
HY_CB = 32
HY_PAIRS = 4

NT_DIMS = (((1,), (1,)), ((), ()))
TN_DIMS = (((0,), (0,)), ((), ()))


def _cparams(sem, vmem_mib):
    return pltpu.CompilerParams(dimension_semantics=sem, vmem_limit_bytes=vmem_mib * MIB)


def _rms_scale(x):
    return x * lax.rsqrt(jnp.mean(x * x, axis=-1, keepdims=True) + EPS)


def _dot(a, b):
    return jnp.dot(a, b, preferred_element_type=F32)


def _group_specs(xa, xb, tm):
    na, seq = xa.shape[0], xa.shape[1]
    last = seq // tm - 1
    width = xa.shape[2]
    spec_a = pl.BlockSpec((None, tm, width), lambda b, i: (jnp.minimum(b, na - 1), jnp.where(b < na, i, last), 0))
    spec_b = pl.BlockSpec((None, tm, width), lambda b, i: (jnp.maximum(b - na, 0), jnp.where(b < na, 0, i), 0))
    return spec_a, spec_b


def _group_tile(xa_ref, xb_ref, na):
    return jnp.where(pl.program_id(0) < na, xa_ref[...], xb_ref[...])


def _even_in_body(xa_ref, xb_ref, g_ref, w_ref, wht_ref, lng_ref, lnb_ref, sw_ref, sb_ref, ya_ref, hbt_ref, *, na):
    x = _group_tile(xa_ref, xb_ref, na)
    h = (_rms_scale(x) * g_ref[...]).astype(BF16)
    u = jax.nn.gelu(_dot(h, w_ref[:, 0:A_WIDTH]))
    v = jax.nn.gelu(_dot(h, w_ref[:, A_WIDTH:2 * A_WIDTH]))
    hbt_ref[...] = lax.dot_general(wht_ref[...], h, NT_DIMS, preferred_element_type=F32)
    tm = x.shape[0]
    for gi in range(A_GROUPS):
        cs = slice(gi * LANES, (gi + 1) * LANES)
        vg = v[:, cs]
        xc = vg - jnp.mean(vg, axis=-1, keepdims=True)
        var = jnp.mean(xc * xc, axis=-1, keepdims=True)
        vn = (xc * lax.rsqrt(var + EPS) * lng_ref[:, cs] + lnb_ref[:, cs]).astype(BF16)
        for c in range(tm // CHUNK):
            rs = slice(c * CHUNK, (c + 1) * CHUNK)
            mixed = _dot(sw_ref[gi], vn[rs]) + sb_ref[gi]
            ya_ref[rs, cs] = (u[rs, cs] * mixed).astype(ya_ref.dtype)


def _even_in(xa, xb, g, w_uv, w_hb_t, ln_g, ln_b, sgu_w, sgu_b):
    na, seq, _ = xa.shape
    bsz = na + xb.shape[0]
    tm = TM_PROJ
    n_hb = w_hb_t.shape[0]
    const = lambda *shape: pl.BlockSpec(shape, lambda b, i: (0,) * len(shape))
    return pl.pallas_call(
        functools.partial(_even_in_body, na=na),
        grid=(bsz, seq // tm),
        in_specs=[
            *_group_specs(xa, xb, tm),
            const(1, D_MODEL),
            const(D_MODEL, 2 * A_WIDTH),
            const(n_hb, D_MODEL),
            const(1, A_WIDTH),
            const(1, A_WIDTH),
            const(A_GROUPS, CHUNK, CHUNK),
            const(A_GROUPS, CHUNK, LANES),
        ],
        out_specs=[
            pl.BlockSpec((None, tm, A_WIDTH), lambda b, i: (b, i, 0)),
            pl.BlockSpec((None, n_hb, tm), lambda b, i: (b, 0, i)),
        ],
        out_shape=[
            jax.ShapeDtypeStruct((bsz, seq, A_WIDTH), BF16),
            jax.ShapeDtypeStruct((bsz, n_hb, seq), F32),
        ],
        compiler_params=_cparams(("parallel", "parallel"), 48),
        name="even_in",
    )(xa, xb, g, w_uv, w_hb_t, ln_g, ln_b, sgu_w, sgu_b)


def _filter_body(ft_ref, t_ref, w1_ref, b1_ref, w2_ref, b2_ref, w3t_ref, b3_ref, fr0_ref, fr1_ref, absd_ref,
                 o_ref, *, seq):
    hp = lax.Precision.HIGHEST
    tl = ft_ref.shape[0]
    h = jnp.sin(fr0_ref[...] * (jnp.dot(ft_ref[...], w1_ref[...], precision=hp, preferred_element_type=F32)
                                + b1_ref[...]))
    h = jnp.sin(fr1_ref[...] * (jnp.dot(h, w2_ref[...], precision=hp, preferred_element_type=F32) + b2_ref[...]))
    out = lax.dot_general(w3t_ref[...], h, NT_DIMS, precision=hp, preferred_element_type=F32) + b3_ref[...]
    decay = jnp.exp(-absd_ref[...] * t_ref[...])
    pos = pl.program_id(0) * tl + lax.broadcasted_iota(I32, (1, tl), 1)
    for o in range(HYENA_ORDER):
        o_ref[o] = jnp.where(pos == seq, 0.0, out[o * B_WIDTH:(o + 1) * B_WIDTH] * decay)


def _hyena_kernels(seq, w1, b1, w2, b2, w3, b3, freq):
    t = jnp.linspace(0.0, 1.0, seq, dtype=F32)[:, None]
    w = 2.0 * math.pi * jnp.arange(seq, dtype=F32)[:, None] / seq
    bands = jnp.linspace(1e-4, FILTER_BANDS - 1, FILTER_BANDS, dtype=F32)[None, :]
    feats = jnp.concatenate([t, jnp.cos(bands * w), jnp.sin(bands * w)], axis=-1)
    feats = jnp.concatenate([feats, feats[:1], feats[:0:-1]], axis=0)
    t_row = feats[:, 0][None, :]
    feats = jnp.pad(feats, ((0, 0), (0, LANES - FILTER_EMB)))
    hpad = LANES - FILTER_HIDDEN
    w1p = jnp.pad(w1.astype(F32), ((0, LANES - FILTER_EMB), (0, hpad)))
    w2p = jnp.pad(w2.astype(F32), ((0, hpad), (0, hpad)))
    n_dir = HYENA_ORDER * B_WIDTH
    w3t = jnp.pad(w3.astype(F32), ((0, hpad), (0, 0))).T.reshape(FILTER_DIRS, n_dir, LANES)
    b3c = b3.astype(F32).reshape(FILTER_DIRS, n_dir, 1)
    b1p = jnp.pad(b1.astype(F32), (0, hpad))[None, :]
    b2p = jnp.pad(b2.astype(F32), (0, hpad))[None, :]
    fr = jnp.pad(freq.astype(F32), ((0, 0), (0, hpad)))
    deltas = jnp.linspace(math.log(DECAY_TARGET) / SLOW_DECAY_PCT,
                          math.log(DECAY_TARGET) / FAST_DECAY_PCT, B_WIDTH, dtype=F32)
    absd = jnp.abs(deltas)[:, None]
    tl = min(seq, 1024)
    nhalf = seq // tl
    const = lambda *shape: pl.BlockSpec(shape, lambda i: (0,) * len(shape))
    return pl.pallas_call(
        functools.partial(_filter_body, seq=seq),
        grid=(2 * nhalf,),
        in_specs=[
            pl.BlockSpec((tl, LANES), lambda i: (i, 0)),
            pl.BlockSpec((1, tl), lambda i: (0, i)),
            const(LANES, LANES), const(1, LANES), const(LANES, LANES), const(1, LANES),
            pl.BlockSpec((None, n_dir, LANES), lambda i: (i // nhalf, 0, 0)),
            pl.BlockSpec((None, n_dir, 1), lambda i: (i // nhalf, 0, 0)),
            const(1, LANES), const(1, LANES), const(B_WIDTH, 1),
        ],
        out_specs=pl.BlockSpec((HYENA_ORDER, B_WIDTH, tl), lambda i: (0, 0, i)),
        out_shape=jax.ShapeDtypeStruct((HYENA_ORDER, B_WIDTH, 2 * seq), F32),
        compiler_params=_cparams(("parallel",), 48),
        name="hyena_filter",
    )(feats, t_row, w1p, b1p, w2p, b2p, w3t, b3c, fr[0:1], fr[1:2], absd)


def _dft_tables(seq):
    n = 2 * seq
    nb = n // LANES
    k1 = jnp.arange(nb, dtype=I32)[:, None]
    ang1 = (2.0 * math.pi / nb) * ((k1 * jnp.arange(nb, dtype=I32)[None, :]) % nb).astype(F32)
    f1_full = jnp.concatenate([jnp.cos(ang1), -jnp.sin(ang1)], axis=0)
    f1_half = f1_full[:, : nb // 2]
    f4_half = f1_half.T
    angt = (2.0 * math.pi / n) * ((k1 * jnp.arange(LANES, dtype=I32)[None, :]) % n).astype(F32)
    twr, twi = jnp.cos(angt), -jnp.sin(angt)
    a = jnp.arange(LANES, dtype=I32)
    ang2 = (2.0 * math.pi / LANES) * ((a[:, None] * a[None, :]) % LANES).astype(F32)
    cr, ci = jnp.cos(ang2), -jnp.sin(ang2)
    m2 = jnp.concatenate([jnp.concatenate([cr, ci], axis=1), jnp.concatenate([-ci, cr], axis=1)], axis=0)
    m3 = jnp.concatenate([jnp.concatenate([cr, -ci], axis=1), jnp.concatenate([ci, cr], axis=1)], axis=0)
    return dict(f1_full=f1_full.astype(BF16), f1_half=f1_half.astype(BF16), f4_half=f4_half.astype(BF16),
                twr=twr, twi=twi, m2=m2.astype(BF16), m3=m3.astype(BF16))


def _fwd_spectrum(pairs, f1_ref, twr_ref, twi_ref, m2_ref):
    nb = twr_ref.shape[0]
    a_all = [_dot(f1_ref[...], jnp.concatenate(xs, axis=1).astype(BF16)) for xs in pairs]
    twr, twi = twr_ref[...], twi_ref[...]
    out = []
    for a in a_all:
        lhs = []
        for d in range(2):
            ar = a[:nb, d * LANES:(d + 1) * LANES]
            ai = a[nb:, d * LANES:(d + 1) * LANES]
            lhs.append(jnp.concatenate([ar * twr - ai * twi, ar * twi + ai * twr], axis=1))
        out.append(_dot(jnp.concatenate(lhs, axis=0).astype(BF16), m2_ref[...]))
    return out


def _spec_body(k_ref, f1_ref, twr_ref, twi_ref, m2_ref, o_ref):
    nb = twr_ref.shape[0]
    inv_n = 1.0 / (nb * LANES)

    def group(it, carry):
        c0 = 2 * HY_PAIRS * it
        pairs = [[k_ref[c0 + 2 * g], k_ref[c0 + 2 * g + 1]] for g in range(HY_PAIRS)]
        for g, z in enumerate(_fwd_spectrum(pairs, f1_ref, twr_ref, twi_ref, m2_ref)):
            o_ref[c0 + 2 * g] = z[:nb] * inv_n
            o_ref[c0 + 2 * g + 1] = z[nb:] * inv_n
        return carry
    lax.fori_loop(0, k_ref.shape[0] // (2 * HY_PAIRS), group, 0)


def _filter_spectrum(kt, tb):
    orders, c, nb, _ = kt.shape
    cb = HY_CB
    const = lambda *shape: pl.BlockSpec(shape, lambda o, j: (0,) * len(shape))
    return pl.pallas_call(
        _spec_body,
        grid=(orders, c // cb),
        in_specs=[
            pl.BlockSpec((None, cb, nb, LANES), lambda o, j: (o, j, 0, 0)),
            const(2 * nb, nb), const(nb, LANES), const(nb, LANES), const(2 * LANES, 2 * LANES),
        ],
        out_specs=pl.BlockSpec((None, cb, nb, 2 * LANES), lambda o, j: (o, j, 0, 0)),
        out_shape=jax.ShapeDtypeStruct((orders, c, nb, 2 * LANES), F32),
        compiler_params=_cparams(("parallel", "parallel"), 48),
        name="hyena_filter_spectrum",
    )(kt, tb["f1_full"], tb["twr"], tb["twi"], tb["m2"])


def _shift_rows(x, down):
    rows = x.shape[0]
    idx = lax.broadcasted_iota(I32, x.shape, 0)
    if down:
        return jnp.where(idx == 0, 0.0, pltpu.roll(x, 1, axis=0))
    return jnp.where(idx == rows - 1, 0.0, pltpu.roll(x, rows - 1, axis=0))


def _short_conv(x, taps_ref, ch):
    lane = lax.broadcasted_iota(I32, x.shape, 1)
    prev = pltpu.roll(jnp.where(lane == LANES - 1, _shift_rows(x, True), x), 1, axis=1)
    nxt = pltpu.roll(jnp.where(lane == 0, _shift_rows(x, False), x), LANES - 1, axis=1)
    return taps_ref[0, ch] * prev + taps_ref[1, ch] * x + taps_ref[2, ch] * nxt


def _conv_body(taps_ref, skip_ref, z_ref, g_ref, kf_ref, f1_ref, f4_ref, twr_ref, twi_ref, m2_ref, m3_ref,
               o_ref, *, zch, gch, conv_z):
    nb = twr_ref.shape[0]
    cb = z_ref.shape[0]
    base = pl.program_id(0) * cb

    def group(it, carry):
        c0 = 2 * HY_PAIRS * it
        chans = [[c0 + 2 * g, c0 + 2 * g + 1] for g in range(HY_PAIRS)]
        xs = [[_short_conv(z_ref[c], taps_ref, zch + base + c) if conv_z else z_ref[c] for c in pr] for pr in chans]
        zs = _fwd_spectrum(xs, f1_ref, twr_ref, twi_ref, m2_ref)
        ccs = []
        for pr, z in zip(chans, zs):
            ys = []
            for d, c in enumerate(pr):
                zr = z[d * nb:(d + 1) * nb, :LANES]
                zi = z[d * nb:(d + 1) * nb, LANES:]
                kf = kf_ref[c]
                kr, ki = kf[:, :LANES], kf[:, LANES:]
                ys.append(jnp.concatenate([zr * kr - zi * ki, zr * ki + zi * kr], axis=1))
            ccs.append(_dot(jnp.concatenate(ys, axis=0).astype(BF16), m3_ref[...]))
        twr, twi = twr_ref[...], twi_ref[...]
        for pr, x2, cc in zip(chans, xs, ccs):
            drs, dis = [], []
            for d in range(2):
                ccr = cc[d * nb:(d + 1) * nb, :LANES]
                cci = cc[d * nb:(d + 1) * nb, LANES:]
                drs.append(ccr * twr + cci * twi)
                dis.append(cci * twr - ccr * twi)
            rhs = jnp.concatenate([jnp.concatenate(drs, axis=1), jnp.concatenate(dis, axis=1)], axis=0)
            y = _dot(f4_ref[...], rhs.astype(BF16))
            for d, c in enumerate(pr):
                gate = _short_conv(g_ref[c], taps_ref, gch + base + c)
                o_ref[c] = gate * (y[:, d * LANES:(d + 1) * LANES] + skip_ref[base + c] * x2[d])
        return carry
    lax.fori_loop(0, cb // (2 * HY_PAIRS), group, 0)


def _hyena_conv(zsrc, zch, gsrc, gch, taps, skip, kf, tb, conv_z):
    bsz, _, nh, _ = zsrc.shape
    nb = 2 * nh
    cb = HY_CB
    smem = pl.BlockSpec(memory_space=pltpu.SMEM)
    const = lambda *shape: pl.BlockSpec(shape, lambda j, b: (0,) * len(shape))
    zblk, gblk = zch // cb, gch // cb
    return pl.pallas_call(
        functools.partial(_conv_body, zch=zch, gch=gch, conv_z=conv_z),
        grid=(B_WIDTH // cb, bsz),
        in_specs=[
            smem, smem,
            pl.BlockSpec((None, cb, nh, LANES), lambda j, b: (b, zblk + j, 0, 0)),
            pl.BlockSpec((None, cb, nh, LANES), lambda j, b: (b, gblk + j, 0, 0)),
            pl.BlockSpec((cb, nb, 2 * LANES), lambda j, b: (j, 0, 0)),
            const(2 * nb, nh), const(nh, 2 * nb), const(nb, LANES), const(nb, LANES),
            const(2 * LANES, 2 * LANES), const(2 * LANES, 2 * LANES),
        ],
        out_specs=pl.BlockSpec((None, cb, nh, LANES), lambda j, b: (b, j, 0, 0)),
        out_shape=jax.ShapeDtypeStruct((bsz, B_WIDTH, nh, LANES), F32),
        compiler_params=_cparams(("parallel", "parallel"), 48),
        name="hyena_conv",
    )(taps, skip, zsrc, gsrc, kf, tb["f1_half"], tb["f4_half"], tb["twr"], tb["twi"], tb["m2"], tb["m3"])


def _mix_ffn_body(xa_ref, xb_ref, ya_ref, ybt_ref, wa_ref, wb_ref, g_ref, wg_ref, wu_ref, wd_ref, o_ref, h_ref,
                  *, na_tiles):
    x = jnp.where(pl.program_id(0) < na_tiles, xa_ref[...], xb_ref[...])
    x = x + _dot(ya_ref[...], wa_ref[...])
    x = x + lax.dot_general(ybt_ref[...].astype(BF16), wb_ref[...], TN_DIMS, preferred_element_type=F32)
    h_ref[...] = (_rms_scale(x) * g_ref[...]).astype(BF16)
    o_ref[...] = x

    def chunk(c, carry):
        cs = pl.ds(pl.multiple_of(c * FF_CHUNK, FF_CHUNK), FF_CHUNK)
        h = h_ref[...]
        a = (jax.nn.silu(_dot(h, wg_ref[:, cs])) * _dot(h, wu_ref[:, cs])).astype(BF16)
        o_ref[...] += _dot(a, wd_ref[cs, :])
        return carry
    lax.fori_loop(0, D_FF // FF_CHUNK, chunk, 0)


def _mix_ffn(xa, xb, ya, ybt, wa, wb, g, wg, wu, wd):
    na, seq, _ = xa.shape
    bsz = na + xb.shape[0]
    tm = TM_FFN
    tps = seq // tm
    na_tiles = na * tps
    flat = lambda v: v.reshape(-1, v.shape[-1])
    resident = lambda w: pl.BlockSpec(w.shape, lambda i: (0, 0), pipeline_mode=pl.Buffered(1))
    return pl.pallas_call(
        functools.partial(_mix_ffn_body, na_tiles=na_tiles),
        grid=(bsz * tps,),
        in_specs=[
            pl.BlockSpec((tm, D_MODEL), lambda i: (jnp.minimum(i, na_tiles - 1), 0)),
            pl.BlockSpec((tm, D_MODEL), lambda i: (jnp.maximum(i - na_tiles, 0), 0)),
            pl.BlockSpec((tm, ya.shape[-1]), lambda i: (i, 0)),
            pl.BlockSpec((None, ybt.shape[1], tm), lambda i: (i // tps, 0, i % tps)),
            resident(wa), resident(wb), pl.BlockSpec((1, D_MODEL), lambda i: (0, 0)),
            resident(wg), resident(wu), resident(wd),
        ],
        out_specs=pl.BlockSpec((tm, D_MODEL), lambda i: (i, 0)),
        out_shape=jax.ShapeDtypeStruct((bsz * seq, D_MODEL), F32),
        scratch_shapes=[pltpu.VMEM((tm, D_MODEL), BF16)],
        compiler_params=_cparams(("parallel",), 56),
        name="mix_ffn",
    )(flat(xa), flat(xb), flat(ya), ybt, wa, wb, g, wg, wu, wd)


def _qkv_body(x_ref, g_ref, w_ref, qg_ref, kg_ref, q_ref, k_ref, v_ref):
    h = (_rms_scale(x_ref[...]) * g_ref[...]).astype(BF16)
    nq = N_HEADS * HEAD_DIM
    nk = N_KV_HEADS * LANES
    lo = lax.broadcasted_iota(I32, (1, LANES), 1) < HEAD_DIM
    qgain = qg_ref[...] * (HEAD_DIM ** -0.5)
    for c in range(nq // LANES):
        x = _dot(h, w_ref[:, c * LANES:(c + 1) * LANES])
        x2 = x * x
        s_lo = jnp.sum(jnp.where(lo, x2, 0.0), axis=-1, keepdims=True)
        s_hi = jnp.sum(jnp.where(lo, 0.0, x2), axis=-1, keepdims=True)
        r = jnp.where(lo, lax.rsqrt(s_lo / HEAD_DIM + EPS), lax.rsqrt(s_hi / HEAD_DIM + EPS))
        q_ref[:, c * LANES:(c + 1) * LANES] = (x * r * qgain).astype(q_ref.dtype)
    for c in range(N_KV_HEADS):
        x = _dot(h, w_ref[:, nq + c * LANES:nq + (c + 1) * LANES])
        k_ref[:, c * LANES:(c + 1) * LANES] = (_rms_scale(x) * kg_ref[...]).astype(k_ref.dtype)
    v_ref[...] = _dot(h, w_ref[:, nq + nk:]).astype(v_ref.dtype)


def _qkv_proj(x, g, w_qkv, q_g, k_g):
    bsz, seq, _ = x.shape
    tm = TM_PROJ
    nq = N_HEADS * HEAD_DIM
    nkv = N_KV_HEADS * HEAD_DIM
    dup = lambda w: jnp.tile(w.reshape(D_MODEL, N_KV_HEADS, 1, HEAD_DIM), (1, 1, 2, 1)).reshape(D_MODEL, 2 * nkv)
    w = jnp.concatenate([w_qkv[:, :nq], dup(w_qkv[:, nq:nq + nkv]), dup(w_qkv[:, nq + nkv:])], axis=1).astype(BF16)
    two = lambda v: jnp.tile(v.astype(F32), 2)[None, :]
    row = lambda width: pl.BlockSpec((None, tm, width), lambda b, i: (b, i, 0))
    const = lambda *shape: pl.BlockSpec(shape, lambda b, i: (0,) * len(shape))
    return pl.pallas_call(
        _qkv_body,
        grid=(bsz, seq // tm),
        in_specs=[row(D_MODEL), const(1, D_MODEL), const(D_MODEL, w.shape[1]), const(1, LANES), const(1, LANES)],
        out_specs=[row(nq), row(2 * nkv), row(2 * nkv)],
        out_shape=[jax.ShapeDtypeStruct((bsz, seq, nq), BF16), jax.ShapeDtypeStruct((bsz, seq, 2 * nkv), BF16),
                   jax.ShapeDtypeStruct((bsz, seq, 2 * nkv), BF16)],
        compiler_params=_cparams(("parallel", "parallel"), 48),
        name="qkv_proj",
    )(x, g, w, two(q_g), two(k_g))


def _t5_bucket(rel):
    nbk = REL_BUCKETS // 2
    max_exact = nbk // 2
    ret = jnp.where(rel > 0, nbk, 0)
    n = jnp.abs(rel)
    large = max_exact + (jnp.log(jnp.maximum(n, 1).astype(F32) / max_exact)
                         / math.log(REL_MAX_DIST / max_exact) * (nbk - max_exact)).astype(I32)
    large = jnp.minimum(large, nbk - 1)
    return ret + jnp.where(n < max_exact, n, large)


def _attn_body(sink_ref, q_ref, kp_ref, ko_ref, kn_ref, vp_ref, vo_ref, vn_ref, bm_ref, o_ref, *, seq):
    i = pl.program_id(1)
    kb = 3 * ATT_BLOCK
    kpos = (i - 1) * ATT_BLOCK + lax.broadcasted_iota(I32, (ATT_BLOCK, kb), 1)
    edge = jnp.where((kpos >= 0) & (kpos < seq), 0.0, NEG_INF)
    lo = lax.broadcasted_iota(I32, (1, LANES), 1) < HEAD_DIM
    first = lax.broadcasted_iota(I32, (2 * ATT_BLOCK, 1), 0) < ATT_BLOCK
    for hk in range(N_KV_HEADS):
        ks = slice(hk * LANES, (hk + 1) * LANES)
        kk = jnp.concatenate([kp_ref[:, ks], ko_ref[:, ks], kn_ref[:, ks]], axis=0)
        vv = jnp.concatenate([vp_ref[:, ks], vo_ref[:, ks], vn_ref[:, ks]], axis=0)
        zero = jnp.zeros_like(kk)
        c0 = 2 * hk
        ql = jnp.concatenate([q_ref[:, c0 * LANES:(c0 + 1) * LANES], q_ref[:, (c0 + 1) * LANES:(c0 + 2) * LANES]],
                             axis=0)
        acc = None
        for par in range(2):
            keep = lo if par == 0 else jnp.logical_not(lo)
            ha, hb = GQA_GROUP * hk + par, GQA_GROUP * hk + 2 + par
            s = lax.dot_general(ql, jnp.where(keep, kk, zero), NT_DIMS, preferred_element_type=F32)
            s = s + jnp.concatenate([bm_ref[ha] + edge, bm_ref[hb] + edge], axis=0)
            sk = jnp.where(first, sink_ref[ha], sink_ref[hb])
            m = jnp.maximum(jnp.max(s, axis=-1, keepdims=True), sk)
            pexp = jnp.exp(s - m)
            den = jnp.sum(pexp, axis=-1, keepdims=True) + jnp.exp(sk - m)
            o = _dot(pexp.astype(BF16), jnp.where(keep, vv, zero)) * (1.0 / den)
            acc = o if acc is None else acc + o
        o_ref[:, c0 * LANES:(c0 + 1) * LANES] = acc[:ATT_BLOCK].astype(o_ref.dtype)
        o_ref[:, (c0 + 1) * LANES:(c0 + 2) * LANES] = acc[ATT_BLOCK:].astype(o_ref.dtype)


def _attention(q, k2, v2, sink, rel_bias):
    bsz, seq, _ = q.shape
    nblk = seq // ATT_BLOCK
    kb = 3 * ATT_BLOCK
    rel = jnp.arange(kb)[None, :] - ATT_BLOCK - jnp.arange(ATT_BLOCK)[:, None]
    bucket = _t5_bucket(rel)
    rb = rel_bias.astype(F32)
    bias = sum(jnp.where(bucket[None] == b, rb[b][:, None, None], 0.0) for b in range(REL_BUCKETS))
    bm = jnp.where((jnp.abs(rel) <= WINDOW)[None], bias, NEG_INF)
    kvw = k2.shape[2]
    prev = lambda b, i: (b, jnp.maximum(i - 1, 0), 0)
    own = lambda b, i: (b, i, 0)
    nxt = lambda b, i: (b, jnp.minimum(i + 1, nblk - 1), 0)
    kv_spec = lambda fn: pl.BlockSpec((None, ATT_BLOCK, kvw), fn)
    return pl.pallas_call(
        functools.partial(_attn_body, seq=seq),
        grid=(bsz, nblk),
        in_specs=[
            pl.BlockSpec(memory_space=pltpu.SMEM),
            pl.BlockSpec((None, ATT_BLOCK, N_HEADS * HEAD_DIM), own),
            kv_spec(prev), kv_spec(own), kv_spec(nxt),
            kv_spec(prev), kv_spec(own), kv_spec(nxt),
            pl.BlockSpec((N_HEADS, ATT_BLOCK, kb), lambda b, i: (0, 0, 0)),
        ],
        out_specs=pl.BlockSpec((None, ATT_BLOCK, N_HEADS * HEAD_DIM), own),
        out_shape=jax.ShapeDtypeStruct((bsz, seq, N_HEADS * HEAD_DIM), BF16),
        compiler_params=_cparams(("parallel", "parallel"), 48),
        name="window_attention",
    )(sink.astype(F32), q, k2, k2, k2, v2, v2, v2, bm)


def _router_body(x_ref, o_ref, wo_ref, g_ref, wr_ref, tri_ref, xn_ref, hp_ref, idx_ref, rank_ref, gate_ref, cnt_ref,
                 run_ref):
    @pl.when(pl.program_id(0) == 0)
    def _():
        run_ref[...] = jnp.zeros_like(run_ref)

    x = x_ref[...] + _dot(o_ref[...], wo_ref[...])
    xn_ref[...] = x
    h = _rms_scale(x) * g_ref[...]
    half = D_MODEL // 2
    bits = lax.bitcast_convert_type(h.astype(BF16).astype(F32), U32)
    hp_ref[...] = (bits[:, half:] & jnp.uint32(0xFFFF0000)) | (bits[:, :half] >> 16)

    logits = lax.dot_general(wr_ref[...], h, NT_DIMS, precision=lax.Precision.HIGHEST,
                             preferred_element_type=F32)
    eid = lax.broadcasted_iota(I32, logits.shape, 0)
    m1 = jnp.max(logits, axis=0, keepdims=True)
    i1 = jnp.min(jnp.where(logits == m1, eid, N_EXPERTS), axis=0, keepdims=True)
    rest = jnp.where(eid == i1, -jnp.inf, logits)
    m2 = jnp.max(rest, axis=0, keepdims=True)
    i2 = jnp.min(jnp.where(rest == m2, eid, N_EXPERTS), axis=0, keepdims=True)
    e2 = jnp.exp(m2 - m1)
    gate_ref[0:1, :] = 1.0 / (1.0 + e2)
    gate_ref[1:2, :] = e2 / (1.0 + e2)
    idx_ref[0:1, :] = i1
    idx_ref[1:2, :] = i2

    sel1 = eid == i1
    sel2 = eid == i2
    onehot = jnp.where(sel1 | sel2, 1.0, 0.0)
    before = _dot(onehot.astype(BF16), tri_ref[...]) + run_ref[:, 0:1]
    rank_ref[0:1, :] = jnp.sum(jnp.where(sel1, before, 0.0), axis=0, keepdims=True).astype(I32)
    rank_ref[1:2, :] = jnp.sum(jnp.where(sel2, before, 0.0), axis=0, keepdims=True).astype(I32)
    run_ref[...] += jnp.sum(onehot, axis=1, keepdims=True)
    cnt_ref[...] = run_ref[...].astype(I32)


def _router(x2d, o2d, w_out, g, w_router):
    t = x2d.shape[0]
    tm = TM_ROUTER
    tri = (jnp.arange(tm)[:, None] < jnp.arange(tm)[None, :]).astype(BF16)
    two = lambda dt: jax.ShapeDtypeStruct((2, t), dt)
    return pl.pallas_call(
        _router_body,
        grid=(t // tm,),
        in_specs=[
            pl.BlockSpec((tm, D_MODEL), lambda i: (i, 0)),
            pl.BlockSpec((tm, o2d.shape[1]), lambda i: (i, 0)),
            pl.BlockSpec(w_out.shape, lambda i: (0, 0)),
            pl.BlockSpec((1, D_MODEL), lambda i: (0, 0)),
            pl.BlockSpec((N_EXPERTS, D_MODEL), lambda i: (0, 0)),
            pl.BlockSpec((tm, tm), lambda i: (0, 0)),
        ],
        out_specs=[
            pl.BlockSpec((tm, D_MODEL), lambda i: (i, 0)),
            pl.BlockSpec((tm, D_MODEL // 2), lambda i: (i, 0)),
            pl.BlockSpec((2, tm), lambda i: (0, i)),
            pl.BlockSpec((2, tm), lambda i: (0, i)),
            pl.BlockSpec((2, tm), lambda i: (0, i)),
            pl.BlockSpec((N_EXPERTS, LANES), lambda i: (0, 0)),
        ],
        out_shape=[
            jax.ShapeDtypeStruct((t, D_MODEL), F32),
            jax.ShapeDtypeStruct((t, D_MODEL // 2), U32),
            two(I32), two(I32), two(F32),
            jax.ShapeDtypeStruct((N_EXPERTS, LANES), I32),
        ],
        scratch_shapes=[pltpu.VMEM((N_EXPERTS, LANES), F32)],
        compiler_params=_cparams(("arbitrary",), 48),
        name="moe_router",
    )(x2d, o2d, w_out, g, w_router.astype(F32).T, tri)


def _dispatch_body(d1_ref, d2_ref, hp_ref, init_ref, xs_ref, sem, *, rows):
    del init_ref

    def issue(k, c):
        r0 = pl.multiple_of(k * SUBLANES, SUBLANES)
        for u in range(SUBLANES):
            src = hp_ref.at[pl.ds(r0 + u, 1)]
            pltpu.make_async_copy(src, xs_ref.at[pl.ds(d1_ref[0, 0, r0 + u], 1)], sem.at[0]).start()
            pltpu.make_async_copy(src, xs_ref.at[pl.ds(d2_ref[0, 0, r0 + u], 1)], sem.at[1]).start()
        return c
    lax.fori_loop(0, rows // SUBLANES, issue, 0)
    pltpu.make_async_copy(hp_ref, xs_ref.at[pl.ds(0, rows)], sem.at[0]).wait()
    pltpu.make_async_copy(hp_ref, xs_ref.at[pl.ds(0, rows)], sem.at[1]).wait()


def _dispatch(hp, dest, n_rows):
    t, width = hp.shape
    rows = TM_DISPATCH
    idx_spec = pl.BlockSpec((1, 1, rows), lambda i: (i, 0, 0), memory_space=pltpu.SMEM)
    return pl.pallas_call(
        functools.partial(_dispatch_body, rows=rows),
        grid=(t // rows,),
        in_specs=[idx_spec, idx_spec, pl.BlockSpec((rows, width), lambda i: (i, 0)),
                  pl.BlockSpec(memory_space=pl.ANY)],
        out_specs=pl.BlockSpec(memory_space=pl.ANY),
        out_shape=jax.ShapeDtypeStruct((n_rows, width), hp.dtype),
        input_output_aliases={3: 0},
        scratch_shapes=[pltpu.SemaphoreType.DMA((2,))],
        compiler_params=_cparams(("arbitrary",), 32),
        name="moe_dispatch",
    )(dest[0].reshape(t // rows, 1, rows), dest[1].reshape(t // rows, 1, rows), hp,
      jnp.zeros((n_rows, width), hp.dtype))


def _expert_body(te_ref, nu_ref, xs_ref, wg_ref, wu_ref, wd_ref, o_ref, xb_ref, acc_ref):
    i = pl.program_id(0)
    j = pl.program_id(1)
    half = D_MODEL // 2

    @pl.when(j == 0)
    def _():
        w = xs_ref[...]
        xb_ref[:, :half] = lax.bitcast_convert_type(w << 16, F32).astype(BF16)
        xb_ref[:, half:] = lax.bitcast_convert_type(w & jnp.uint32(0xFFFF0000), F32).astype(BF16)
        acc_ref[...] = jnp.zeros_like(acc_ref)

    @pl.when(i < nu_ref[0])
    def _():
        xb = xb_ref[...]
        a = (jax.nn.silu(_dot(xb, wg_ref[...])) * _dot(xb, wu_ref[...])).astype(BF16)
        acc_ref[...] += _dot(a, wd_ref[...])

    @pl.when(j == pl.num_programs(1) - 1)
    def _():
        o_ref[...] = acc_ref[...]


def _experts(xs, tile_expert, n_used, wg, wu, wd):
    p = xs.shape[0]
    tm, tf = TM_MOE, TF_MOE
    grid_spec = pltpu.PrefetchScalarGridSpec(
        num_scalar_prefetch=2,
        grid=(p // tm, D_FF_EXPERT // tf),
        in_specs=[
            pl.BlockSpec((tm, D_MODEL // 2), lambda i, j, te, nu: (i, 0)),
            pl.BlockSpec((None, D_MODEL, tf), lambda i, j, te, nu: (te[i], 0, j)),
            pl.BlockSpec((None, D_MODEL, tf), lambda i, j, te, nu: (te[i], 0, j)),
            pl.BlockSpec((None, tf, D_MODEL), lambda i, j, te, nu: (te[i], j, 0)),
        ],
        out_specs=pl.BlockSpec((tm, D_MODEL), lambda i, j, te, nu: (i, 0)),
        scratch_shapes=[pltpu.VMEM((tm, D_MODEL), BF16), pltpu.VMEM((tm, D_MODEL), F32)],
    )
    return pl.pallas_call(
        _expert_body,
        grid_spec=grid_spec,
        out_shape=jax.ShapeDtypeStruct((p, D_MODEL), F32),
        compiler_params=_cparams(("parallel", "arbitrary"), 56),
        name="moe_experts",
    )(tile_expert, n_used, xs, wg, wu, wd)


def _combine_body(d1c_ref, d2c_ref, d1n_ref, d2n_ref, x_ref, g1_ref, g2_ref, ys_ref, oa_ref, ob_ref,
                  y1_ref, y2_ref, sem, *, rows, na_blocks):
    i = pl.program_id(0)
    slot = i % 2

    def gather(d1_ref, d2_ref, s):
        def issue(k, c):
            r0 = pl.multiple_of(k * SUBLANES, SUBLANES)
            for u in range(SUBLANES):
                pltpu.make_async_copy(ys_ref.at[pl.ds(d1_ref[0, 0, r0 + u], 1)], y1_ref.at[s, pl.ds(r0 + u, 1)],
                                      sem.at[0, s]).start()
                pltpu.make_async_copy(ys_ref.at[pl.ds(d2_ref[0, 0, r0 + u], 1)], y2_ref.at[s, pl.ds(r0 + u, 1)],
                                      sem.at[1, s]).start()
            return c
        lax.fori_loop(0, rows // SUBLANES, issue, 0)

    @pl.when(i == 0)
    def _():
        gather(d1c_ref, d2c_ref, 0)

    @pl.when(i + 1 < pl.num_programs(0))
    def _():
        gather(d1n_ref, d2n_ref, 1 - slot)

    pltpu.make_async_copy(ys_ref.at[pl.ds(0, rows)], y1_ref.at[slot], sem.at[0, slot]).wait()
    pltpu.make_async_copy(ys_ref.at[pl.ds(0, rows)], y2_ref.at[slot], sem.at[1, slot]).wait()
    val = x_ref[...] + g1_ref[...] * y1_ref[slot] + g2_ref[...] * y2_ref[slot]

    @pl.when(i < na_blocks)
    def _():
        oa_ref[...] = val

    @pl.when(i >= na_blocks)
    def _():
        ob_ref[...] = val


def _combine(x2d, ys, dest, gates, t_a):
    t = x2d.shape[0]
    rows = TM_COMBINE
    n = t // rows
    na = t_a // rows
    cur = pl.BlockSpec((1, 1, rows), lambda i: (i, 0, 0), memory_space=pltpu.SMEM)
    nxt = pl.BlockSpec((1, 1, rows), lambda i: (jnp.minimum(i + 1, n - 1), 0, 0), memory_space=pltpu.SMEM)
    gate_spec = pl.BlockSpec((rows, 1), lambda i: (i, 0))
    row_spec = pl.BlockSpec((rows, D_MODEL), lambda i: (i, 0))
    d1 = dest[0].reshape(n, 1, rows)
    d2 = dest[1].reshape(n, 1, rows)
    return pl.pallas_call(
        functools.partial(_combine_body, rows=rows, na_blocks=na),
        grid=(n,),
        in_specs=[cur, cur, nxt, nxt, row_spec, gate_spec, gate_spec, pl.BlockSpec(memory_space=pl.ANY)],
        out_specs=[pl.BlockSpec((rows, D_MODEL), lambda i: (jnp.minimum(i, na - 1), 0)),
                   pl.BlockSpec((rows, D_MODEL), lambda i: (jnp.maximum(i - na, 0), 0))],
        out_shape=[jax.ShapeDtypeStruct((t_a, D_MODEL), F32), jax.ShapeDtypeStruct((t - t_a, D_MODEL), F32)],
        scratch_shapes=[pltpu.VMEM((2, rows, D_MODEL), F32), pltpu.VMEM((2, rows, D_MODEL), F32),
                        pltpu.SemaphoreType.DMA((2, 2))],
        compiler_params=_cparams(("arbitrary",), 48),
        name="moe_combine",
    )(d1, d2, d1, d2, x2d, gates[0][:, None], gates[1][:, None], ys)


def _moe(x2d, o2d, w_out, g, w_router, wg, wu, wd, t_a):
    t = x2d.shape[0]
    tm = TM_MOE
    x2d, hp, idx, rank, gates, cnt = _router(x2d, o2d, w_out, g, w_router)
    counts = cnt[:, 0]
    tiles = (counts + tm - 1) // tm
    tile_end = jnp.cumsum(tiles)
    row_start = (tile_end - tiles) * tm
    n_tiles = (2 * t) // tm + N_EXPERTS
    eid = jnp.arange(N_EXPERTS, dtype=I32)[:, None, None]
    dest = jnp.sum(jnp.where(idx[None] == eid, row_start[:, None, None], 0), axis=0) + rank
    tile_expert = jnp.minimum(jnp.searchsorted(tile_end, jnp.arange(n_tiles), side="right"), N_EXPERTS - 1)
    xs = _dispatch(hp, dest, n_tiles * tm)
    ys = _experts(xs, tile_expert.astype(I32), tile_end[-1:].astype(I32), wg, wu, wd)
    return _combine(x2d, ys, dest, gates, t_a)


def _even_layer(xa, xb, norm_mix, norm_ffn, w_in, w_out, ln_g, ln_b, sgu_w, sgu_b, hy_conv,
                f_w1, f_b1, f_w2, f_b2, f_w3, f_b3, f_freq, hy_skip, wg, wu, wd):
    seq = xa.shape[1]
    bsz = xa.shape[0] + xb.shape[0]
    nh = seq // LANES
    sgu_bb = jnp.broadcast_to(sgu_b.astype(F32)[:, :, None], (A_GROUPS, CHUNK, LANES))
    w_in = w_in.astype(BF16)
    ya, hbt = _even_in(xa, xb, norm_mix[None, :], w_in[:, :2 * A_WIDTH], w_in[:, 2 * A_WIDTH:].T,
                       ln_g[None, :], ln_b[None, :], sgu_w.astype(BF16), sgu_bb)
    hbt = hbt.reshape(bsz, -1, nh, LANES)

    tb = _dft_tables(seq)
    kt = _hyena_kernels(seq, f_w1, f_b1, f_w2, f_b2, f_w3, f_b3, f_freq)
    kf = _filter_spectrum(kt.reshape(HYENA_ORDER, B_WIDTH, 2 * nh, LANES), tb)

    taps = hy_conv.astype(F32)
    skip = hy_skip.astype(F32)
    z1 = _hyena_conv(hbt, 0, hbt, B_WIDTH, taps, skip[0], kf[0], tb, conv_z=True)
    ybt = _hyena_conv(z1, 0, hbt, 2 * B_WIDTH, taps, skip[1], kf[1], tb, conv_z=False)

    w_out = w_out.astype(BF16)
    x2d = _mix_ffn(xa, xb, ya, ybt.reshape(bsz, B_WIDTH, seq), w_out[:A_WIDTH], w_out[A_WIDTH:],
                   norm_ffn[None, :], wg.astype(BF16), wu.astype(BF16), wd.astype(BF16))
    return x2d.reshape(bsz, seq, D_MODEL)


def _odd_layer(x, n_a, norm_mix, norm_ffn, w_qkv, q_g, k_g, sink, w_out, rel_bias, w_router, wg, wu, wd):
    bsz, seq, _ = x.shape
    q, k2, v2 = _qkv_proj(x, norm_mix[None, :], w_qkv, q_g, k_g)
    o = _attention(q, k2, v2, sink, rel_bias)
    ya, yb = _moe(x.reshape(bsz * seq, D_MODEL), o.reshape(bsz * seq, -1), w_out.astype(BF16), norm_ffn[None, :],
                  w_router, wg.astype(BF16), wu.astype(BF16), wd.astype(BF16), n_a * seq)
    return ya.reshape(n_a, seq, D_MODEL), yb.reshape(bsz - n_a, seq, D_MODEL)


def kernel(x_prompt, x_sample, norm_mix, norm_ffn, ev_w_in, ev_w_out, sgu_ln_g, sgu_ln_b, sgu_w, sgu_b,
           hy_conv, hy_f_w1, hy_f_b1, hy_f_w2, hy_f_b2, hy_f_w3, hy_f_b3, hy_f_freq, hy_skip,
           ffn_w_gate, ffn_w_up, ffn_w_down, at_w_qkv, at_q_norm, at_k_norm, at_sink, at_w_out,
           rel_bias, moe_router, moe_w_gate, moe_w_up, moe_w_down):
    even_p = (ev_w_in, ev_w_out, sgu_ln_g, sgu_ln_b, sgu_w, sgu_b, hy_conv, hy_f_w1, hy_f_b1, hy_f_w2,
              hy_f_b2, hy_f_w3, hy_f_b3, hy_f_freq, hy_skip, ffn_w_gate, ffn_w_up, ffn_w_down)
    odd_p = (at_w_qkv, at_q_norm, at_k_norm, at_sink, at_w_out, moe_router, moe_w_gate, moe_w_up, moe_w_down)
    assert x_prompt.shape[1:] == x_sample.shape[1:]
    n_a = x_prompt.shape[0]
    depth = norm_mix.shape[0]
    assert depth % 2 == 0, "layers come in (even, odd) pairs"
    xa, xb = x_prompt, x_sample
    for i in range(0, depth, 2):
        j = i // 2
        x = _even_layer(xa, xb, norm_mix[i], norm_ffn[i], *[p[j] for p in even_p])
        xa, xb = _odd_layer(x, n_a, norm_mix[i + 1], norm_ffn[i + 1], *[p[j] for p in odd_p[:5]], rel_bias,
                            *[p[j] for p in odd_p[5:]])
    return (xa, xb)
```

```python
import functools
import math

import jax
import jax.numpy as jnp
from jax import lax
from jax.experimental import pallas as pl
from jax.experimental.pallas import tpu as pltpu

F32 = jnp.float32
BF16 = jnp.bfloat16
U32 = jnp.uint32
I32 = jnp.int32

D_MODEL = 1024
A_GROUPS = 4
A_WIDTH = D_MODEL // 2
CHUNK = 128
B_WIDTH = D_MODEL // 2
HYENA_ORDER = 2
FILTER_DIRS = 2
FILTER_BANDS = 16
FILTER_EMB = 1 + 2 * FILTER_BANDS
FILTER_HIDDEN = 64
DECAY_TARGET = 1e-2
FAST_DECAY_PCT = 0.3
SLOW_DECAY_PCT = 1.5
HEAD_DIM = 64
N_HEADS = D_MODEL // HEAD_DIM
N_KV_HEADS = N_HEADS // 4
GQA_GROUP = N_HEADS // N_KV_HEADS
WINDOW = 128
ATT_BLOCK = 128
REL_BUCKETS = 32
REL_MAX_DIST = 128
NEG_INF = -1e30
D_FF = 2816
N_EXPERTS = 8
D_FF_EXPERT = 3584
EPS = 1e-6

LANES = 128
SUBLANES = 8
MIB = 1024 * 1024

TM_PROJ = 512
TM_FFN = 512
FF_CHUNK = 512
TM_MOE = 512
TF_MOE = D_FF_EXPERT // 2
TM_ROUTER = 512
TM_COMBINE = 256
TM_DISPATCH = 512
HY_CB = 32
HY_PAIRS = 4

NT_DIMS = (((1,), (1,)), ((), ()))
TN_DIMS = (((0,), (0,)), ((), ()))


def _cparams(sem, vmem_mib):
    return pltpu.CompilerParams(dimension_semantics=sem, vmem_limit_bytes=vmem_mib * MIB)


def _rms_scale(x):
    return x * lax.rsqrt(jnp.mean(x * x, axis=-1, keepdims=True) + EPS)


def _dot(a, b):
    return jnp.dot(a, b, preferred_element_type=F32)


def _group_specs(xa, xb, tm):
    na, seq = xa.shape[0], xa.shape[1]
    last = seq // tm - 1
    width = xa.shape[2]
    spec_a = pl.BlockSpec((None, tm, width), lambda b, i: (jnp.minimum(b, na - 1), jnp.where(b < na, i, last), 0))
    spec_b = pl.BlockSpec((None, tm, width), lambda b, i: (jnp.maximum(b - na, 0), jnp.where(b < na, 0, i), 0))
    return spec_a, spec_b


def _group_tile(xa_ref, xb_ref, na):
    return jnp.where(pl.program_id(0) < na, xa_ref[...], xb_ref[...])


def _even_in_body(xa_ref, xb_ref, g_ref, w_ref, wht_ref, lng_ref, lnb_ref, sw_ref, sb_ref, ya_ref, hbt_ref, *, na):
    x = _group_tile(xa_ref, xb_ref, na)
    h = (_rms_scale(x) * g_ref[...]).astype(BF16)
    u = jax.nn.gelu(_dot(h, w_ref[:, 0:A_WIDTH]))
    v = jax.nn.gelu(_dot(h, w_ref[:, A_WIDTH:2 * A_WIDTH]))
    hbt_ref[...] = lax.dot_general(wht_ref[...], h, NT_DIMS, preferred_element_type=F32)
    tm = x.shape[0]
    for gi in range(A_GROUPS):
        cs = slice(gi * LANES, (gi + 1) * LANES)
        vg = v[:, cs]
        xc = vg - jnp.mean(vg, axis=-1, keepdims=True)
        var = jnp.mean(xc * xc, axis=-1, keepdims=True)
        vn = (xc * lax.rsqrt(var + EPS) * lng_ref[:, cs] + lnb_ref[:, cs]).astype(BF16)
        for c in range(tm // CHUNK):
            rs = slice(c * CHUNK, (c + 1) * CHUNK)
            mixed = _dot(sw_ref[gi], vn[rs]) + sb_ref[gi]
            ya_ref[rs, cs] = (u[rs, cs] * mixed).astype(ya_ref.dtype)


def _even_in(xa, xb, g, w_uv, w_hb_t, ln_g, ln_b, sgu_w, sgu_b):
    na, seq, _ = xa.shape
    bsz = na + xb.shape[0]
    tm = TM_PROJ
    n_hb = w_hb_t.shape[0]
    const = lambda *shape: pl.BlockSpec(shape, lambda b, i: (0,) * len(shape))
    return pl.pallas_call(
        functools.partial(_even_in_body, na=na),
        grid=(bsz, seq // tm),
        in_specs=[
            *_group_specs(xa, xb, tm),
            const(1, D_MODEL),
            const(D_MODEL, 2 * A_WIDTH),
            const(n_hb, D_MODEL),
            const(1, A_WIDTH),
            const(1, A_WIDTH),
            const(A_GROUPS, CHUNK, CHUNK),
            const(A_GROUPS, CHUNK, LANES),
        ],
        out_specs=[
            pl.BlockSpec((None, tm, A_WIDTH), lambda b, i: (b, i, 0)),
            pl.BlockSpec((None, n_hb, tm), lambda b, i: (b, 0, i)),
        ],
        out_shape=[
            jax.ShapeDtypeStruct((bsz, seq, A_WIDTH), BF16),
            jax.ShapeDtypeStruct((bsz, n_hb, seq), F32),
        ],
        compiler_params=_cparams(("parallel", "parallel"), 48),
        name="even_in",
    )(xa, xb, g, w_uv, w_hb_t, ln_g, ln_b, sgu_w, sgu_b)


def _filter_body(ft_ref, t_ref, w1_ref, b1_ref, w2_ref, b2_ref, w3t_ref, b3_ref, fr0_ref, fr1_ref, absd_ref,
                 o_ref, *, seq):
    hp = lax.Precision.HIGHEST
    tl = ft_ref.shape[0]
    h = jnp.sin(fr0_ref[...] * (jnp.dot(ft_ref[...], w1_ref[...], precision=hp, preferred_element_type=F32)
                                + b1_ref[...]))
    h = jnp.sin(fr1_ref[...] * (jnp.dot(h, w2_ref[...], precision=hp, preferred_element_type=F32) + b2_ref[...]))
    out = lax.dot_general(w3t_ref[...], h, NT_DIMS, precision=hp, preferred_element_type=F32) + b3_ref[...]
    decay = jnp.exp(-absd_ref[...] * t_ref[...])
    pos = pl.program_id(0) * tl + lax.broadcasted_iota(I32, (1, tl), 1)
    for o in range(HYENA_ORDER):
        o_ref[o] = jnp.where(pos == seq, 0.0, out[o * B_WIDTH:(o + 1) * B_WIDTH] * decay)


def _hyena_kernels(seq, w1, b1, w2, b2, w3, b3, freq):
    t = jnp.linspace(0.0, 1.0, seq, dtype=F32)[:, None]
    w = 2.0 * math.pi * jnp.arange(seq, dtype=F32)[:, None] / seq
    bands = jnp.linspace(1e-4, FILTER_BANDS - 1, FILTER_BANDS, dtype=F32)[None, :]
    feats = jnp.concatenate([t, jnp.cos(bands * w), jnp.sin(bands * w)], axis=-1)
    feats = jnp.concatenate([feats, feats[:1], feats[:0:-1]], axis=0)
    t_row = feats[:, 0][None, :]
    feats = jnp.pad(feats, ((0, 0), (0, LANES - FILTER_EMB)))
    hpad = LANES - FILTER_HIDDEN
    w1p = jnp.pad(w1.astype(F32), ((0, LANES - FILTER_EMB), (0, hpad)))
    w2p = jnp.pad(w2.astype(F32), ((0, hpad), (0, hpad)))
    n_dir = HYENA_ORDER * B_WIDTH
    w3t = jnp.pad(w3.astype(F32), ((0, hpad), (0, 0))).T.reshape(FILTER_DIRS, n_dir, LANES)
    b3c = b3.astype(F32).reshape(FILTER_DIRS, n_dir, 1)
    b1p = jnp.pad(b1.astype(F32), (0, hpad))[None, :]
    b2p = jnp.pad(b2.astype(F32), (0, hpad))[None, :]
    fr = jnp.pad(freq.astype(F32), ((0, 0), (0, hpad)))
    deltas = jnp.linspace(math.log(DECAY_TARGET) / SLOW_DECAY_PCT,
                          math.log(DECAY_TARGET) / FAST_DECAY_PCT, B_WIDTH, dtype=F32)
    absd = jnp.abs(deltas)[:, None]
    tl = min(seq, 1024)
    nhalf = seq // tl
    const = lambda *shape: pl.BlockSpec(shape, lambda i: (0,) * len(shape))
    return pl.pallas_call(
        functools.partial(_filter_body, seq=seq),
        grid=(2 * nhalf,),
        in_specs=[
            pl.BlockSpec((tl, LANES), lambda i: (i, 0)),
            pl.BlockSpec((1, tl), lambda i: (0, i)),
            const(LANES, LANES), const(1, LANES), const(LANES, LANES), const(1, LANES),
            pl.BlockSpec((None, n_dir, LANES), lambda i: (i // nhalf, 0, 0)),
            pl.BlockSpec((None, n_dir, 1), lambda i: (i // nhalf, 0, 0)),
            const(1, LANES), const(1, LANES), const(B_WIDTH, 1),
        ],
        out_specs=pl.BlockSpec((HYENA_ORDER, B_WIDTH, tl), lambda i: (0, 0, i)),
        out_shape=jax.ShapeDtypeStruct((HYENA_ORDER, B_WIDTH, 2 * seq), F32),
        compiler_params=_cparams(("parallel",), 48),
        name="hyena_filter",
    )(feats, t_row, w1p, b1p, w2p, b2p, w3t, b3c, fr[0:1], fr[1:2], absd)


def _dft_tables(seq):
    n = 2 * seq
    nb = n // LANES
    k1 = jnp.arange(nb, dtype=I32)[:, None]
    ang1 = (2.0 * math.pi / nb) * ((k1 * jnp.arange(nb, dtype=I32)[None, :]) % nb).astype(F32)
    f1_full = jnp.concatenate([jnp.cos(ang1), -jnp.sin(ang1)], axis=0)
    f1_half = f1_full[:, : nb // 2]
    f4_half = f1_half.T
    angt = (2.0 * math.pi / n) * ((k1 * jnp.arange(LANES, dtype=I32)[None, :]) % n).astype(F32)
    twr, twi = jnp.cos(angt), -jnp.sin(angt)
    a = jnp.arange(LANES, dtype=I32)
    ang2 = (2.0 * math.pi / LANES) * ((a[:, None] * a[None, :]) % LANES).astype(F32)
    cr, ci = jnp.cos(ang2), -jnp.sin(ang2)
    m2 = jnp.concatenate([jnp.concatenate([cr, ci], axis=1), jnp.concatenate([-ci, cr], axis=1)], axis=0)
    m3 = jnp.concatenate([jnp.concatenate([cr, -ci], axis=1), jnp.concatenate([ci, cr], axis=1)], axis=0)
    return dict(f1_full=f1_full.astype(BF16), f1_half=f1_half.astype(BF16), f4_half=f4_half.astype(BF16),
                twr=twr, twi=twi, twr_h=twr.astype(BF16), twi_h=twi.astype(BF16),
                m2=m2.astype(BF16), m3=m3.astype(BF16))


def _fwd_spectrum(pairs, f1_ref, twr_ref, twi_ref, m2_ref):
    nb = twr_ref.shape[0]
    a_all = [_dot(f1_ref[...], jnp.concatenate(xs, axis=1).astype(BF16)) for xs in pairs]
    twr, twi = twr_ref[...], twi_ref[...]
    out = []
    for a in a_all:
        a = a.astype(twr.dtype)
        lhs = []
        for d in range(2):
            ar = a[:nb, d * LANES:(d + 1) * LANES]
            ai = a[nb:, d * LANES:(d + 1) * LANES]
            lhs.append(jnp.concatenate([ar * twr - ai * twi, ar * twi + ai * twr], axis=1))
        out.append(_dot(jnp.concatenate(lhs, axis=0).astype(BF16), m2_ref[...]))
    return out


def _spec_body(k_ref, f1_ref, twr_ref, twi_ref, m2_ref, o_ref):
    nb = twr_ref.shape[0]
    inv_n = 1.0 / (nb * LANES)

    def group(it, carry):
        c0 = 2 * HY_PAIRS * it
        pairs = [[k_ref[c0 + 2 * g], k_ref[c0 + 2 * g + 1]] for g in range(HY_PAIRS)]
        for g, z in enumerate(_fwd_spectrum(pairs, f1_ref, twr_ref, twi_ref, m2_ref)):
            o_ref[c0 + 2 * g] = (z[:nb] * inv_n).astype(o_ref.dtype)
            o_ref[c0 + 2 * g + 1] = (z[nb:] * inv_n).astype(o_ref.dtype)
        return carry
    lax.fori_loop(0, k_ref.shape[0] // (2 * HY_PAIRS), group, 0)


def _filter_spectrum(kt, tb):
    orders, c, nb, _ = kt.shape
    cb = HY_CB
    const = lambda *shape: pl.BlockSpec(shape, lambda o, j: (0,) * len(shape))
    return pl.pallas_call(
        _spec_body,
        grid=(orders, c // cb),
        in_specs=[
            pl.BlockSpec((None, cb, nb, LANES), lambda o, j: (o, j, 0, 0)),
            const(2 * nb, nb), const(nb, LANES), const(nb, LANES), const(2 * LANES, 2 * LANES),
        ],
        out_specs=pl.BlockSpec((None, cb, nb, 2 * LANES), lambda o, j: (o, j, 0, 0)),
        out_shape=jax.ShapeDtypeStruct((orders, c, nb, 2 * LANES), BF16),
        compiler_params=_cparams(("parallel", "parallel"), 48),
        name="hyena_filter_spectrum",
    )(kt, tb["f1_full"], tb["twr"], tb["twi"], tb["m2"])


def _shift_rows(x, down):
    rows = x.shape[0]
    idx = lax.broadcasted_iota(I32, x.shape, 0)
    if down:
        return jnp.where(idx == 0, 0.0, pltpu.roll(x, 1, axis=0))
    return jnp.where(idx == rows - 1, 0.0, pltpu.roll(x, rows - 1, axis=0))


def _short_conv(x, taps_ref, ch):
    lane = lax.broadcasted_iota(I32, x.shape, 1)
    prev = pltpu.roll(jnp.where(lane == LANES - 1, _shift_rows(x, True), x), 1, axis=1)
    nxt = pltpu.roll(jnp.where(lane == 0, _shift_rows(x, False), x), LANES - 1, axis=1)
    return taps_ref[0, ch] * prev + taps_ref[1, ch] * x + taps_ref[2, ch] * nxt


def _conv_body(taps_ref, skip_ref, z_ref, g_ref, kf_ref, f1_ref, f4_ref, twr_ref, twi_ref, m2_ref, m3_ref,
               o_ref, *, zch, gch, conv_z):
    nb = twr_ref.shape[0]
    cb = z_ref.shape[0]
    base = pl.program_id(0) * cb

    def group(it, carry):
        c0 = 2 * HY_PAIRS * it
        chans = [[c0 + 2 * g, c0 + 2 * g + 1] for g in range(HY_PAIRS)]
        xs = [[_short_conv(z_ref[c], taps_ref, zch + base + c) if conv_z else z_ref[c] for c in pr] for pr in chans]
        zs = _fwd_spectrum(xs, f1_ref, twr_ref, twi_ref, m2_ref)
        ccs = []
        for pr, z in zip(chans, zs):
            z = z.astype(kf_ref.dtype)
            ys = []
            for d, c in enumerate(pr):
                zr = z[d * nb:(d + 1) * nb, :LANES]
                zi = z[d * nb:(d + 1) * nb, LANES:]
                kf = kf_ref[c]
                kr, ki = kf[:, :LANES], kf[:, LANES:]
                ys.append(jnp.concatenate([zr * kr - zi * ki, zr * ki + zi * kr], axis=1))
            ccs.append(_dot(jnp.concatenate(ys, axis=0).astype(BF16), m3_ref[...]))
        twr, twi = twr_ref[...], twi_ref[...]
        for pr, x2, cc in zip(chans, xs, ccs):
            cc = cc.astype(twr.dtype)
            drs, dis = [], []
            for d in range(2):
                ccr = cc[d * nb:(d + 1) * nb, :LANES]
                cci = cc[d * nb:(d + 1) * nb, LANES:]
                drs.append(ccr * twr + cci * twi)
                dis.append(cci * twr - ccr * twi)
            rhs = jnp.concatenate([jnp.concatenate(drs, axis=1), jnp.concatenate(dis, axis=1)], axis=0)
            y = _dot(f4_ref[...], rhs.astype(BF16))
            for d, c in enumerate(pr):
                gate = _short_conv(g_ref[c], taps_ref, gch + base + c)
                o_ref[c] = gate * (y[:, d * LANES:(d + 1) * LANES] + skip_ref[base + c] * x2[d])
        return carry
    lax.fori_loop(0, cb // (2 * HY_PAIRS), group, 0)


def _hyena_conv(zsrc, zch, gsrc, gch, taps, skip, kf, order, tb, conv_z):
    bsz, _, nh, _ = zsrc.shape
    nb = 2 * nh
    cb = HY_CB
    smem = pl.BlockSpec(memory_space=pltpu.SMEM)
    const = lambda *shape: pl.BlockSpec(shape, lambda j, b: (0,) * len(shape))
    zblk, gblk = zch // cb, gch // cb
    return pl.pallas_call(
        functools.partial(_conv_body, zch=zch, gch=gch, conv_z=conv_z),
        grid=(B_WIDTH // cb, bsz),
        in_specs=[
            smem, smem,
            pl.BlockSpec((None, cb, nh, LANES), lambda j, b: (b, zblk + j, 0, 0)),
            pl.BlockSpec((None, cb, nh, LANES), lambda j, b: (b, gblk + j, 0, 0)),
            pl.BlockSpec((None, cb, nb, 2 * LANES), lambda j, b: (order, j, 0, 0)),
            const(2 * nb, nh), const(nh, 2 * nb), const(nb, LANES), const(nb, LANES),
            const(2 * LANES, 2 * LANES), const(2 * LANES, 2 * LANES),
        ],
        out_specs=pl.BlockSpec((None, cb, nh, LANES), lambda j, b: (b, j, 0, 0)),
        out_shape=jax.ShapeDtypeStruct((bsz, B_WIDTH, nh, LANES), F32),
        compiler_params=_cparams(("parallel", "parallel"), 48),
        name="hyena_conv",
    )(taps, skip, zsrc, gsrc, kf, tb["f1_half"], tb["f4_half"], tb["twr_h"], tb["twi_h"], tb["m2"], tb["m3"])


def _mix_ffn_body(xa_ref, xb_ref, ya_ref, ybt_ref, wa_ref, wb_ref, g_ref, wg_ref, wu_ref, wd_ref, o_ref, h_ref,
                  *, na_tiles):
    x = jnp.where(pl.program_id(0) < na_tiles, xa_ref[...], xb_ref[...])
    x = x + _dot(ya_ref[...], wa_ref[...])
    x = x + lax.dot_general(ybt_ref[...].astype(BF16), wb_ref[...], TN_DIMS, preferred_element_type=F32)
    h_ref[...] = (_rms_scale(x) * g_ref[...]).astype(BF16)
    o_ref[...] = x

    def add_chunk(cs):
        h = h_ref[...]
        a = (jax.nn.silu(_dot(h, wg_ref[:, cs])) * _dot(h, wu_ref[:, cs])).astype(BF16)
        o_ref[...] += _dot(a, wd_ref[cs, :])

    def chunk(c, carry):
        add_chunk(pl.ds(pl.multiple_of(c * FF_CHUNK, FF_CHUNK), FF_CHUNK))
        return carry
    n_full = D_FF // FF_CHUNK
    lax.fori_loop(0, n_full, chunk, 0)
    if D_FF % FF_CHUNK:
        add_chunk(slice(n_full * FF_CHUNK, D_FF))


def _mix_ffn(xa, xb, ya, ybt, wa, wb, g, wg, wu, wd):
    na, seq, _ = xa.shape
    bsz = na + xb.shape[0]
    tm = TM_FFN
    tps = seq // tm
    na_tiles = na * tps
    flat = lambda v: v.reshape(-1, v.shape[-1])
    resident = lambda w: pl.BlockSpec(w.shape, lambda i: (0, 0), pipeline_mode=pl.Buffered(1))
    return pl.pallas_call(
        functools.partial(_mix_ffn_body, na_tiles=na_tiles),
        grid=(bsz * tps,),
        in_specs=[
            pl.BlockSpec((tm, D_MODEL), lambda i: (jnp.minimum(i, na_tiles - 1), 0)),
            pl.BlockSpec((tm, D_MODEL), lambda i: (jnp.maximum(i - na_tiles, 0), 0)),
            pl.BlockSpec((tm, ya.shape[-1]), lambda i: (i, 0)),
            pl.BlockSpec((None, ybt.shape[1], tm), lambda i: (i // tps, 0, i % tps)),
            resident(wa), resident(wb), pl.BlockSpec((1, D_MODEL), lambda i: (0, 0)),
            resident(wg), resident(wu), resident(wd),
        ],
        out_specs=pl.BlockSpec((tm, D_MODEL), lambda i: (i, 0)),
        out_shape=jax.ShapeDtypeStruct((bsz * seq, D_MODEL), F32),
        scratch_shapes=[pltpu.VMEM((tm, D_MODEL), BF16)],
        compiler_params=_cparams(("parallel",), 56),
        name="mix_ffn",
    )(flat(xa), flat(xb), flat(ya), ybt, wa, wb, g, wg, wu, wd)


def _qkv_body(x_ref, g_ref, w_ref, qg_ref, kg_ref, q_ref, k_ref, v_ref):
    h = (_rms_scale(x_ref[...]) * g_ref[...]).astype(BF16)
    nq = N_HEADS * HEAD_DIM
    nk = N_KV_HEADS * LANES
    lo = lax.broadcasted_iota(I32, (1, LANES), 1) < HEAD_DIM
    qgain = qg_ref[...] * (HEAD_DIM ** -0.5)
    wide = 2 * LANES
    for c2 in range(nq // wide):
        xw = _dot(h, w_ref[:, c2 * wide:(c2 + 1) * wide])
        for half in range(2):
            c = 2 * c2 + half
            x = xw[:, half * LANES:(half + 1) * LANES]
            x2 = x * x
            s_lo = jnp.sum(jnp.where(lo, x2, 0.0), axis=-1, keepdims=True)
            s_hi = jnp.sum(jnp.where(lo, 0.0, x2), axis=-1, keepdims=True)
            r = jnp.where(lo, lax.rsqrt(s_lo / HEAD_DIM + EPS), lax.rsqrt(s_hi / HEAD_DIM + EPS))
            q_ref[:, c * LANES:(c + 1) * LANES] = (x * r * qgain).astype(q_ref.dtype)
    for c2 in range(nk // wide):
        xw = _dot(h, w_ref[:, nq + c2 * wide:nq + (c2 + 1) * wide])
        for half in range(2):
            c = 2 * c2 + half
            x = xw[:, half * LANES:(half + 1) * LANES]
            k_ref[:, c * LANES:(c + 1) * LANES] = (_rms_scale(x) * kg_ref[...]).astype(k_ref.dtype)
    v_ref[...] = _dot(h, w_ref[:, nq + nk:]).astype(v_ref.dtype)


def _qkv_proj(x, g, w_qkv, q_g, k_g):
    bsz, seq, _ = x.shape
    tm = TM_PROJ
    nq = N_HEADS * HEAD_DIM
    nkv = N_KV_HEADS * HEAD_DIM
    dup = lambda w: jnp.tile(w.reshape(D_MODEL, N_KV_HEADS, 1, HEAD_DIM), (1, 1, 2, 1)).reshape(D_MODEL, 2 * nkv)
    w = jnp.concatenate([w_qkv[:, :nq], dup(w_qkv[:, nq:nq + nkv]), dup(w_qkv[:, nq + nkv:])], axis=1).astype(BF16)
    two = lambda v: jnp.tile(v.astype(F32), 2)[None, :]
    row = lambda width: pl.BlockSpec((None, tm, width), lambda b, i: (b, i, 0))
    const = lambda *shape: pl.BlockSpec(shape, lambda b, i: (0,) * len(shape))
    return pl.pallas_call(
        _qkv_body,
        grid=(bsz, seq // tm),
        in_specs=[row(D_MODEL), const(1, D_MODEL), const(D_MODEL, w.shape[1]), const(1, LANES), const(1, LANES)],
        out_specs=[row(nq), row(2 * nkv), row(2 * nkv)],
        out_shape=[jax.ShapeDtypeStruct((bsz, seq, nq), BF16), jax.ShapeDtypeStruct((bsz, seq, 2 * nkv), BF16),
                   jax.ShapeDtypeStruct((bsz, seq, 2 * nkv), BF16)],
        compiler_params=_cparams(("parallel", "parallel"), 48),
        name="qkv_proj",
    )(x, g, w, two(q_g), two(k_g))


def _t5_bucket(rel):
    nbk = REL_BUCKETS // 2
    max_exact = nbk // 2
    ret = jnp.where(rel > 0, nbk, 0)
    n = jnp.abs(rel)
    large = max_exact + (jnp.log(jnp.maximum(n, 1).astype(F32) / max_exact)
                         / math.log(REL_MAX_DIST / max_exact) * (nbk - max_exact)).astype(I32)
    large = jnp.minimum(large, nbk - 1)
    return ret + jnp.where(n < max_exact, n, large)


def _attn_body(sink_ref, q_ref, kp_ref, ko_ref, kn_ref, vp_ref, vo_ref, vn_ref, bm_ref, o_ref):
    kb = 3 * ATT_BLOCK
    lo = lax.broadcasted_iota(I32, (1, LANES), 1) < HEAD_DIM
    first = lax.broadcasted_iota(I32, (2 * ATT_BLOCK, 1), 0) < ATT_BLOCK
    for hk in range(N_KV_HEADS):
        ks = slice(hk * LANES, (hk + 1) * LANES)
        kk = jnp.concatenate([kp_ref[:, ks], ko_ref[:, ks], kn_ref[:, ks]], axis=0)
        vv = jnp.concatenate([vp_ref[:, ks], vo_ref[:, ks], vn_ref[:, ks]], axis=0)
        zero = jnp.zeros_like(kk)
        kz = jnp.concatenate([jnp.where(lo, kk, zero), jnp.where(lo, zero, kk)], axis=0)
        vz = jnp.concatenate([jnp.where(lo, vv, zero), jnp.where(lo, zero, vv)], axis=0)
        c0 = 2 * hk
        ql = jnp.concatenate([q_ref[:, c0 * LANES:(c0 + 1) * LANES], q_ref[:, (c0 + 1) * LANES:(c0 + 2) * LANES]],
                             axis=0)
        s_all = lax.dot_general(ql, kz, NT_DIMS, preferred_element_type=F32)
        probs, invs = [], []
        for par in range(2):
            ha, hb = GQA_GROUP * hk + par, GQA_GROUP * hk + 2 + par
            s = s_all[:, par * kb:(par + 1) * kb] + jnp.concatenate([bm_ref[ha], bm_ref[hb]], axis=0)
            sk = jnp.where(first, sink_ref[ha], sink_ref[hb])
            m = jnp.maximum(jnp.max(s, axis=-1, keepdims=True), sk)
            pexp = jnp.exp(s - m)
            invs.append(1.0 / (jnp.sum(pexp, axis=-1, keepdims=True) + jnp.exp(sk - m)))
            probs.append(pexp.astype(BF16))
        acc = _dot(jnp.concatenate(probs, axis=1), vz) * jnp.where(lo, invs[0], invs[1])
        o_ref[:, c0 * LANES:(c0 + 1) * LANES] = acc[:ATT_BLOCK].astype(o_ref.dtype)
        o_ref[:, (c0 + 1) * LANES:(c0 + 2) * LANES] = acc[ATT_BLOCK:].astype(o_ref.dtype)


def _attention(q, k2, v2, sink, rel_bias):
    bsz, seq, _ = q.shape
    nblk = seq // ATT_BLOCK
    kb = 3 * ATT_BLOCK
    rel = jnp.arange(kb)[None, :] - ATT_BLOCK - jnp.arange(ATT_BLOCK)[:, None]
    bucket = _t5_bucket(rel)
    rb = rel_bias.astype(F32)
    bias = sum(jnp.where(bucket[None] == b, rb[b][:, None, None], 0.0) for b in range(REL_BUCKETS))
    bm = jnp.where((jnp.abs(rel) <= WINDOW)[None], bias, NEG_INF)
    assert nblk >= 2
    kcol = jnp.arange(kb)[None, None, :]
    bm3 = jnp.stack([jnp.where(kcol >= ATT_BLOCK, bm, NEG_INF), bm, jnp.where(kcol < 2 * ATT_BLOCK, bm, NEG_INF)])
    which = lambda i: jnp.where(i == 0, 0, jnp.where(i == nblk - 1, 2, 1))
    kvw = k2.shape[2]
    prev = lambda b, i: (b, jnp.maximum(i - 1, 0), 0)
    own = lambda b, i: (b, i, 0)
    nxt = lambda b, i: (b, jnp.minimum(i + 1, nblk - 1), 0)
    kv_spec = lambda fn: pl.BlockSpec((None, ATT_BLOCK, kvw), fn)
    return pl.pallas_call(
        _attn_body,
        grid=(bsz, nblk),
        in_specs=[
            pl.BlockSpec(memory_space=pltpu.SMEM),
            pl.BlockSpec((None, ATT_BLOCK, N_HEADS * HEAD_DIM), own),
            kv_spec(prev), kv_spec(own), kv_spec(nxt),
            kv_spec(prev), kv_spec(own), kv_spec(nxt),
            pl.BlockSpec((None, N_HEADS, ATT_BLOCK, kb), lambda b, i: (which(i), 0, 0, 0)),
        ],
        out_specs=pl.BlockSpec((None, ATT_BLOCK, N_HEADS * HEAD_DIM), own),
        out_shape=jax.ShapeDtypeStruct((bsz, seq, N_HEADS * HEAD_DIM), BF16),
        compiler_params=_cparams(("parallel", "parallel"), 48),
        name="window_attention",
    )(sink.astype(F32), q, k2, k2, k2, v2, v2, v2, bm3)


def _router_body(x_ref, o_ref, wo_ref, g_ref, wr_ref, tri_ref, xn_ref, hp_ref, idx_ref, rank_ref, gate_ref, cnt_ref,
                 run_ref):
    @pl.when(pl.program_id(0) == 0)
    def _():
        run_ref[...] = jnp.zeros_like(run_ref)

    x = x_ref[...] + _dot(o_ref[...], wo_ref[...])
    xn_ref[...] = x
    h = _rms_scale(x) * g_ref[...]
    half = D_MODEL // 2
    bits = lax.bitcast_convert_type(h.astype(BF16).astype(F32), U32)
    hp_ref[...] = (bits[:, half:] & jnp.uint32(0xFFFF0000)) | (bits[:, :half] >> 16)

    logits = lax.dot_general(wr_ref[...], h, NT_DIMS, precision=lax.Precision.HIGHEST,
                             preferred_element_type=F32)
    eid = lax.broadcasted_iota(I32, logits.shape, 0)
    m1 = jnp.max(logits, axis=0, keepdims=True)
    i1 = jnp.min(jnp.where(logits == m1, eid, N_EXPERTS), axis=0, keepdims=True)
    rest = jnp.where(eid == i1, -jnp.inf, logits)
    m2 = jnp.max(rest, axis=0, keepdims=True)
    i2 = jnp.min(jnp.where(rest == m2, eid, N_EXPERTS), axis=0, keepdims=True)
    e2 = jnp.exp(m2 - m1)
    gate_ref[0:1, :] = 1.0 / (1.0 + e2)
    gate_ref[1:2, :] = e2 / (1.0 + e2)
    idx_ref[0:1, :] = i1
    idx_ref[1:2, :] = i2

    sel1 = eid == i1
    sel2 = eid == i2
    onehot = jnp.where(sel1 | sel2, 1.0, 0.0)
    before = _dot(onehot.astype(BF16), tri_ref[...]) + run_ref[:, 0:1]
    rank_ref[0:1, :] = jnp.sum(jnp.where(sel1, before, 0.0), axis=0, keepdims=True).astype(I32)
    rank_ref[1:2, :] = jnp.sum(jnp.where(sel2, before, 0.0), axis=0, keepdims=True).astype(I32)
    run_ref[...] += jnp.sum(onehot, axis=1, keepdims=True)
    cnt_ref[...] = run_ref[...].astype(I32)


def _router(x2d, o2d, w_out, g, w_router):
    t = x2d.shape[0]
    tm = TM_ROUTER
    tri = (jnp.arange(tm)[:, None] < jnp.arange(tm)[None, :]).astype(BF16)
    two = lambda dt: jax.ShapeDtypeStruct((2, t), dt)
    return pl.pallas_call(
        _router_body,
        grid=(t // tm,),
        in_specs=[
            pl.BlockSpec((tm, D_MODEL), lambda i: (i, 0)),
            pl.BlockSpec((tm, o2d.shape[1]), lambda i: (i, 0)),
            pl.BlockSpec(w_out.shape, lambda i: (0, 0)),
            pl.BlockSpec((1, D_MODEL), lambda i: (0, 0)),
            pl.BlockSpec((N_EXPERTS, D_MODEL), lambda i: (0, 0)),
            pl.BlockSpec((tm, tm), lambda i: (0, 0)),
        ],
        out_specs=[
            pl.BlockSpec((tm, D_MODEL), lambda i: (i, 0)),
            pl.BlockSpec((tm, D_MODEL // 2), lambda i: (i, 0)),
            pl.BlockSpec((2, tm), lambda i: (0, i)),
            pl.BlockSpec((2, tm), lambda i: (0, i)),
            pl.BlockSpec((2, tm), lambda i: (0, i)),
            pl.BlockSpec((N_EXPERTS, LANES), lambda i: (0, 0)),
        ],
        out_shape=[
            jax.ShapeDtypeStruct((t, D_MODEL), F32),
            jax.ShapeDtypeStruct((t, D_MODEL // 2), U32),
            two(I32), two(I32), two(F32),
            jax.ShapeDtypeStruct((N_EXPERTS, LANES), I32),
        ],
        scratch_shapes=[pltpu.VMEM((N_EXPERTS, LANES), F32)],
        compiler_params=_cparams(("arbitrary",), 48),
        name="moe_router",
    )(x2d, o2d, w_out, g, w_router.astype(F32).T, tri)


def _dispatch_body(d1_ref, d2_ref, hp_ref, init_ref, xs_ref, sem, *, rows):
    del init_ref

    def issue(k, c):
        r0 = pl.multiple_of(k * SUBLANES, SUBLANES)
        for u in range(SUBLANES):
            src = hp_ref.at[pl.ds(r0 + u, 1)]
            pltpu.make_async_copy(src, xs_ref.at[pl.ds(d1_ref[0, 0, r0 + u], 1)], sem.at[0]).start()
            pltpu.make_async_copy(src, xs_ref.at[pl.ds(d2_ref[0, 0, r0 + u], 1)], sem.at[1]).start()
        return c
    lax.fori_loop(0, rows // SUBLANES, issue, 0)
    pltpu.make_async_copy(hp_ref, xs_ref.at[pl.ds(0, rows)], sem.at[0]).wait()
    pltpu.make_async_copy(hp_ref, xs_ref.at[pl.ds(0, rows)], sem.at[1]).wait()


def _dispatch(hp, dest, n_rows):
    t, width = hp.shape
    rows = TM_DISPATCH
    idx_spec = pl.BlockSpec((1, 1, rows), lambda i: (i, 0, 0), memory_space=pltpu.SMEM)
    return pl.pallas_call(
        functools.partial(_dispatch_body, rows=rows),
        grid=(t // rows,),
        in_specs=[idx_spec, idx_spec, pl.BlockSpec((rows, width), lambda i: (i, 0)),
                  pl.BlockSpec(memory_space=pl.ANY)],
        out_specs=pl.BlockSpec(memory_space=pl.ANY),
        out_shape=jax.ShapeDtypeStruct((n_rows, width), hp.dtype),
        input_output_aliases={3: 0},
        scratch_shapes=[pltpu.SemaphoreType.DMA((2,))],
        compiler_params=_cparams(("arbitrary",), 32),
        name="moe_dispatch",
    )(dest[0].reshape(t // rows, 1, rows), dest[1].reshape(t // rows, 1, rows), hp,
      jnp.zeros((n_rows, width), hp.dtype))


def _expert_body(te_ref, nu_ref, xs_ref, wg_ref, wu_ref, wd_ref, o_ref, xb_ref, acc_ref):
    i = pl.program_id(0)
    j = pl.program_id(1)
    half = D_MODEL // 2

    @pl.when(j == 0)
    def _():
        w = xs_ref[...]
        xb_ref[:, :half] = lax.bitcast_convert_type(w << 16, F32).astype(BF16)
        xb_ref[:, half:] = lax.bitcast_convert_type(w & jnp.uint32(0xFFFF0000), F32).astype(BF16)
        acc_ref[...] = jnp.zeros_like(acc_ref)

    @pl.when(i < nu_ref[0])
    def _():
        xb = xb_ref[...]
        a = (jax.nn.silu(_dot(xb, wg_ref[...])) * _dot(xb, wu_ref[...])).astype(BF16)
        acc_ref[...] += _dot(a, wd_ref[...])

    @pl.when(j == pl.num_programs(1) - 1)
    def _():
        o_ref[...] = acc_ref[...]


def _experts(xs, tile_expert, n_used, wg, wu, wd):
    p = xs.shape[0]
    tm, tf = TM_MOE, TF_MOE
    grid_spec = pltpu.PrefetchScalarGridSpec(
        num_scalar_prefetch=2,
        grid=(p // tm, D_FF_EXPERT // tf),
        in_specs=[
            pl.BlockSpec((tm, D_MODEL // 2), lambda i, j, te, nu: (i, 0)),
            pl.BlockSpec((None, D_MODEL, tf), lambda i, j, te, nu: (te[i], 0, j)),
            pl.BlockSpec((None, D_MODEL, tf), lambda i, j, te, nu: (te[i], 0, j)),
            pl.BlockSpec((None, tf, D_MODEL), lambda i, j, te, nu: (te[i], j, 0)),
        ],
        out_specs=pl.BlockSpec((tm, D_MODEL), lambda i, j, te, nu: (i, 0)),
        scratch_shapes=[pltpu.VMEM((tm, D_MODEL), BF16), pltpu.VMEM((tm, D_MODEL), F32)],
    )
    return pl.pallas_call(
        _expert_body,
        grid_spec=grid_spec,
        out_shape=jax.ShapeDtypeStruct((p, D_MODEL), F32),
        compiler_params=_cparams(("parallel", "arbitrary"), 56),
        name="moe_experts",
    )(tile_expert, n_used, xs, wg, wu, wd)


def _combine_body(d1c_ref, d2c_ref, d1n_ref, d2n_ref, x_ref, g1_ref, g2_ref, ys_ref, oa_ref, ob_ref,
                  y1_ref, y2_ref, sem, *, rows, na_blocks):
    i = pl.program_id(0)
    slot = i % 2

    def gather(d1_ref, d2_ref, s):
        def issue(k, c):
            r0 = pl.multiple_of(k * SUBLANES, SUBLANES)
            for u in range(SUBLANES):
                pltpu.make_async_copy(ys_ref.at[pl.ds(d1_ref[0, 0, r0 + u], 1)], y1_ref.at[s, pl.ds(r0 + u, 1)],
                                      sem.at[0, s]).start()
                pltpu.make_async_copy(ys_ref.at[pl.ds(d2_ref[0, 0, r0 + u], 1)], y2_ref.at[s, pl.ds(r0 + u, 1)],
                                      sem.at[1, s]).start()
            return c
        lax.fori_loop(0, rows // SUBLANES, issue, 0)

    @pl.when(i == 0)
    def _():
        gather(d1c_ref, d2c_ref, 0)

    @pl.when(i + 1 < pl.num_programs(0))
    def _():
        gather(d1n_ref, d2n_ref, 1 - slot)

    pltpu.make_async_copy(ys_ref.at[pl.ds(0, rows)], y1_ref.at[slot], sem.at[0, slot]).wait()
    pltpu.make_async_copy(ys_ref.at[pl.ds(0, rows)], y2_ref.at[slot], sem.at[1, slot]).wait()
    val = x_ref[...] + g1_ref[...] * y1_ref[slot] + g2_ref[...] * y2_ref[slot]

    @pl.when(i < na_blocks)
    def _():
        oa_ref[...] = val

    @pl.when(i >= na_blocks)
    def _():
        ob_ref[...] = val


def _combine(x2d, ys, dest, gates, t_a):
    t = x2d.shape[0]
    rows = TM_COMBINE
    n = t // rows
    na = t_a // rows
    cur = pl.BlockSpec((1, 1, rows), lambda i: (i, 0, 0), memory_space=pltpu.SMEM)
    nxt = pl.BlockSpec((1, 1, rows), lambda i: (jnp.minimum(i + 1, n - 1), 0, 0), memory_space=pltpu.SMEM)
    gate_spec = pl.BlockSpec((rows, 1), lambda i: (i, 0))
    row_spec = pl.BlockSpec((rows, D_MODEL), lambda i: (i, 0))
    d1 = dest[0].reshape(n, 1, rows)
    d2 = dest[1].reshape(n, 1, rows)
    return pl.pallas_call(
        functools.partial(_combine_body, rows=rows, na_blocks=na),
        grid=(n,),
        in_specs=[cur, cur, nxt, nxt, row_spec, gate_spec, gate_spec, pl.BlockSpec(memory_space=pl.ANY)],
        out_specs=[pl.BlockSpec((rows, D_MODEL), lambda i: (jnp.minimum(i, na - 1), 0)),
                   pl.BlockSpec((rows, D_MODEL), lambda i: (jnp.maximum(i - na, 0), 0))],
        out_shape=[jax.ShapeDtypeStruct((t_a, D_MODEL), F32), jax.ShapeDtypeStruct((t - t_a, D_MODEL), F32)],
        scratch_shapes=[pltpu.VMEM((2, rows, D_MODEL), F32), pltpu.VMEM((2, rows, D_MODEL), F32),
                        pltpu.SemaphoreType.DMA((2, 2))],
        compiler_params=_cparams(("arbitrary",), 48),
        name="moe_combine",
    )(d1, d2, d1, d2, x2d, gates[0][:, None], gates[1][:, None], ys)


def _moe(x2d, o2d, w_out, g, w_router, wg, wu, wd, t_a):
    t = x2d.shape[0]
    tm = TM_MOE
    x2d, hp, idx, rank, gates, cnt = _router(x2d, o2d, w_out, g, w_router)
    counts = cnt[:, 0]
    tiles = (counts + tm - 1) // tm
    tile_end = jnp.cumsum(tiles)
    row_start = (tile_end - tiles) * tm
    n_tiles = (2 * t) // tm + N_EXPERTS
    eid = jnp.arange(N_EXPERTS, dtype=I32)[:, None, None]
    dest = jnp.sum(jnp.where(idx[None] == eid, row_start[:, None, None], 0), axis=0) + rank
    tile_expert = jnp.minimum(jnp.searchsorted(tile_end, jnp.arange(n_tiles), side="right"), N_EXPERTS - 1)
    xs = _dispatch(hp, dest, n_tiles * tm)
    ys = _experts(xs, tile_expert.astype(I32), tile_end[-1:].astype(I32), wg, wu, wd)
    return _combine(x2d, ys, dest, gates, t_a)


def _even_layer(xa, xb, norm_mix, norm_ffn, w_in, w_out, ln_g, ln_b, sgu_w, sgu_b, hy_conv,
                f_w1, f_b1, f_w2, f_b2, f_w3, f_b3, f_freq, hy_skip, wg, wu, wd):
    seq = xa.shape[1]
    bsz = xa.shape[0] + xb.shape[0]
    nh = seq // LANES
    sgu_bb = jnp.broadcast_to(sgu_b.astype(F32)[:, :, None], (A_GROUPS, CHUNK, LANES))
    w_in = w_in.astype(BF16)
    ya, hbt = _even_in(xa, xb, norm_mix[None, :], w_in[:, :2 * A_WIDTH], w_in[:, 2 * A_WIDTH:].T,
                       ln_g[None, :], ln_b[None, :], sgu_w.astype(BF16), sgu_bb)
    hbt = hbt.reshape(bsz, -1, nh, LANES)

    tb = _dft_tables(seq)
    kt = _hyena_kernels(seq, f_w1, f_b1, f_w2, f_b2, f_w3, f_b3, f_freq)
    kf = _filter_spectrum(kt.reshape(HYENA_ORDER, B_WIDTH, 2 * nh, LANES), tb)

    taps = hy_conv.astype(F32)
    skip = hy_skip.astype(F32)
    z1 = _hyena_conv(hbt, 0, hbt, B_WIDTH, taps, skip[0], kf, 0, tb, conv_z=True)
    ybt = _hyena_conv(z1, 0, hbt, 2 * B_WIDTH, taps, skip[1], kf, 1, tb, conv_z=False)

    w_out = w_out.astype(BF16)
    x2d = _mix_ffn(xa, xb, ya, ybt.reshape(bsz, B_WIDTH, seq), w_out[:A_WIDTH], w_out[A_WIDTH:],
                   norm_ffn[None, :], wg.astype(BF16), wu.astype(BF16), wd.astype(BF16))
    return x2d.reshape(bsz, seq, D_MODEL)


def _odd_layer(x, n_a, norm_mix, norm_ffn, w_qkv, q_g, k_g, sink, w_out, rel_bias, w_router, wg, wu, wd):
    bsz, seq, _ = x.shape
    q, k2, v2 = _qkv_proj(x, norm_mix[None, :], w_qkv, q_g, k_g)
    o = _attention(q, k2, v2, sink, rel_bias)
    ya, yb = _moe(x.reshape(bsz * seq, D_MODEL), o.reshape(bsz * seq, -1), w_out.astype(BF16), norm_ffn[None, :],
                  w_router, wg.astype(BF16), wu.astype(BF16), wd.astype(BF16), n_a * seq)
    return ya.reshape(n_a, seq, D_MODEL), yb.reshape(bsz - n_a, seq, D_MODEL)


def kernel(x_prompt, x_sample, norm_mix, norm_ffn, ev_w_in, ev_w_out, sgu_ln_g, sgu_ln_b, sgu_w, sgu_b,
           hy_conv, hy_f_w1, hy_f_b1, hy_f_w2, hy_f_b2, hy_f_w3, hy_f_b3, hy_f_freq, hy_skip,
           ffn_w_gate, ffn_w_up, ffn_w_down, at_w_qkv, at_q_norm, at_k_norm, at_sink, at_w_out,
           rel_bias, moe_router, moe_w_gate, moe_w_up, moe_w_down):
    even_p = (ev_w_in, ev_w_out, sgu_ln_g, sgu_ln_b, sgu_w, sgu_b, hy_conv, hy_f_w1, hy_f_b1, hy_f_w2,
              hy_f_b2, hy_f_w3, hy_f_b3, hy_f_freq, hy_skip, ffn_w_gate, ffn_w_up, ffn_w_down)
    odd_p = (at_w_qkv, at_q_norm, at_k_norm, at_sink, at_w_out, moe_router, moe_w_gate, moe_w_up, moe_w_down)
    assert x_prompt.shape[1:] == x_sample.shape[1:]
    n_a = x_prompt.shape[0]
    depth = norm_mix.shape[0]
    assert depth % 2 == 0, "layers come in (even, odd) pairs"
    xa, xb = x_prompt, x_sample
    for i in range(0, depth, 2):
        j = i // 2
        x = _even_layer(xa, xb, norm_mix[i], norm_ffn[i], *[p[j] for p in even_p])
        xa, xb = _odd_layer(x, n_a, norm_mix[i + 1], norm_ffn[i + 1], *[p[j] for p in odd_p[:5]], rel_bias,
                            *[p[j] for p in odd_p[5:]])
    return (xa, xb)
```

```python
import functools
import math

import jax
import jax.numpy as jnp
from jax import lax
from jax.experimental import pallas as pl
from jax.experimental.pallas import tpu as pltpu

F32 = jnp.float32
BF16 = jnp.bfloat16
U32 = jnp.uint32
I32 = jnp.int32

D_MODEL = 1024
A_GROUPS = 4
A_WIDTH = D_MODEL // 2
CHUNK = 128
B_WIDTH = D_MODEL // 2
HYENA_ORDER = 2
FILTER_DIRS = 2
FILTER_BANDS = 16
FILTER_EMB = 1 + 2 * FILTER_BANDS
FILTER_HIDDEN = 64
DECAY_TARGET = 1e-2
FAST_DECAY_PCT = 0.3
SLOW_DECAY_PCT = 1.5
HEAD_DIM = 64
N_HEADS = D_MODEL // HEAD_DIM
N_KV_HEADS = N_HEADS // 4
GQA_GROUP = N_HEADS // N_KV_HEADS
WINDOW = 128
ATT_BLOCK = 128
REL_BUCKETS = 32
REL_MAX_DIST = 128
NEG_INF = -1e30
D_FF = 2816
N_EXPERTS = 8
D_FF_EXPERT = 3584
EPS = 1e-6

LANES = 128
SUBLANES = 8
MIB = 1024 * 1024

TM_PROJ = 512
TM_FFN = 512
FF_CHUNK = 512
TM_MOE = 512
TF_MOE = D_FF_EXPERT // 2
TM_ROUTER = 512
TM_COMBINE = 256
TM_DISPATCH = 512
HY_CB = 32
HY_PAIRS = 8

NT_DIMS = (((1,), (1,)), ((), ()))
TN_DIMS = (((0,), (0,)), ((), ()))


def _cparams(sem, vmem_mib):
    return pltpu.CompilerParams(dimension_semantics=sem, vmem_limit_bytes=vmem_mib * MIB)


def _rms_scale(x):
    return x * lax.rsqrt(jnp.mean(x * x, axis=-1, keepdims=True) + EPS)


def _dot(a, b):
    return jnp.dot(a, b, preferred_element_type=F32)


def _dot3(a, b, dims):
    def split(x):
        hi = x.astype(BF16)
        return hi, (x - hi.astype(F32)).astype(BF16)
    (ah, al), (bh, bl) = split(a), split(b)
    mm = lambda x, y: lax.dot_general(x, y, dims, preferred_element_type=F32)
    return mm(ah, bh) + (mm(al, bh) + mm(ah, bl))


def _group_specs(xa, xb, tm):
    na, seq = xa.shape[0], xa.shape[1]
    last = seq // tm - 1
    width = xa.shape[2]
    spec_a = pl.BlockSpec((None, tm, width), lambda b, i: (jnp.minimum(b, na - 1), jnp.where(b < na, i, last), 0))
    spec_b = pl.BlockSpec((None, tm, width), lambda b, i: (jnp.maximum(b - na, 0), jnp.where(b < na, 0, i), 0))
    return spec_a, spec_b


def _group_tile(xa_ref, xb_ref, na):
    return jnp.where(pl.program_id(0) < na, xa_ref[...], xb_ref[...])


def _even_in_body(xa_ref, xb_ref, g_ref, w_ref, wht_ref, lng_ref, lnb_ref, sw_ref, sb_ref, ya_ref, hbt_ref, *, na):
    x = _group_tile(xa_ref, xb_ref, na)
    h = (_rms_scale(x) * g_ref[...]).astype(BF16)
    u = jax.nn.gelu(_dot(h, w_ref[:, 0:A_WIDTH]))
    v = jax.nn.gelu(_dot(h, w_ref[:, A_WIDTH:2 * A_WIDTH]))
    hbt_ref[...] = lax.dot_general(wht_ref[...], h, NT_DIMS, preferred_element_type=F32)
    tm = x.shape[0]
    for gi in range(A_GROUPS):
        cs = slice(gi * LANES, (gi + 1) * LANES)
        vg = v[:, cs]
        xc = vg - jnp.mean(vg, axis=-1, keepdims=True)
        var = jnp.mean(xc * xc, axis=-1, keepdims=True)
        vn = (xc * lax.rsqrt(var + EPS) * lng_ref[:, cs] + lnb_ref[:, cs]).astype(BF16)
        for c in range(tm // CHUNK):
            rs = slice(c * CHUNK, (c + 1) * CHUNK)
            mixed = _dot(sw_ref[gi], vn[rs]) + sb_ref[gi]
            ya_ref[rs, cs] = (u[rs, cs] * mixed).astype(ya_ref.dtype)


def _even_in(xa, xb, g, w_uv, w_hb_t, ln_g, ln_b, sgu_w, sgu_b):
    na, seq, _ = xa.shape
    bsz = na + xb.shape[0]
    tm = TM_PROJ
    n_hb = w_hb_t.shape[0]
    const = lambda *shape: pl.BlockSpec(shape, lambda b, i: (0,) * len(shape))
    return pl.pallas_call(
        functools.partial(_even_in_body, na=na),
        grid=(bsz, seq // tm),
        in_specs=[
            *_group_specs(xa, xb, tm),
            const(1, D_MODEL),
            const(D_MODEL, 2 * A_WIDTH),
            const(n_hb, D_MODEL),
            const(1, A_WIDTH),
            const(1, A_WIDTH),
            const(A_GROUPS, CHUNK, CHUNK),
            const(A_GROUPS, CHUNK, LANES),
        ],
        out_specs=[
            pl.BlockSpec((None, tm, A_WIDTH), lambda b, i: (b, i, 0)),
            pl.BlockSpec((None, n_hb, tm), lambda b, i: (b, 0, i)),
        ],
        out_shape=[
            jax.ShapeDtypeStruct((bsz, seq, A_WIDTH), BF16),
            jax.ShapeDtypeStruct((bsz, n_hb, seq), F32),
        ],
        compiler_params=_cparams(("parallel", "parallel"), 48),
        name="even_in",
    )(xa, xb, g, w_uv, w_hb_t, ln_g, ln_b, sgu_w, sgu_b)


def _filter_body(ft_ref, t_ref, w1_ref, b1_ref, w2_ref, b2_ref, w3t_ref, b3_ref, fr0_ref, fr1_ref, absd_ref,
                 o_ref, *, seq):
    hp = lax.Precision.HIGHEST
    tl = ft_ref.shape[0]
    h = jnp.sin(fr0_ref[...] * (jnp.dot(ft_ref[...], w1_ref[...], precision=hp, preferred_element_type=F32)
                                + b1_ref[...]))
    h = jnp.sin(fr1_ref[...] * (jnp.dot(h, w2_ref[...], precision=hp, preferred_element_type=F32) + b2_ref[...]))
    out = _dot3(w3t_ref[...], h, NT_DIMS) + b3_ref[...]
    decay = jnp.exp(-absd_ref[...] * t_ref[...])
    pos = pl.program_id(0) * tl + lax.broadcasted_iota(I32, (1, tl), 1)
    for o in range(HYENA_ORDER):
        o_ref[o] = jnp.where(pos == seq, 0.0, out[o * B_WIDTH:(o + 1) * B_WIDTH] * decay)


def _hyena_kernels(seq, w1, b1, w2, b2, w3, b3, freq):
    t = jnp.linspace(0.0, 1.0, seq, dtype=F32)[:, None]
    w = 2.0 * math.pi * jnp.arange(seq, dtype=F32)[:, None] / seq
    bands = jnp.linspace(1e-4, FILTER_BANDS - 1, FILTER_BANDS, dtype=F32)[None, :]
    feats = jnp.concatenate([t, jnp.cos(bands * w), jnp.sin(bands * w)], axis=-1)
    feats = jnp.concatenate([feats, feats[:1], feats[:0:-1]], axis=0)
    t_row = feats[:, 0][None, :]
    feats = jnp.pad(feats, ((0, 0), (0, LANES - FILTER_EMB)))
    hpad = LANES - FILTER_HIDDEN
    w1p = jnp.pad(w1.astype(F32), ((0, LANES - FILTER_EMB), (0, hpad)))
    w2p = jnp.pad(w2.astype(F32), ((0, hpad), (0, hpad)))
    n_dir = HYENA_ORDER * B_WIDTH
    w3t = jnp.pad(w3.astype(F32), ((0, hpad), (0, 0))).T.reshape(FILTER_DIRS, n_dir, LANES)
    b3c = b3.astype(F32).reshape(FILTER_DIRS, n_dir, 1)
    b1p = jnp.pad(b1.astype(F32), (0, hpad))[None, :]
    b2p = jnp.pad(b2.astype(F32), (0, hpad))[None, :]
    fr = jnp.pad(freq.astype(F32), ((0, 0), (0, hpad)))
    deltas = jnp.linspace(math.log(DECAY_TARGET) / SLOW_DECAY_PCT,
                          math.log(DECAY_TARGET) / FAST_DECAY_PCT, B_WIDTH, dtype=F32)
    absd = jnp.abs(deltas)[:, None]
    tl = min(seq, 1024)
    nhalf = seq // tl
    const = lambda *shape: pl.BlockSpec(shape, lambda i: (0,) * len(shape))
    return pl.pallas_call(
        functools.partial(_filter_body, seq=seq),
        grid=(2 * nhalf,),
        in_specs=[
            pl.BlockSpec((tl, LANES), lambda i: (i, 0)),
            pl.BlockSpec((1, tl), lambda i: (0, i)),
            const(LANES, LANES), const(1, LANES), const(LANES, LANES), const(1, LANES),
            pl.BlockSpec((None, n_dir, LANES), lambda i: (i // nhalf, 0, 0)),
            pl.BlockSpec((None, n_dir, 1), lambda i: (i // nhalf, 0, 0)),
            const(1, LANES), const(1, LANES), const(B_WIDTH, 1),
        ],
        out_specs=pl.BlockSpec((HYENA_ORDER, B_WIDTH, tl), lambda i: (0, 0, i)),
        out_shape=jax.ShapeDtypeStruct((HYENA_ORDER, B_WIDTH, 2 * seq), F32),
        compiler_params=_cparams(("parallel",), 48),
        name="hyena_filter",
    )(feats, t_row, w1p, b1p, w2p, b2p, w3t, b3c, fr[0:1], fr[1:2], absd)


def _dft_tables(seq):
    n = 2 * seq
    nb = n // LANES
    k1 = jnp.arange(nb, dtype=I32)[:, None]
    ang1 = (2.0 * math.pi / nb) * ((k1 * jnp.arange(nb, dtype=I32)[None, :]) % nb).astype(F32)
    f1_full = jnp.concatenate([jnp.cos(ang1), -jnp.sin(ang1)], axis=0)
    f1_half = f1_full[:, : nb // 2]
    f4_half = f1_half.T
    angt = (2.0 * math.pi / n) * ((k1 * jnp.arange(LANES, dtype=I32)[None, :]) % n).astype(F32)
    twr, twi = jnp.cos(angt), -jnp.sin(angt)
    a = jnp.arange(LANES, dtype=I32)
    ang2 = (2.0 * math.pi / LANES) * ((a[:, None] * a[None, :]) % LANES).astype(F32)
    cr, ci = jnp.cos(ang2), -jnp.sin(ang2)
    m2 = jnp.concatenate([jnp.concatenate([cr, ci], axis=1), jnp.concatenate([-ci, cr], axis=1)], axis=0)
    m3 = jnp.concatenate([jnp.concatenate([cr, -ci], axis=1), jnp.concatenate([ci, cr], axis=1)], axis=0)
    return dict(f1_full=f1_full.astype(BF16), f1_half=f1_half.astype(BF16), f4_half=f4_half.astype(BF16),
                twr=twr, twi=twi, twr_h=twr.astype(BF16), twi_h=twi.astype(BF16),
                m2=m2.astype(BF16), m3=m3.astype(BF16))


def _fwd_spectrum(pairs, f1_ref, twr_ref, twi_ref, m2_ref):
    nb = twr_ref.shape[0]
    a_all = [_dot(f1_ref[...], jnp.concatenate(xs, axis=1).astype(BF16)) for xs in pairs]
    twr, twi = twr_ref[...], twi_ref[...]
    out = []
    for a in a_all:
        a = a.astype(twr.dtype)
        lhs = []
        for d in range(2):
            ar = a[:nb, d * LANES:(d + 1) * LANES]
            ai = a[nb:, d * LANES:(d + 1) * LANES]
            lhs.append(jnp.concatenate([ar * twr - ai * twi, ar * twi + ai * twr], axis=1))
        out.append(_dot(jnp.concatenate(lhs, axis=0).astype(BF16), m2_ref[...]))
    return out


def _spec_body(k_ref, f1_ref, twr_ref, twi_ref, m2_ref, o_ref):
    nb = twr_ref.shape[0]
    inv_n = 1.0 / (nb * LANES)

    def group(it, carry):
        c0 = 2 * HY_PAIRS * it
        pairs = [[k_ref[c0 + 2 * g], k_ref[c0 + 2 * g + 1]] for g in range(HY_PAIRS)]
        for g, z in enumerate(_fwd_spectrum(pairs, f1_ref, twr_ref, twi_ref, m2_ref)):
            o_ref[c0 + 2 * g] = (z[:nb] * inv_n).astype(o_ref.dtype)
            o_ref[c0 + 2 * g + 1] = (z[nb:] * inv_n).astype(o_ref.dtype)
        return carry
    lax.fori_loop(0, k_ref.shape[0] // (2 * HY_PAIRS), group, 0)


def _filter_spectrum(kt, tb):
    orders, c, nb, _ = kt.shape
    cb = HY_CB
    const = lambda *shape: pl.BlockSpec(shape, lambda o, j: (0,) * len(shape))
    return pl.pallas_call(
        _spec_body,
        grid=(orders, c // cb),
        in_specs=[
            pl.BlockSpec((None, cb, nb, LANES), lambda o, j: (o, j, 0, 0)),
            const(2 * nb, nb), const(nb, LANES), const(nb, LANES), const(2 * LANES, 2 * LANES),
        ],
        out_specs=pl.BlockSpec((None, cb, nb, 2 * LANES), lambda o, j: (o, j, 0, 0)),
        out_shape=jax.ShapeDtypeStruct((orders, c, nb, 2 * LANES), BF16),
        compiler_params=_cparams(("parallel", "parallel"), 48),
        name="hyena_filter_spectrum",
    )(kt, tb["f1_full"], tb["twr"], tb["twi"], tb["m2"])


def _shift_rows(x, down):
    rows = x.shape[0]
    idx = lax.broadcasted_iota(I32, x.shape, 0)
    if down:
        return jnp.where(idx == 0, 0.0, pltpu.roll(x, 1, axis=0))
    return jnp.where(idx == rows - 1, 0.0, pltpu.roll(x, rows - 1, axis=0))


def _short_conv(x, taps_ref, ch):
    lane = lax.broadcasted_iota(I32, x.shape, 1)
    prev = pltpu.roll(jnp.where(lane == LANES - 1, _shift_rows(x, True), x), 1, axis=1)
    nxt = pltpu.roll(jnp.where(lane == 0, _shift_rows(x, False), x), LANES - 1, axis=1)
    return taps_ref[0, ch] * prev + taps_ref[1, ch] * x + taps_ref[2, ch] * nxt


def _conv_body(taps_ref, skip_ref, z_ref, g_ref, kf_ref, f1_ref, f4_ref, twr_ref, twi_ref, m2_ref, m3_ref,
               o_ref, *, zch, gch, conv_z):
    nb = twr_ref.shape[0]
    cb = z_ref.shape[0]
    base = pl.program_id(0) * cb

    def group(it, carry):
        c0 = 2 * HY_PAIRS * it
        chans = [[c0 + 2 * g, c0 + 2 * g + 1] for g in range(HY_PAIRS)]
        xs = [[_short_conv(z_ref[c], taps_ref, zch + base + c) if conv_z else z_ref[c] for c in pr] for pr in chans]
        zs = _fwd_spectrum(xs, f1_ref, twr_ref, twi_ref, m2_ref)
        ccs = []
        for pr, z in zip(chans, zs):
            z = z.astype(kf_ref.dtype)
            ys = []
            for d, c in enumerate(pr):
                zr = z[d * nb:(d + 1) * nb, :LANES]
                zi = z[d * nb:(d + 1) * nb, LANES:]
                kf = kf_ref[c]
                kr, ki = kf[:, :LANES], kf[:, LANES:]
                ys.append(jnp.concatenate([zr * kr - zi * ki, zr * ki + zi * kr], axis=1))
            ccs.append(_dot(jnp.concatenate(ys, axis=0).astype(BF16), m3_ref[...]))
        twr, twi = twr_ref[...], twi_ref[...]
        for pr, x2, cc in zip(chans, xs, ccs):
            cc = cc.astype(twr.dtype)
            drs, dis = [], []
            for d in range(2):
                ccr = cc[d * nb:(d + 1) * nb, :LANES]
                cci = cc[d * nb:(d + 1) * nb, LANES:]
                drs.append(ccr * twr + cci * twi)
                dis.append(cci * twr - ccr * twi)
            rhs = jnp.concatenate([jnp.concatenate(drs, axis=1), jnp.concatenate(dis, axis=1)], axis=0)
            y = _dot(f4_ref[...], rhs.astype(BF16))
            for d, c in enumerate(pr):
                gate = _short_conv(g_ref[c], taps_ref, gch + base + c)
                o_ref[c] = gate * (y[:, d * LANES:(d + 1) * LANES] + skip_ref[base + c] * x2[d])
        return carry
    lax.fori_loop(0, cb // (2 * HY_PAIRS), group, 0)


def _hyena_conv(zsrc, zch, gsrc, gch, taps, skip, kf, order, tb, conv_z):
    bsz, _, nh, _ = zsrc.shape
    nb = 2 * nh
    cb = HY_CB
    smem = pl.BlockSpec(memory_space=pltpu.SMEM)
    const = lambda *shape: pl.BlockSpec(shape, lambda j, b: (0,) * len(shape))
    zblk, gblk = zch // cb, gch // cb
    return pl.pallas_call(
        functools.partial(_conv_body, zch=zch, gch=gch, conv_z=conv_z),
        grid=(B_WIDTH // cb, bsz),
        in_specs=[
            smem, smem,
            pl.BlockSpec((None, cb, nh, LANES), lambda j, b: (b, zblk + j, 0, 0)),
            pl.BlockSpec((None, cb, nh, LANES), lambda j, b: (b, gblk + j, 0, 0)),
            pl.BlockSpec((None, cb, nb, 2 * LANES), lambda j, b: (order, j, 0, 0)),
            const(2 * nb, nh), const(nh, 2 * nb), const(nb, LANES), const(nb, LANES),
            const(2 * LANES, 2 * LANES), const(2 * LANES, 2 * LANES),
        ],
        out_specs=pl.BlockSpec((None, cb, nh, LANES), lambda j, b: (b, j, 0, 0)),
        out_shape=jax.ShapeDtypeStruct((bsz, B_WIDTH, nh, LANES), F32),
        compiler_params=_cparams(("parallel", "parallel"), 48),
        name="hyena_conv",
    )(taps, skip, zsrc, gsrc, kf, tb["f1_half"], tb["f4_half"], tb["twr_h"], tb["twi_h"], tb["m2"], tb["m3"])


def _mix_ffn_body(xa_ref, xb_ref, ya_ref, ybt_ref, wa_ref, wb_ref, g_ref, wg_ref, wu_ref, wd_ref, o_ref, h_ref,
                  *, na_tiles):
    x = jnp.where(pl.program_id(0) < na_tiles, xa_ref[...], xb_ref[...])
    x = x + _dot(ya_ref[...], wa_ref[...])
    x = x + lax.dot_general(ybt_ref[...].astype(BF16), wb_ref[...], TN_DIMS, preferred_element_type=F32)
    h_ref[...] = (_rms_scale(x) * g_ref[...]).astype(BF16)
    o_ref[...] = x

    def add_chunk(cs):
        h = h_ref[...]
        a = (jax.nn.silu(_dot(h, wg_ref[:, cs])) * _dot(h, wu_ref[:, cs])).astype(BF16)
        o_ref[...] += _dot(a, wd_ref[cs, :])

    def chunk(c, carry):
        add_chunk(pl.ds(pl.multiple_of(c * FF_CHUNK, FF_CHUNK), FF_CHUNK))
        return carry
    n_full = D_FF // FF_CHUNK
    lax.fori_loop(0, n_full, chunk, 0)
    if D_FF % FF_CHUNK:
        add_chunk(slice(n_full * FF_CHUNK, D_FF))


def _mix_ffn(xa, xb, ya, ybt, wa, wb, g, wg, wu, wd):
    na, seq, _ = xa.shape
    bsz = na + xb.shape[0]
    tm = TM_FFN
    tps = seq // tm
    na_tiles = na * tps
    flat = lambda v: v.reshape(-1, v.shape[-1])
    resident = lambda w: pl.BlockSpec(w.shape, lambda i: (0, 0), pipeline_mode=pl.Buffered(1))
    return pl.pallas_call(
        functools.partial(_mix_ffn_body, na_tiles=na_tiles),
        grid=(bsz * tps,),
        in_specs=[
            pl.BlockSpec((tm, D_MODEL), lambda i: (jnp.minimum(i, na_tiles - 1), 0)),
            pl.BlockSpec((tm, D_MODEL), lambda i: (jnp.maximum(i - na_tiles, 0), 0)),
            pl.BlockSpec((tm, ya.shape[-1]), lambda i: (i, 0)),
            pl.BlockSpec((None, ybt.shape[1], tm), lambda i: (i // tps, 0, i % tps)),
            resident(wa), resident(wb), pl.BlockSpec((1, D_MODEL), lambda i: (0, 0)),
            resident(wg), resident(wu), resident(wd),
        ],
        out_specs=pl.BlockSpec((tm, D_MODEL), lambda i: (i, 0)),
        out_shape=jax.ShapeDtypeStruct((bsz * seq, D_MODEL), F32),
        scratch_shapes=[pltpu.VMEM((tm, D_MODEL), BF16)],
        compiler_params=_cparams(("parallel",), 56),
        name="mix_ffn",
    )(flat(xa), flat(xb), flat(ya), ybt, wa, wb, g, wg, wu, wd)


def _qkv_body(x_ref, g_ref, w_ref, qg_ref, kg_ref, q_ref, k_ref, v_ref):
    h = (_rms_scale(x_ref[...]) * g_ref[...]).astype(BF16)
    nq = N_HEADS * HEAD_DIM
    nk = N_KV_HEADS * LANES
    lo = lax.broadcasted_iota(I32, (1, LANES), 1) < HEAD_DIM
    qgain = qg_ref[...] * (HEAD_DIM ** -0.5)
    wide = 2 * LANES
    for c2 in range(nq // wide):
        xw = _dot(h, w_ref[:, c2 * wide:(c2 + 1) * wide])
        for half in range(2):
            c = 2 * c2 + half
            x = xw[:, half * LANES:(half + 1) * LANES]
            x2 = x * x
            s_lo = jnp.sum(jnp.where(lo, x2, 0.0), axis=-1, keepdims=True)
            s_hi = jnp.sum(jnp.where(lo, 0.0, x2), axis=-1, keepdims=True)
            r = jnp.where(lo, lax.rsqrt(s_lo / HEAD_DIM + EPS), lax.rsqrt(s_hi / HEAD_DIM + EPS))
            q_ref[:, c * LANES:(c + 1) * LANES] = (x * r * qgain).astype(q_ref.dtype)
    for c2 in range(nk // wide):
        xw = _dot(h, w_ref[:, nq + c2 * wide:nq + (c2 + 1) * wide])
        for half in range(2):
            c = 2 * c2 + half
            x = xw[:, half * LANES:(half + 1) * LANES]
            k_ref[:, c * LANES:(c + 1) * LANES] = (_rms_scale(x) * kg_ref[...]).astype(k_ref.dtype)
    v_ref[...] = _dot(h, w_ref[:, nq + nk:]).astype(v_ref.dtype)


def _qkv_proj(x, g, w_qkv, q_g, k_g):
    bsz, seq, _ = x.shape
    tm = TM_PROJ
    nq = N_HEADS * HEAD_DIM
    nkv = N_KV_HEADS * HEAD_DIM
    dup = lambda w: jnp.tile(w.reshape(D_MODEL, N_KV_HEADS, 1, HEAD_DIM), (1, 1, 2, 1)).reshape(D_MODEL, 2 * nkv)
    w = jnp.concatenate([w_qkv[:, :nq], dup(w_qkv[:, nq:nq + nkv]), dup(w_qkv[:, nq + nkv:])], axis=1).astype(BF16)
    two = lambda v: jnp.tile(v.astype(F32), 2)[None, :]
    row = lambda width: pl.BlockSpec((None, tm, width), lambda b, i: (b, i, 0))
    const = lambda *shape: pl.BlockSpec(shape, lambda b, i: (0,) * len(shape))
    return pl.pallas_call(
        _qkv_body,
        grid=(bsz, seq // tm),
        in_specs=[row(D_MODEL), const(1, D_MODEL), const(D_MODEL, w.shape[1]), const(1, LANES), const(1, LANES)],
        out_specs=[row(nq), row(2 * nkv), row(2 * nkv)],
        out_shape=[jax.ShapeDtypeStruct((bsz, seq, nq), BF16), jax.ShapeDtypeStruct((bsz, seq, 2 * nkv), BF16),
                   jax.ShapeDtypeStruct((bsz, seq, 2 * nkv), BF16)],
        compiler_params=_cparams(("parallel", "parallel"), 48),
        name="qkv_proj",
    )(x, g, w, two(q_g), two(k_g))


def _t5_bucket(rel):
    nbk = REL_BUCKETS // 2
    max_exact = nbk // 2
    ret = jnp.where(rel > 0, nbk, 0)
    n = jnp.abs(rel)
    large = max_exact + (jnp.log(jnp.maximum(n, 1).astype(F32) / max_exact)
                         / math.log(REL_MAX_DIST / max_exact) * (nbk - max_exact)).astype(I32)
    large = jnp.minimum(large, nbk - 1)
    return ret + jnp.where(n < max_exact, n, large)


def _attn_body(sink_ref, q_ref, kp_ref, ko_ref, kn_ref, vp_ref, vo_ref, vn_ref, bm_ref, o_ref):
    kb = 3 * ATT_BLOCK
    lo = lax.broadcasted_iota(I32, (1, LANES), 1) < HEAD_DIM
    first = lax.broadcasted_iota(I32, (2 * ATT_BLOCK, 1), 0) < ATT_BLOCK
    for hk in range(N_KV_HEADS):
        ks = slice(hk * LANES, (hk + 1) * LANES)
        kk = jnp.concatenate([kp_ref[:, ks], ko_ref[:, ks], kn_ref[:, ks]], axis=0)
        vv = jnp.concatenate([vp_ref[:, ks], vo_ref[:, ks], vn_ref[:, ks]], axis=0)
        zero = jnp.zeros_like(kk)
        kz = jnp.concatenate([jnp.where(lo, kk, zero), jnp.where(lo, zero, kk)], axis=0)
        vz = jnp.concatenate([jnp.where(lo, vv, zero), jnp.where(lo, zero, vv)], axis=0)
        c0 = 2 * hk
        ql = jnp.concatenate([q_ref[:, c0 * LANES:(c0 + 1) * LANES], q_ref[:, (c0 + 1) * LANES:(c0 + 2) * LANES]],
                             axis=0)
        s_all = lax.dot_general(ql, kz, NT_DIMS, preferred_element_type=F32)
        probs, invs = [], []
        for par in range(2):
            ha, hb = GQA_GROUP * hk + par, GQA_GROUP * hk + 2 + par
            s = s_all[:, par * kb:(par + 1) * kb] + jnp.concatenate([bm_ref[ha], bm_ref[hb]], axis=0)
            sk = jnp.where(first, sink_ref[ha], sink_ref[hb])
            m = jnp.maximum(jnp.max(s, axis=-1, keepdims=True), sk)
            pexp = jnp.exp(s - m)
            invs.append(1.0 / (jnp.sum(pexp, axis=-1, keepdims=True) + jnp.exp(sk - m)))
            probs.append(pexp.astype(BF16))
        acc = _dot(jnp.concatenate(probs, axis=1), vz) * jnp.where(lo, invs[0], invs[1])
        o_ref[:, c0 * LANES:(c0 + 1) * LANES] = acc[:ATT_BLOCK].astype(o_ref.dtype)
        o_ref[:, (c0 + 1) * LANES:(c0 + 2) * LANES] = acc[ATT_BLOCK:].astype(o_ref.dtype)


def _attention(q, k2, v2, sink, rel_bias):
    bsz, seq, _ = q.shape
    nblk = seq // ATT_BLOCK
    kb = 3 * ATT_BLOCK
    rel = jnp.arange(kb)[None, :] - ATT_BLOCK - jnp.arange(ATT_BLOCK)[:, None]
    bucket = _t5_bucket(rel)
    rb = rel_bias.astype(F32)
    bias = sum(jnp.where(bucket[None] == b, rb[b][:, None, None], 0.0) for b in range(REL_BUCKETS))
    bm = jnp.where((jnp.abs(rel) <= WINDOW)[None], bias, NEG_INF)
    assert nblk >= 2
    kcol = jnp.arange(kb)[None, None, :]
    bm3 = jnp.stack([jnp.where(kcol >= ATT_BLOCK, bm, NEG_INF), bm, jnp.where(kcol < 2 * ATT_BLOCK, bm, NEG_INF)])
    which = lambda i: jnp.where(i == 0, 0, jnp.where(i == nblk - 1, 2, 1))
    kvw = k2.shape[2]
    prev = lambda b, i: (b, jnp.maximum(i - 1, 0), 0)
    own = lambda b, i: (b, i, 0)
    nxt = lambda b, i: (b, jnp.minimum(i + 1, nblk - 1), 0)
    kv_spec = lambda fn: pl.BlockSpec((None, ATT_BLOCK, kvw), fn)
    return pl.pallas_call(
        _attn_body,
        grid=(bsz, nblk),
        in_specs=[
            pl.BlockSpec(memory_space=pltpu.SMEM),
            pl.BlockSpec((None, ATT_BLOCK, N_HEADS * HEAD_DIM), own),
            kv_spec(prev), kv_spec(own), kv_spec(nxt),
            kv_spec(prev), kv_spec(own), kv_spec(nxt),
            pl.BlockSpec((None, N_HEADS, ATT_BLOCK, kb), lambda b, i: (which(i), 0, 0, 0)),
        ],
        out_specs=pl.BlockSpec((None, ATT_BLOCK, N_HEADS * HEAD_DIM), own),
        out_shape=jax.ShapeDtypeStruct((bsz, seq, N_HEADS * HEAD_DIM), BF16),
        compiler_params=_cparams(("parallel", "parallel"), 48),
        name="window_attention",
    )(sink.astype(F32), q, k2, k2, k2, v2, v2, v2, bm3)


def _router_body(x_ref, o_ref, wo_ref, g_ref, wr_ref, tri_ref, xn_ref, hp_ref, idx_ref, rank_ref, gate_ref, cnt_ref,
                 run_ref):
    @pl.when(pl.program_id(0) == 0)
    def _():
        run_ref[...] = jnp.zeros_like(run_ref)

    x = x_ref[...] + _dot(o_ref[...], wo_ref[...])
    xn_ref[...] = x
    h = _rms_scale(x) * g_ref[...]
    half = D_MODEL // 2
    bits = lax.bitcast_convert_type(h.astype(BF16).astype(F32), U32)
    hp_ref[...] = (bits[:, half:] & jnp.uint32(0xFFFF0000)) | (bits[:, :half] >> 16)

    logits = _dot3(wr_ref[...], h, NT_DIMS)
    eid = lax.broadcasted_iota(I32, logits.shape, 0)
    m1 = jnp.max(logits, axis=0, keepdims=True)
    i1 = jnp.min(jnp.where(logits == m1, eid, N_EXPERTS), axis=0, keepdims=True)
    rest = jnp.where(eid == i1, -jnp.inf, logits)
    m2 = jnp.max(rest, axis=0, keepdims=True)
    i2 = jnp.min(jnp.where(rest == m2, eid, N_EXPERTS), axis=0, keepdims=True)
    e2 = jnp.exp(m2 - m1)
    gate_ref[0:1, :] = 1.0 / (1.0 + e2)
    gate_ref[1:2, :] = e2 / (1.0 + e2)
    idx_ref[0:1, :] = i1
    idx_ref[1:2, :] = i2

    sel1 = eid == i1
    sel2 = eid == i2
    onehot = jnp.where(sel1 | sel2, 1.0, 0.0)
    before = _dot(onehot.astype(BF16), tri_ref[...]) + run_ref[:, 0:1]
    rank_ref[0:1, :] = jnp.sum(jnp.where(sel1, before, 0.0), axis=0, keepdims=True).astype(I32)
    rank_ref[1:2, :] = jnp.sum(jnp.where(sel2, before, 0.0), axis=0, keepdims=True).astype(I32)
    run_ref[...] += jnp.sum(onehot, axis=1, keepdims=True)
    cnt_ref[...] = run_ref[...].astype(I32)


def _router(x2d, o2d, w_out, g, w_router):
    t = x2d.shape[0]
    tm = TM_ROUTER
    tri = (jnp.arange(tm)[:, None] < jnp.arange(tm)[None, :]).astype(BF16)
    two = lambda dt: jax.ShapeDtypeStruct((2, t), dt)
    return pl.pallas_call(
        _router_body,
        grid=(t // tm,),
        in_specs=[
            pl.BlockSpec((tm, D_MODEL), lambda i: (i, 0)),
            pl.BlockSpec((tm, o2d.shape[1]), lambda i: (i, 0)),
            pl.BlockSpec(w_out.shape, lambda i: (0, 0)),
            pl.BlockSpec((1, D_MODEL), lambda i: (0, 0)),
            pl.BlockSpec((N_EXPERTS, D_MODEL), lambda i: (0, 0)),
            pl.BlockSpec((tm, tm), lambda i: (0, 0)),
        ],
        out_specs=[
            pl.BlockSpec((tm, D_MODEL), lambda i: (i, 0)),
            pl.BlockSpec((tm, D_MODEL // 2), lambda i: (i, 0)),
            pl.BlockSpec((2, tm), lambda i: (0, i)),
            pl.BlockSpec((2, tm), lambda i: (0, i)),
            pl.BlockSpec((2, tm), lambda i: (0, i)),
            pl.BlockSpec((N_EXPERTS, LANES), lambda i: (0, 0)),
        ],
        out_shape=[
            jax.ShapeDtypeStruct((t, D_MODEL), F32),
            jax.ShapeDtypeStruct((t, D_MODEL // 2), U32),
            two(I32), two(I32), two(F32),
            jax.ShapeDtypeStruct((N_EXPERTS, LANES), I32),
        ],
        scratch_shapes=[pltpu.VMEM((N_EXPERTS, LANES), F32)],
        compiler_params=_cparams(("arbitrary",), 48),
        name="moe_router",
    )(x2d, o2d, w_out, g, w_router.astype(F32).T, tri)


def _dispatch_body(d1_ref, d2_ref, hp_ref, init_ref, xs_ref, sem, *, rows):
    del init_ref

    def issue(k, c):
        r0 = pl.multiple_of(k * SUBLANES, SUBLANES)
        for u in range(SUBLANES):
            src = hp_ref.at[pl.ds(r0 + u, 1)]
            pltpu.make_async_copy(src, xs_ref.at[pl.ds(d1_ref[0, 0, r0 + u], 1)], sem.at[0]).start()
            pltpu.make_async_copy(src, xs_ref.at[pl.ds(d2_ref[0, 0, r0 + u], 1)], sem.at[1]).start()
        return c
    lax.fori_loop(0, rows // SUBLANES, issue, 0)
    pltpu.make_async_copy(hp_ref, xs_ref.at[pl.ds(0, rows)], sem.at[0]).wait()
    pltpu.make_async_copy(hp_ref, xs_ref.at[pl.ds(0, rows)], sem.at[1]).wait()


def _dispatch(hp, dest, n_rows):
    t, width = hp.shape
    rows = TM_DISPATCH
    idx_spec = pl.BlockSpec((1, 1, rows), lambda i: (i, 0, 0), memory_space=pltpu.SMEM)
    return pl.pallas_call(
        functools.partial(_dispatch_body, rows=rows),
        grid=(t // rows,),
        in_specs=[idx_spec, idx_spec, pl.BlockSpec((rows, width), lambda i: (i, 0)),
                  pl.BlockSpec(memory_space=pl.ANY)],
        out_specs=pl.BlockSpec(memory_space=pl.ANY),
        out_shape=jax.ShapeDtypeStruct((n_rows, width), hp.dtype),
        input_output_aliases={3: 0},
        scratch_shapes=[pltpu.SemaphoreType.DMA((2,))],
        compiler_params=_cparams(("arbitrary",), 32),
        name="moe_dispatch",
    )(dest[0].reshape(t // rows, 1, rows), dest[1].reshape(t // rows, 1, rows), hp,
      jnp.zeros((n_rows, width), hp.dtype))


def _expert_body(te_ref, nu_ref, xs_ref, wg_ref, wu_ref, wd_ref, o_ref, xb_ref, acc_ref):
    i = pl.program_id(0)
    j = pl.program_id(1)
    half = D_MODEL // 2

    @pl.when(j == 0)
    def _():
        w = xs_ref[...]
        xb_ref[:, :half] = lax.bitcast_convert_type(w << 16, F32).astype(BF16)
        xb_ref[:, half:] = lax.bitcast_convert_type(w & jnp.uint32(0xFFFF0000), F32).astype(BF16)
        acc_ref[...] = jnp.zeros_like(acc_ref)

    @pl.when(i < nu_ref[0])
    def _():
        xb = xb_ref[...]
        a = (jax.nn.silu(_dot(xb, wg_ref[...])) * _dot(xb, wu_ref[...])).astype(BF16)
        acc_ref[...] += _dot(a, wd_ref[...])

    @pl.when(j == pl.num_programs(1) - 1)
    def _():
        o_ref[...] = acc_ref[...]


def _experts(xs, tile_expert, n_used, wg, wu, wd):
    p = xs.shape[0]
    tm, tf = TM_MOE, TF_MOE
    grid_spec = pltpu.PrefetchScalarGridSpec(
        num_scalar_prefetch=2,
        grid=(p // tm, D_FF_EXPERT // tf),
        in_specs=[
            pl.BlockSpec((tm, D_MODEL // 2), lambda i, j, te, nu: (i, 0)),
            pl.BlockSpec((None, D_MODEL, tf), lambda i, j, te, nu: (te[i], 0, j)),
            pl.BlockSpec((None, D_MODEL, tf), lambda i, j, te, nu: (te[i], 0, j)),
            pl.BlockSpec((None, tf, D_MODEL), lambda i, j, te, nu: (te[i], j, 0)),
        ],
        out_specs=pl.BlockSpec((tm, D_MODEL), lambda i, j, te, nu: (i, 0)),
        scratch_shapes=[pltpu.VMEM((tm, D_MODEL), BF16), pltpu.VMEM((tm, D_MODEL), F32)],
    )
    return pl.pallas_call(
        _expert_body,
        grid_spec=grid_spec,
        out_shape=jax.ShapeDtypeStruct((p, D_MODEL), F32),
        compiler_params=_cparams(("parallel", "arbitrary"), 56),
        name="moe_experts",
    )(tile_expert, n_used, xs, wg, wu, wd)


def _combine_body(d1c_ref, d2c_ref, d1n_ref, d2n_ref, x_ref, g1_ref, g2_ref, ys_ref, oa_ref, ob_ref,
                  y1_ref, y2_ref, sem, *, rows, na_blocks):
    i = pl.program_id(0)
    slot = i % 2

    def gather(d1_ref, d2_ref, s):
        def issue(k, c):
            r0 = pl.multiple_of(k * SUBLANES, SUBLANES)
            for u in range(SUBLANES):
                pltpu.make_async_copy(ys_ref.at[pl.ds(d1_ref[0, 0, r0 + u], 1)], y1_ref.at[s, pl.ds(r0 + u, 1)],
                                      sem.at[0, s]).start()
                pltpu.make_async_copy(ys_ref.at[pl.ds(d2_ref[0, 0, r0 + u], 1)], y2_ref.at[s, pl.ds(r0 + u, 1)],
                                      sem.at[1, s]).start()
            return c
        lax.fori_loop(0, rows // SUBLANES, issue, 0)

    @pl.when(i == 0)
    def _():
        gather(d1c_ref, d2c_ref, 0)

    @pl.when(i + 1 < pl.num_programs(0))
    def _():
        gather(d1n_ref, d2n_ref, 1 - slot)

    pltpu.make_async_copy(ys_ref.at[pl.ds(0, rows)], y1_ref.at[slot], sem.at[0, slot]).wait()
    pltpu.make_async_copy(ys_ref.at[pl.ds(0, rows)], y2_ref.at[slot], sem.at[1, slot]).wait()
    val = x_ref[...] + g1_ref[...] * y1_ref[slot] + g2_ref[...] * y2_ref[slot]

    @pl.when(i < na_blocks)
    def _():
        oa_ref[...] = val

    @pl.when(i >= na_blocks)
    def _():
        ob_ref[...] = val


def _combine(x2d, ys, dest, gates, t_a):
    t = x2d.shape[0]
    rows = TM_COMBINE
    n = t // rows
    na = t_a // rows
    cur = pl.BlockSpec((1, 1, rows), lambda i: (i, 0, 0), memory_space=pltpu.SMEM)
    nxt = pl.BlockSpec((1, 1, rows), lambda i: (jnp.minimum(i + 1, n - 1), 0, 0), memory_space=pltpu.SMEM)
    gate_spec = pl.BlockSpec((rows, 1), lambda i: (i, 0))
    row_spec = pl.BlockSpec((rows, D_MODEL), lambda i: (i, 0))
    d1 = dest[0].reshape(n, 1, rows)
    d2 = dest[1].reshape(n, 1, rows)
    return pl.pallas_call(
        functools.partial(_combine_body, rows=rows, na_blocks=na),
        grid=(n,),
        in_specs=[cur, cur, nxt, nxt, row_spec, gate_spec, gate_spec, pl.BlockSpec(memory_space=pl.ANY)],
        out_specs=[pl.BlockSpec((rows, D_MODEL), lambda i: (jnp.minimum(i, na - 1), 0)),
                   pl.BlockSpec((rows, D_MODEL), lambda i: (jnp.maximum(i - na, 0), 0))],
        out_shape=[jax.ShapeDtypeStruct((t_a, D_MODEL), F32), jax.ShapeDtypeStruct((t - t_a, D_MODEL), F32)],
        scratch_shapes=[pltpu.VMEM((2, rows, D_MODEL), F32), pltpu.VMEM((2, rows, D_MODEL), F32),
                        pltpu.SemaphoreType.DMA((2, 2))],
        compiler_params=_cparams(("arbitrary",), 48),
        name="moe_combine",
    )(d1, d2, d1, d2, x2d, gates[0][:, None], gates[1][:, None], ys)


def _moe(x2d, o2d, w_out, g, w_router, wg, wu, wd, t_a):
    t = x2d.shape[0]
    tm = TM_MOE
    x2d, hp, idx, rank, gates, cnt = _router(x2d, o2d, w_out, g, w_router)
    counts = cnt[:, 0]
    tiles = (counts + tm - 1) // tm
    tile_end = jnp.cumsum(tiles)
    row_start = (tile_end - tiles) * tm
    n_tiles = (2 * t) // tm + N_EXPERTS
    eid = jnp.arange(N_EXPERTS, dtype=I32)[:, None, None]
    dest = jnp.sum(jnp.where(idx[None] == eid, row_start[:, None, None], 0), axis=0) + rank
    tile_expert = jnp.minimum(jnp.searchsorted(tile_end, jnp.arange(n_tiles), side="right"), N_EXPERTS - 1)
    xs = _dispatch(hp, dest, n_tiles * tm)
    ys = _experts(xs, tile_expert.astype(I32), tile_end[-1:].astype(I32), wg, wu, wd)
    return _combine(x2d, ys, dest, gates, t_a)


def _even_layer(xa, xb, norm_mix, norm_ffn, w_in, w_out, ln_g, ln_b, sgu_w, sgu_b, hy_conv,
                f_w1, f_b1, f_w2, f_b2, f_w3, f_b3, f_freq, hy_skip, wg, wu, wd):
    seq = xa.shape[1]
    bsz = xa.shape[0] + xb.shape[0]
    nh = seq // LANES
    sgu_bb = jnp.broadcast_to(sgu_b.astype(F32)[:, :, None], (A_GROUPS, CHUNK, LANES))
    w_in = w_in.astype(BF16)
    ya, hbt = _even_in(xa, xb, norm_mix[None, :], w_in[:, :2 * A_WIDTH], w_in[:, 2 * A_WIDTH:].T,
                       ln_g[None, :], ln_b[None, :], sgu_w.astype(BF16), sgu_bb)
    hbt = hbt.reshape(bsz, -1, nh, LANES)

    tb = _dft_tables(seq)
    kt = _hyena_kernels(seq, f_w1, f_b1, f_w2, f_b2, f_w3, f_b3, f_freq)
    kf = _filter_spectrum(kt.reshape(HYENA_ORDER, B_WIDTH, 2 * nh, LANES), tb)

    taps = hy_conv.astype(F32)
    skip = hy_skip.astype(F32)
    z1 = _hyena_conv(hbt, 0, hbt, B_WIDTH, taps, skip[0], kf, 0, tb, conv_z=True)
    ybt = _hyena_conv(z1, 0, hbt, 2 * B_WIDTH, taps, skip[1], kf, 1, tb, conv_z=False)

    w_out = w_out.astype(BF16)
    x2d = _mix_ffn(xa, xb, ya, ybt.reshape(bsz, B_WIDTH, seq), w_out[:A_WIDTH], w_out[A_WIDTH:],
                   norm_ffn[None, :], wg.astype(BF16), wu.astype(BF16), wd.astype(BF16))
    return x2d.reshape(bsz, seq, D_MODEL)


def _odd_layer(x, n_a, norm_mix, norm_ffn, w_qkv, q_g, k_g, sink, w_out, rel_bias, w_router, wg, wu, wd):
    bsz, seq, _ = x.shape
    q, k2, v2 = _qkv_proj(x, norm_mix[None, :], w_qkv, q_g, k_g)
    o = _attention(q, k2, v2, sink, rel_bias)
    ya, yb = _moe(x.reshape(bsz * seq, D_MODEL), o.reshape(bsz * seq, -1), w_out.astype(BF16), norm_ffn[None, :],
                  w_router, wg.astype(BF16), wu.astype(BF16), wd.astype(BF16), n_a * seq)
    return ya.reshape(n_a, seq, D_MODEL), yb.reshape(bsz - n_a, seq, D_MODEL)


def kernel(x_prompt, x_sample, norm_mix, norm_ffn, ev_w_in, ev_w_out, sgu_ln_g, sgu_ln_b, sgu_w, sgu_b,
           hy_conv, hy_f_w1, hy_f_b1, hy_f_w2, hy_f_b2, hy_f_w3, hy_f_b3, hy_f_freq, hy_skip,
           ffn_w_gate, ffn_w_up, ffn_w_down, at_w_qkv, at_q_norm, at_k_norm, at_sink, at_w_out,
           rel_bias, moe_router, moe_w_gate, moe_w_up, moe_w_down):
    even_p = (ev_w_in, ev_w_out, sgu_ln_g, sgu_ln_b, sgu_w, sgu_b, hy_conv, hy_f_w1, hy_f_b1, hy_f_w2,
              hy_f_b2, hy_f_w3, hy_f_b3, hy_f_freq, hy_skip, ffn_w_gate, ffn_w_up, ffn_w_down)
    odd_p = (at_w_qkv, at_q_norm, at_k_norm, at_sink, at_w_out, moe_router, moe_w_gate, moe_w_up, moe_w_down)
    assert x_prompt.shape[1:] == x_sample.shape[1:]
    n_a = x_prompt.shape[0]
    depth = norm_mix.shape[0]
    assert depth % 2 == 0, "layers come in (even, odd) pairs"
    xa, xb = x_prompt, x_sample
    for i in range(0, depth, 2):
        j = i // 2
        x = _even_layer(xa, xb, norm_mix[i], norm_ffn[i], *[p[j] for p in even_p])
        xa, xb = _odd_layer(x, n_a, norm_mix[i + 1], norm_ffn[i + 1], *[p[j] for p in odd_p[:5]], rel_bias,
                            *[p[j] for p in odd_p[5:]])
    return (xa, xb)
```

```python
import functools
import math

import jax
import jax.numpy as jnp
from jax import lax
from jax.experimental import pallas as pl
from jax.experimental.pallas import tpu as pltpu

F32 = jnp.float32
BF16 = jnp.bfloat16
U32 = jnp.uint32
I32 = jnp.int32

D_MODEL = 1024
A_GROUPS = 4
A_WIDTH = D_MODEL // 2
CHUNK = 128
B_WIDTH = D_MODEL // 2
HYENA_ORDER = 2
FILTER_DIRS = 2
FILTER_BANDS = 16
FILTER_EMB = 1 + 2 * FILTER_BANDS
FILTER_HIDDEN = 64
DECAY_TARGET = 1e-2
FAST_DECAY_PCT = 0.3
SLOW_DECAY_PCT = 1.5
HEAD_DIM = 64
N_HEADS = D_MODEL // HEAD_DIM
N_KV_HEADS = N_HEADS // 4
GQA_GROUP = N_HEADS // N_KV_HEADS
WINDOW = 128
ATT_BLOCK = 128
REL_BUCKETS = 32
REL_MAX_DIST = 128
NEG_INF = -1e30
D_FF = 2816
N_EXPERTS = 8
D_FF_EXPERT = 3584
EPS = 1e-6

LANES = 128
SUBLANES = 8
MIB = 1024 * 1024

TM_PROJ = 512
TM_FFN = 512
FF_CHUNK = 1024
TM_MOE = 512
TF_MOE = D_FF_EXPERT // 2
TM_ROUTER = 512
TM_COMBINE = 256
TM_DISPATCH = 512
CAST_ROWS = 512
HY_CB = 32
HY_PAIRS = 8

NT_DIMS = (((1,), (1,)), ((), ()))
TN_DIMS = (((0,), (0,)), ((), ()))


def _cparams(sem, vmem_mib):
    return pltpu.CompilerParams(dimension_semantics=sem, vmem_limit_bytes=vmem_mib * MIB)


def _rms_scale(x):
    return x * lax.rsqrt(jnp.mean(x * x, axis=-1, keepdims=True) + EPS)


def _dot(a, b):
    return jnp.dot(a, b, preferred_element_type=F32)


def _dot3(a, b, dims):
    def split(x):
        hi = x.astype(BF16)
        return hi, (x - hi.astype(F32)).astype(BF16)
    (ah, al), (bh, bl) = split(a), split(b)
    mm = lambda x, y: lax.dot_general(x, y, dims, preferred_element_type=F32)
    return mm(ah, bh) + (mm(al, bh) + mm(ah, bl))


def _cast_body(x_ref, o_ref):
    o_ref[...] = x_ref[...].astype(o_ref.dtype)


def _to_bf16(w):
    e, r, c = w.shape
    rows = min(r, CAST_ROWS)
    spec = pl.BlockSpec((None, rows, c), lambda i, j: (i, j, 0))
    return pl.pallas_call(
        _cast_body,
        grid=(e, r // rows),
        in_specs=[spec],
        out_specs=spec,
        out_shape=jax.ShapeDtypeStruct(w.shape, BF16),
        compiler_params=_cparams(("parallel", "parallel"), 48),
        name="cast_bf16",
    )(w)


def _group_specs(xa, xb, tm):
    na, seq = xa.shape[0], xa.shape[1]
    last = seq // tm - 1
    width = xa.shape[2]
    spec_a = pl.BlockSpec((None, tm, width), lambda b, i: (jnp.minimum(b, na - 1), jnp.where(b < na, i, last), 0))
    spec_b = pl.BlockSpec((None, tm, width), lambda b, i: (jnp.maximum(b - na, 0), jnp.where(b < na, 0, i), 0))
    return spec_a, spec_b


def _group_tile(xa_ref, xb_ref, na):
    return jnp.where(pl.program_id(0) < na, xa_ref[...], xb_ref[...])


def _even_in_body(xa_ref, xb_ref, g_ref, w_ref, wht_ref, lng_ref, lnb_ref, sw_ref, sb_ref, ya_ref, hbt_ref, *, na):
    x = _group_tile(xa_ref, xb_ref, na)
    h = (_rms_scale(x) * g_ref[...]).astype(BF16)
    u = jax.nn.gelu(_dot(h, w_ref[:, 0:A_WIDTH]))
    v = jax.nn.gelu(_dot(h, w_ref[:, A_WIDTH:2 * A_WIDTH]))
    hbt_ref[...] = lax.dot_general(wht_ref[...], h, NT_DIMS, preferred_element_type=F32)
    tm = x.shape[0]
    for gi in range(A_GROUPS):
        cs = slice(gi * LANES, (gi + 1) * LANES)
        vg = v[:, cs]
        xc = vg - jnp.mean(vg, axis=-1, keepdims=True)
        var = jnp.mean(xc * xc, axis=-1, keepdims=True)
        vn = (xc * lax.rsqrt(var + EPS) * lng_ref[:, cs] + lnb_ref[:, cs]).astype(BF16)
        for c in range(tm // CHUNK):
            rs = slice(c * CHUNK, (c + 1) * CHUNK)
            mixed = _dot(sw_ref[gi], vn[rs]) + sb_ref[gi]
            ya_ref[rs, cs] = (u[rs, cs] * mixed).astype(ya_ref.dtype)


def _even_in(xa, xb, g, w_uv, w_hb_t, ln_g, ln_b, sgu_w, sgu_b):
    na, seq, _ = xa.shape
    bsz = na + xb.shape[0]
    tm = TM_PROJ
    n_hb = w_hb_t.shape[0]
    const = lambda *shape: pl.BlockSpec(shape, lambda b, i: (0,) * len(shape))
    return pl.pallas_call(
        functools.partial(_even_in_body, na=na),
        grid=(bsz, seq // tm),
        in_specs=[
            *_group_specs(xa, xb, tm),
            const(1, D_MODEL),
            const(D_MODEL, 2 * A_WIDTH),
            const(n_hb, D_MODEL),
            const(1, A_WIDTH),
            const(1, A_WIDTH),
            const(A_GROUPS, CHUNK, CHUNK),
            const(A_GROUPS, CHUNK, LANES),
        ],
        out_specs=[
            pl.BlockSpec((None, tm, A_WIDTH), lambda b, i: (b, i, 0)),
            pl.BlockSpec((None, n_hb, tm), lambda b, i: (b, 0, i)),
        ],
        out_shape=[
            jax.ShapeDtypeStruct((bsz, seq, A_WIDTH), BF16),
            jax.ShapeDtypeStruct((bsz, n_hb, seq), F32),
        ],
        compiler_params=_cparams(("parallel", "parallel"), 48),
        name="even_in",
    )(xa, xb, g, w_uv, w_hb_t, ln_g, ln_b, sgu_w, sgu_b)


def _filter_body(ft_ref, w1_ref, b1_ref, w2_ref, b2_ref, w3_ref, b3_ref, fr0_ref, fr1_ref, absd_ref, o_ref, *, seq):
    hp = lax.Precision.HIGHEST
    nn = (((1,), (0,)), ((), ()))
    ft = ft_ref[...]
    tl = ft.shape[1]
    h = jnp.sin(fr0_ref[...] * (jnp.dot(w1_ref[...], ft, precision=hp, preferred_element_type=F32) + b1_ref[...]))
    h = jnp.sin(fr1_ref[...] * (jnp.dot(w2_ref[...], h, precision=hp, preferred_element_type=F32) + b2_ref[...]))
    out = _dot3(w3_ref[...], h, nn) + b3_ref[...]
    decay = jnp.exp(-absd_ref[...] * ft[0:1, :])
    pos = pl.program_id(0) * tl + lax.broadcasted_iota(I32, (1, tl), 1)
    for o in range(HYENA_ORDER):
        o_ref[o] = jnp.where(pos == seq, 0.0, out[o * B_WIDTH:(o + 1) * B_WIDTH] * decay)


def _hyena_kernels(seq, w1, b1, w2, b2, w3, b3, freq):
    t = jnp.linspace(0.0, 1.0, seq, dtype=F32)[:, None]
    w = 2.0 * math.pi * jnp.arange(seq, dtype=F32)[:, None] / seq
    bands = jnp.linspace(1e-4, FILTER_BANDS - 1, FILTER_BANDS, dtype=F32)[None, :]
    feats = jnp.concatenate([t, jnp.cos(bands * w), jnp.sin(bands * w)], axis=-1)
    feats = jnp.concatenate([feats, feats[:1], feats[:0:-1]], axis=0)
    emb = -(-FILTER_EMB // SUBLANES) * SUBLANES
    feats_t = jnp.pad(feats, ((0, 0), (0, emb - FILTER_EMB))).T
    w1t = jnp.pad(w1.astype(F32), ((0, emb - FILTER_EMB), (0, 0))).T
    w2t = w2.astype(F32).T
    n_dir = HYENA_ORDER * B_WIDTH
    w3t = w3.astype(F32).T.reshape(FILTER_DIRS, n_dir, FILTER_HIDDEN)
    col = lambda v: v.astype(F32)[:, None]
    b3c = b3.astype(F32).reshape(FILTER_DIRS, n_dir, 1)
    deltas = jnp.linspace(math.log(DECAY_TARGET) / SLOW_DECAY_PCT,
                          math.log(DECAY_TARGET) / FAST_DECAY_PCT, B_WIDTH, dtype=F32)
    tl = min(seq, 1024)
    nhalf = seq // tl
    const = lambda *shape: pl.BlockSpec(shape, lambda i: (0,) * len(shape))
    hid = FILTER_HIDDEN
    return pl.pallas_call(
        functools.partial(_filter_body, seq=seq),
        grid=(2 * nhalf,),
        in_specs=[
            pl.BlockSpec((emb, tl), lambda i: (0, i)),
            const(hid, emb), const(hid, 1), const(hid, hid), const(hid, 1),
            pl.BlockSpec((None, n_dir, hid), lambda i: (i // nhalf, 0, 0)),
            pl.BlockSpec((None, n_dir, 1), lambda i: (i // nhalf, 0, 0)),
            const(hid, 1), const(hid, 1), const(B_WIDTH, 1),
        ],
        out_specs=pl.BlockSpec((HYENA_ORDER, B_WIDTH, tl), lambda i: (0, 0, i)),
        out_shape=jax.ShapeDtypeStruct((HYENA_ORDER, B_WIDTH, 2 * seq), F32),
        compiler_params=_cparams(("parallel",), 48),
        name="hyena_filter",
    )(feats_t, w1t, col(b1), w2t, col(b2), w3t, b3c, col(freq[0]), col(freq[1]), col(jnp.abs(deltas)))


def _dft_tables(seq):
    n = 2 * seq
    nb = n // LANES
    k1 = jnp.arange(nb, dtype=I32)[:, None]
    ang1 = (2.0 * math.pi / nb) * ((k1 * jnp.arange(nb, dtype=I32)[None, :]) % nb).astype(F32)
    f1_full = jnp.concatenate([jnp.cos(ang1), -jnp.sin(ang1)], axis=0)
    f1_half = f1_full[:, : nb // 2]
    f4_half = f1_half.T
    angt = (2.0 * math.pi / n) * ((k1 * jnp.arange(LANES, dtype=I32)[None, :]) % n).astype(F32)
    twr, twi = jnp.cos(angt), -jnp.sin(angt)
    a = jnp.arange(LANES, dtype=I32)
    ang2 = (2.0 * math.pi / LANES) * ((a[:, None] * a[None, :]) % LANES).astype(F32)
    cr, ci = jnp.cos(ang2), -jnp.sin(ang2)
    m2 = jnp.concatenate([jnp.concatenate([cr, ci], axis=1), jnp.concatenate([-ci, cr], axis=1)], axis=0)
    m3 = jnp.concatenate([jnp.concatenate([cr, -ci], axis=1), jnp.concatenate([ci, cr], axis=1)], axis=0)
    return dict(f1_full=f1_full.astype(BF16), f1_half=f1_half.astype(BF16), f4_half=f4_half.astype(BF16),
                twr=twr, twi=twi, twr_h=twr.astype(BF16), twi_h=twi.astype(BF16),
                m2=m2.astype(BF16), m3=m3.astype(BF16))


def _fwd_spectrum(pairs, f1_ref, twr_ref, twi_ref, m2_ref):
    nb = twr_ref.shape[0]
    a_all = [_dot(f1_ref[...], jnp.concatenate(xs, axis=1).astype(BF16)) for xs in pairs]
    twr, twi = twr_ref[...], twi_ref[...]
    out = []
    for a in a_all:
        a = a.astype(twr.dtype)
        lhs = []
        for d in range(2):
            ar = a[:nb, d * LANES:(d + 1) * LANES]
            ai = a[nb:, d * LANES:(d + 1) * LANES]
            lhs.append(jnp.concatenate([ar * twr - ai * twi, ar * twi + ai * twr], axis=1))
        out.append(_dot(jnp.concatenate(lhs, axis=0).astype(BF16), m2_ref[...]))
    return out


def _spec_body(k_ref, f1_ref, twr_ref, twi_ref, m2_ref, o_ref):
    nb = twr_ref.shape[0]
    inv_n = 1.0 / (nb * LANES)

    def group(it, carry):
        c0 = 2 * HY_PAIRS * it
        pairs = [[k_ref[c0 + 2 * g], k_ref[c0 + 2 * g + 1]] for g in range(HY_PAIRS)]
        for g, z in enumerate(_fwd_spectrum(pairs, f1_ref, twr_ref, twi_ref, m2_ref)):
            o_ref[c0 + 2 * g] = (z[:nb] * inv_n).astype(o_ref.dtype)
            o_ref[c0 + 2 * g + 1] = (z[nb:] * inv_n).astype(o_ref.dtype)
        return carry
    lax.fori_loop(0, k_ref.shape[0] // (2 * HY_PAIRS), group, 0)


def _filter_spectrum(kt, tb):
    orders, c, nb, _ = kt.shape
    cb = HY_CB
    const = lambda *shape: pl.BlockSpec(shape, lambda o, j: (0,) * len(shape))
    return pl.pallas_call(
        _spec_body,
        grid=(orders, c // cb),
        in_specs=[
            pl.BlockSpec((None, cb, nb, LANES), lambda o, j: (o, j, 0, 0)),
            const(2 * nb, nb), const(nb, LANES), const(nb, LANES), const(2 * LANES, 2 * LANES),
        ],
        out_specs=pl.BlockSpec((None, cb, nb, 2 * LANES), lambda o, j: (o, j, 0, 0)),
        out_shape=jax.ShapeDtypeStruct((orders, c, nb, 2 * LANES), BF16),
        compiler_params=_cparams(("parallel", "parallel"), 48),
        name="hyena_filter_spectrum",
    )(kt, tb["f1_full"], tb["twr"], tb["twi"], tb["m2"])


def _shift_rows(x, down):
    rows = x.shape[0]
    idx = lax.broadcasted_iota(I32, x.shape, 0)
    if down:
        return jnp.where(idx == 0, 0.0, pltpu.roll(x, 1, axis=0))
    return jnp.where(idx == rows - 1, 0.0, pltpu.roll(x, rows - 1, axis=0))


def _short_conv(x, taps_ref, ch):
    lane = lax.broadcasted_iota(I32, x.shape, 1)
    prev = pltpu.roll(jnp.where(lane == LANES - 1, _shift_rows(x, True), x), 1, axis=1)
    nxt = pltpu.roll(jnp.where(lane == 0, _shift_rows(x, False), x), LANES - 1, axis=1)
    return taps_ref[0, ch] * prev + taps_ref[1, ch] * x + taps_ref[2, ch] * nxt


def _conv_body(taps_ref, skip_ref, z_ref, g_ref, kf_ref, f1_ref, f4_ref, twr_ref, twi_ref, m2_ref, m3_ref,
               o_ref, *, zch, gch, conv_z):
    nb = twr_ref.shape[0]
    cb = z_ref.shape[0]
    base = pl.program_id(0) * cb

    def group(it, carry):
        c0 = 2 * HY_PAIRS * it
        chans = [[c0 + 2 * g, c0 + 2 * g + 1] for g in range(HY_PAIRS)]
        xs = [[_short_conv(z_ref[c], taps_ref, zch + base + c) if conv_z else z_ref[c] for c in pr] for pr in chans]
        zs = _fwd_spectrum(xs, f1_ref, twr_ref, twi_ref, m2_ref)
        ccs = []
        for pr, z in zip(chans, zs):
            z = z.astype(kf_ref.dtype)
            ys = []
            for d, c in enumerate(pr):
                zr = z[d * nb:(d + 1) * nb, :LANES]
                zi = z[d * nb:(d + 1) * nb, LANES:]
                kf = kf_ref[c]
                kr, ki = kf[:, :LANES], kf[:, LANES:]
                ys.append(jnp.concatenate([zr * kr - zi * ki, zr * ki + zi * kr], axis=1))
            ccs.append(_dot(jnp.concatenate(ys, axis=0).astype(BF16), m3_ref[...]))
        twr, twi = twr_ref[...], twi_ref[...]
        for pr, x2, cc in zip(chans, xs, ccs):
            cc = cc.astype(twr.dtype)
            drs, dis = [], []
            for d in range(2):
                ccr = cc[d * nb:(d + 1) * nb, :LANES]
                cci = cc[d * nb:(d + 1) * nb, LANES:]
                drs.append(ccr * twr + cci * twi)
                dis.append(cci * twr - ccr * twi)
            rhs = jnp.concatenate([jnp.concatenate(drs, axis=1), jnp.concatenate(dis, axis=1)], axis=0)
            y = _dot(f4_ref[...], rhs.astype(BF16))
            for d, c in enumerate(pr):
                gate = _short_conv(g_ref[c], taps_ref, gch + base + c)
                o_ref[c] = gate * (y[:, d * LANES:(d + 1) * LANES] + skip_ref[base + c] * x2[d])
        return carry
    lax.fori_loop(0, cb // (2 * HY_PAIRS), group, 0)


def _hyena_conv(zsrc, zch, gsrc, gch, taps, skip, kf, order, tb, conv_z):
    bsz, _, nh, _ = zsrc.shape
    nb = 2 * nh
    cb = HY_CB
    smem = pl.BlockSpec(memory_space=pltpu.SMEM)
    const = lambda *shape: pl.BlockSpec(shape, lambda j, b: (0,) * len(shape))
    zblk, gblk = zch // cb, gch // cb
    return pl.pallas_call(
        functools.partial(_conv_body, zch=zch, gch=gch, conv_z=conv_z),
        grid=(B_WIDTH // cb, bsz),
        in_specs=[
            smem, smem,
            pl.BlockSpec((None, cb, nh, LANES), lambda j, b: (b, zblk + j, 0, 0)),
            pl.BlockSpec((None, cb, nh, LANES), lambda j, b: (b, gblk + j, 0, 0)),
            pl.BlockSpec((None, cb, nb, 2 * LANES), lambda j, b: (order, j, 0, 0)),
            const(2 * nb, nh), const(nh, 2 * nb), const(nb, LANES), const(nb, LANES),
            const(2 * LANES, 2 * LANES), const(2 * LANES, 2 * LANES),
        ],
        out_specs=pl.BlockSpec((None, cb, nh, LANES), lambda j, b: (b, j, 0, 0)),
        out_shape=jax.ShapeDtypeStruct((bsz, B_WIDTH, nh, LANES), F32),
        compiler_params=_cparams(("parallel", "parallel"), 48),
        name="hyena_conv",
    )(taps, skip, zsrc, gsrc, kf, tb["f1_half"], tb["f4_half"], tb["twr_h"], tb["twi_h"], tb["m2"], tb["m3"])


def _mix_ffn_body(xa_ref, xb_ref, ya_ref, ybt_ref, wa_ref, wb_ref, g_ref, wg_ref, wu_ref, wd_ref, o_ref, h_ref,
                  *, na_tiles):
    x = jnp.where(pl.program_id(0) < na_tiles, xa_ref[...], xb_ref[...])
    x = x + _dot(ya_ref[...], wa_ref[...])
    x = x + lax.dot_general(ybt_ref[...].astype(BF16), wb_ref[...], TN_DIMS, preferred_element_type=F32)
    h_ref[...] = (_rms_scale(x) * g_ref[...]).astype(BF16)
    o_ref[...] = x

    def add_chunk(cs):
        h = h_ref[...]
        a = (jax.nn.silu(_dot(h, wg_ref[:, cs])) * _dot(h, wu_ref[:, cs])).astype(BF16)
        o_ref[...] += _dot(a, wd_ref[cs, :])

    def chunk(c, carry):
        add_chunk(pl.ds(pl.multiple_of(c * FF_CHUNK, FF_CHUNK), FF_CHUNK))
        return carry
    n_full = D_FF // FF_CHUNK
    lax.fori_loop(0, n_full, chunk, 0)
    if D_FF % FF_CHUNK:
        add_chunk(slice(n_full * FF_CHUNK, D_FF))


def _mix_ffn(xa, xb, ya, ybt, wa, wb, g, wg, wu, wd):
    na, seq, _ = xa.shape
    bsz = na + xb.shape[0]
    tm = TM_FFN
    tps = seq // tm
    na_tiles = na * tps
    flat = lambda v: v.reshape(-1, v.shape[-1])
    resident = lambda w: pl.BlockSpec(w.shape, lambda i: (0, 0), pipeline_mode=pl.Buffered(1))
    return pl.pallas_call(
        functools.partial(_mix_ffn_body, na_tiles=na_tiles),
        grid=(bsz * tps,),
        in_specs=[
            pl.BlockSpec((tm, D_MODEL), lambda i: (jnp.minimum(i, na_tiles - 1), 0)),
            pl.BlockSpec((tm, D_MODEL), lambda i: (jnp.maximum(i - na_tiles, 0), 0)),
            pl.BlockSpec((tm, ya.shape[-1]), lambda i: (i, 0)),
            pl.BlockSpec((None, ybt.shape[1], tm), lambda i: (i // tps, 0, i % tps)),
            resident(wa), resident(wb), pl.BlockSpec((1, D_MODEL), lambda i: (0, 0)),
            resident(wg), resident(wu), resident(wd),
        ],
        out_specs=pl.BlockSpec((tm, D_MODEL), lambda i: (i, 0)),
        out_shape=jax.ShapeDtypeStruct((bsz * seq, D_MODEL), F32),
        scratch_shapes=[pltpu.VMEM((tm, D_MODEL), BF16)],
        compiler_params=_cparams(("parallel",), 56),
        name="mix_ffn",
    )(flat(xa), flat(xb), flat(ya), ybt, wa, wb, g, wg, wu, wd)


def _qkv_body(x_ref, g_ref, w_ref, qg_ref, kg_ref, q_ref, k_ref, v_ref):
    h = (_rms_scale(x_ref[...]) * g_ref[...]).astype(BF16)
    nq = N_HEADS * HEAD_DIM
    nk = N_KV_HEADS * LANES
    lo = lax.broadcasted_iota(I32, (1, LANES), 1) < HEAD_DIM
    qgain = qg_ref[...] * (HEAD_DIM ** -0.5)
    wide = 2 * LANES
    for c2 in range(nq // wide):
        xw = _dot(h, w_ref[:, c2 * wide:(c2 + 1) * wide])
        for half in range(2):
            c = 2 * c2 + half
            x = xw[:, half * LANES:(half + 1) * LANES]
            x2 = x * x
            s_lo = jnp.sum(jnp.where(lo, x2, 0.0), axis=-1, keepdims=True)
            s_hi = jnp.sum(jnp.where(lo, 0.0, x2), axis=-1, keepdims=True)
            r = jnp.where(lo, lax.rsqrt(s_lo / HEAD_DIM + EPS), lax.rsqrt(s_hi / HEAD_DIM + EPS))
            q_ref[:, c * LANES:(c + 1) * LANES] = (x * r * qgain).astype(q_ref.dtype)
    for c2 in range(nk // wide):
        xw = _dot(h, w_ref[:, nq + c2 * wide:nq + (c2 + 1) * wide])
        for half in range(2):
            c = 2 * c2 + half
            x = xw[:, half * LANES:(half + 1) * LANES]
            k_ref[:, c * LANES:(c + 1) * LANES] = (_rms_scale(x) * kg_ref[...]).astype(k_ref.dtype)
    v_ref[...] = _dot(h, w_ref[:, nq + nk:]).astype(v_ref.dtype)


def _qkv_proj(x, g, w_qkv, q_g, k_g):
    bsz, seq, _ = x.shape
    tm = TM_PROJ
    nq = N_HEADS * HEAD_DIM
    nkv = N_KV_HEADS * HEAD_DIM
    dup = lambda w: jnp.tile(w.reshape(D_MODEL, N_KV_HEADS, 1, HEAD_DIM), (1, 1, 2, 1)).reshape(D_MODEL, 2 * nkv)
    w = jnp.concatenate([w_qkv[:, :nq], dup(w_qkv[:, nq:nq + nkv]), dup(w_qkv[:, nq + nkv:])], axis=1).astype(BF16)
    two = lambda v: jnp.tile(v.astype(F32), 2)[None, :]
    row = lambda width: pl.BlockSpec((None, tm, width), lambda b, i: (b, i, 0))
    const = lambda *shape: pl.BlockSpec(shape, lambda b, i: (0,) * len(shape))
    return pl.pallas_call(
        _qkv_body,
        grid=(bsz, seq // tm),
        in_specs=[row(D_MODEL), const(1, D_MODEL), const(D_MODEL, w.shape[1]), const(1, LANES), const(1, LANES)],
        out_specs=[row(nq), row(2 * nkv), row(2 * nkv)],
        out_shape=[jax.ShapeDtypeStruct((bsz, seq, nq), BF16), jax.ShapeDtypeStruct((bsz, seq, 2 * nkv), BF16),
                   jax.ShapeDtypeStruct((bsz, seq, 2 * nkv), BF16)],
        compiler_params=_cparams(("parallel", "parallel"), 48),
        name="qkv_proj",
    )(x, g, w, two(q_g), two(k_g))


def _t5_bucket(rel):
    nbk = REL_BUCKETS // 2
    max_exact = nbk // 2
    ret = jnp.where(rel > 0, nbk, 0)
    n = jnp.abs(rel)
    large = max_exact + (jnp.log(jnp.maximum(n, 1).astype(F32) / max_exact)
                         / math.log(REL_MAX_DIST / max_exact) * (nbk - max_exact)).astype(I32)
    large = jnp.minimum(large, nbk - 1)
    return ret + jnp.where(n < max_exact, n, large)


def _attn_body(sink_ref, q_ref, kp_ref, ko_ref, kn_ref, vp_ref, vo_ref, vn_ref, bm_ref, o_ref):
    kb = 3 * ATT_BLOCK
    lo = lax.broadcasted_iota(I32, (1, LANES), 1) < HEAD_DIM
    first = lax.broadcasted_iota(I32, (2 * ATT_BLOCK, 1), 0) < ATT_BLOCK
    for hk in range(N_KV_HEADS):
        ks = slice(hk * LANES, (hk + 1) * LANES)
        kk = jnp.concatenate([kp_ref[:, ks], ko_ref[:, ks], kn_ref[:, ks]], axis=0)
        vv = jnp.concatenate([vp_ref[:, ks], vo_ref[:, ks], vn_ref[:, ks]], axis=0)
        zero = jnp.zeros_like(kk)
        kz = jnp.concatenate([jnp.where(lo, kk, zero), jnp.where(lo, zero, kk)], axis=0)
        vz = jnp.concatenate([jnp.where(lo, vv, zero), jnp.where(lo, zero, vv)], axis=0)
        c0 = 2 * hk
        ql = jnp.concatenate([q_ref[:, c0 * LANES:(c0 + 1) * LANES], q_ref[:, (c0 + 1) * LANES:(c0 + 2) * LANES]],
                             axis=0)
        s_all = lax.dot_general(ql, kz, NT_DIMS, preferred_element_type=F32)
        probs, invs = [], []
        for par in range(2):
            ha, hb = GQA_GROUP * hk + par, GQA_GROUP * hk + 2 + par
            s = s_all[:, par * kb:(par + 1) * kb] + jnp.concatenate([bm_ref[ha], bm_ref[hb]], axis=0)
            sk = jnp.where(first, sink_ref[ha], sink_ref[hb])
            m = jnp.maximum(jnp.max(s, axis=-1, keepdims=True), sk)
            pexp = jnp.exp(s - m)
            invs.append(1.0 / (jnp.sum(pexp, axis=-1, keepdims=True) + jnp.exp(sk - m)))
            probs.append(pexp.astype(BF16))
        acc = _dot(jnp.concatenate(probs, axis=1), vz) * jnp.where(lo, invs[0], invs[1])
        o_ref[:, c0 * LANES:(c0 + 1) * LANES] = acc[:ATT_BLOCK].astype(o_ref.dtype)
        o_ref[:, (c0 + 1) * LANES:(c0 + 2) * LANES] = acc[ATT_BLOCK:].astype(o_ref.dtype)


def _attention(q, k2, v2, sink, rel_bias):
    bsz, seq, _ = q.shape
    nblk = seq // ATT_BLOCK
    kb = 3 * ATT_BLOCK
    rel = jnp.arange(kb)[None, :] - ATT_BLOCK - jnp.arange(ATT_BLOCK)[:, None]
    bucket = _t5_bucket(rel)
    rb = rel_bias.astype(F32)
    bias = sum(jnp.where(bucket[None] == b, rb[b][:, None, None], 0.0) for b in range(REL_BUCKETS))
    bm = jnp.where((jnp.abs(rel) <= WINDOW)[None], bias, NEG_INF)
    assert nblk >= 2
    kcol = jnp.arange(kb)[None, None, :]
    bm3 = jnp.stack([jnp.where(kcol >= ATT_BLOCK, bm, NEG_INF), bm, jnp.where(kcol < 2 * ATT_BLOCK, bm, NEG_INF)])
    which = lambda i: jnp.where(i == 0, 0, jnp.where(i == nblk - 1, 2, 1))
    kvw = k2.shape[2]
    prev = lambda b, i: (b, jnp.maximum(i - 1, 0), 0)
    own = lambda b, i: (b, i, 0)
    nxt = lambda b, i: (b, jnp.minimum(i + 1, nblk - 1), 0)
    kv_spec = lambda fn: pl.BlockSpec((None, ATT_BLOCK, kvw), fn)
    return pl.pallas_call(
        _attn_body,
        grid=(bsz, nblk),
        in_specs=[
            pl.BlockSpec(memory_space=pltpu.SMEM),
            pl.BlockSpec((None, ATT_BLOCK, N_HEADS * HEAD_DIM), own),
            kv_spec(prev), kv_spec(own), kv_spec(nxt),
            kv_spec(prev), kv_spec(own), kv_spec(nxt),
            pl.BlockSpec((None, N_HEADS, ATT_BLOCK, kb), lambda b, i: (which(i), 0, 0, 0)),
        ],
        out_specs=pl.BlockSpec((None, ATT_BLOCK, N_HEADS * HEAD_DIM), own),
        out_shape=jax.ShapeDtypeStruct((bsz, seq, N_HEADS * HEAD_DIM), BF16),
        compiler_params=_cparams(("parallel", "parallel"), 48),
        name="window_attention",
    )(sink.astype(F32), q, k2, k2, k2, v2, v2, v2, bm3)


def _router_body(x_ref, o_ref, wo_ref, g_ref, wr_ref, tri_ref, xn_ref, hp_ref, idx_ref, rank_ref, gate_ref, cnt_ref,
                 run_ref):
    @pl.when(pl.program_id(0) == 0)
    def _():
        run_ref[...] = jnp.zeros_like(run_ref)

    x = x_ref[...] + _dot(o_ref[...], wo_ref[...])
    xn_ref[...] = x
    h = _rms_scale(x) * g_ref[...]
    half = D_MODEL // 2
    bits = lax.bitcast_convert_type(h.astype(BF16).astype(F32), U32)
    hp_ref[...] = (bits[:, half:] & jnp.uint32(0xFFFF0000)) | (bits[:, :half] >> 16)

    logits = _dot3(wr_ref[...], h, NT_DIMS)
    eid = lax.broadcasted_iota(I32, logits.shape, 0)
    m1 = jnp.max(logits, axis=0, keepdims=True)
    i1 = jnp.min(jnp.where(logits == m1, eid, N_EXPERTS), axis=0, keepdims=True)
    rest = jnp.where(eid == i1, -jnp.inf, logits)
    m2 = jnp.max(rest, axis=0, keepdims=True)
    i2 = jnp.min(jnp.where(rest == m2, eid, N_EXPERTS), axis=0, keepdims=True)
    e2 = jnp.exp(m2 - m1)
    gate_ref[0:1, :] = 1.0 / (1.0 + e2)
    gate_ref[1:2, :] = e2 / (1.0 + e2)
    idx_ref[0:1, :] = i1
    idx_ref[1:2, :] = i2

    sel1 = eid == i1
    sel2 = eid == i2
    onehot = jnp.where(sel1 | sel2, 1.0, 0.0)
    before = _dot(onehot.astype(BF16), tri_ref[...]) + run_ref[:, 0:1]
    rank_ref[0:1, :] = jnp.sum(jnp.where(sel1, before, 0.0), axis=0, keepdims=True).astype(I32)
    rank_ref[1:2, :] = jnp.sum(jnp.where(sel2, before, 0.0), axis=0, keepdims=True).astype(I32)
    run_ref[...] += jnp.sum(onehot, axis=1, keepdims=True)
    cnt_ref[...] = run_ref[...].astype(I32)


def _router(x2d, o2d, w_out, g, w_router):
    t = x2d.shape[0]
    tm = TM_ROUTER
    tri = (jnp.arange(tm)[:, None] < jnp.arange(tm)[None, :]).astype(BF16)
    two = lambda dt: jax.ShapeDtypeStruct((2, t), dt)
    return pl.pallas_call(
        _router_body,
        grid=(t // tm,),
        in_specs=[
            pl.BlockSpec((tm, D_MODEL), lambda i: (i, 0)),
            pl.BlockSpec((tm, o2d.shape[1]), lambda i: (i, 0)),
            pl.BlockSpec(w_out.shape, lambda i: (0, 0)),
            pl.BlockSpec((1, D_MODEL), lambda i: (0, 0)),
            pl.BlockSpec((N_EXPERTS, D_MODEL), lambda i: (0, 0)),
            pl.BlockSpec((tm, tm), lambda i: (0, 0)),
        ],
        out_specs=[
            pl.BlockSpec((tm, D_MODEL), lambda i: (i, 0)),
            pl.BlockSpec((tm, D_MODEL // 2), lambda i: (i, 0)),
            pl.BlockSpec((2, tm), lambda i: (0, i)),
            pl.BlockSpec((2, tm), lambda i: (0, i)),
            pl.BlockSpec((2, tm), lambda i: (0, i)),
            pl.BlockSpec((N_EXPERTS, LANES), lambda i: (0, 0)),
        ],
        out_shape=[
            jax.ShapeDtypeStruct((t, D_MODEL), F32),
            jax.ShapeDtypeStruct((t, D_MODEL // 2), U32),
            two(I32), two(I32), two(F32),
            jax.ShapeDtypeStruct((N_EXPERTS, LANES), I32),
        ],
        scratch_shapes=[pltpu.VMEM((N_EXPERTS, LANES), F32)],
        compiler_params=_cparams(("arbitrary",), 48),
        name="moe_router",
    )(x2d, o2d, w_out, g, w_router.astype(F32).T, tri)


def _dispatch_body(d1_ref, d2_ref, hp_ref, init_ref, xs_ref, sem, *, rows):
    del init_ref

    def issue(k, c):
        r0 = pl.multiple_of(k * SUBLANES, SUBLANES)
        for u in range(SUBLANES):
            src = hp_ref.at[pl.ds(r0 + u, 1)]
            pltpu.make_async_copy(src, xs_ref.at[pl.ds(d1_ref[0, 0, r0 + u], 1)], sem.at[0]).start()
            pltpu.make_async_copy(src, xs_ref.at[pl.ds(d2_ref[0, 0, r0 + u], 1)], sem.at[1]).start()
        return c
    lax.fori_loop(0, rows // SUBLANES, issue, 0)
    pltpu.make_async_copy(hp_ref, xs_ref.at[pl.ds(0, rows)], sem.at[0]).wait()
    pltpu.make_async_copy(hp_ref, xs_ref.at[pl.ds(0, rows)], sem.at[1]).wait()


def _dispatch(hp, dest, n_rows):
    t, width = hp.shape
    rows = TM_DISPATCH
    idx_spec = pl.BlockSpec((1, 1, rows), lambda i: (i, 0, 0), memory_space=pltpu.SMEM)
    return pl.pallas_call(
        functools.partial(_dispatch_body, rows=rows),
        grid=(t // rows,),
        in_specs=[idx_spec, idx_spec, pl.BlockSpec((rows, width), lambda i: (i, 0)),
                  pl.BlockSpec(memory_space=pl.ANY)],
        out_specs=pl.BlockSpec(memory_space=pl.ANY),
        out_shape=jax.ShapeDtypeStruct((n_rows, width), hp.dtype),
        input_output_aliases={3: 0},
        scratch_shapes=[pltpu.SemaphoreType.DMA((2,))],
        compiler_params=_cparams(("arbitrary",), 32),
        name="moe_dispatch",
    )(dest[0].reshape(t // rows, 1, rows), dest[1].reshape(t // rows, 1, rows), hp,
      jnp.zeros((n_rows, width), hp.dtype))


def _expert_body(te_ref, nu_ref, xs_ref, wg_ref, wu_ref, wd_ref, o_ref, xb_ref, acc_ref):
    i = pl.program_id(0)
    j = pl.program_id(1)
    half = D_MODEL // 2

    @pl.when(j == 0)
    def _():
        w = xs_ref[...]
        xb_ref[:, :half] = lax.bitcast_convert_type(w << 16, F32).astype(BF16)
        xb_ref[:, half:] = lax.bitcast_convert_type(w & jnp.uint32(0xFFFF0000), F32).astype(BF16)
        acc_ref[...] = jnp.zeros_like(acc_ref)

    @pl.when(i < nu_ref[0])
    def _():
        xb = xb_ref[...]
        a = (jax.nn.silu(_dot(xb, wg_ref[...])) * _dot(xb, wu_ref[...])).astype(BF16)
        acc_ref[...] += _dot(a, wd_ref[...])

    @pl.when(j == pl.num_programs(1) - 1)
    def _():
        o_ref[...] = acc_ref[...]


def _experts(xs, tile_expert, n_used, wg, wu, wd):
    p = xs.shape[0]
    tm, tf = TM_MOE, TF_MOE
    grid_spec = pltpu.PrefetchScalarGridSpec(
        num_scalar_prefetch=2,
        grid=(p // tm, D_FF_EXPERT // tf),
        in_specs=[
            pl.BlockSpec((tm, D_MODEL // 2), lambda i, j, te, nu: (i, 0)),
            pl.BlockSpec((None, D_MODEL, tf), lambda i, j, te, nu: (te[i], 0, j)),
            pl.BlockSpec((None, D_MODEL, tf), lambda i, j, te, nu: (te[i], 0, j)),
            pl.BlockSpec((None, tf, D_MODEL), lambda i, j, te, nu: (te[i], j, 0)),
        ],
        out_specs=pl.BlockSpec((tm, D_MODEL), lambda i, j, te, nu: (i, 0)),
        scratch_shapes=[pltpu.VMEM((tm, D_MODEL), BF16), pltpu.VMEM((tm, D_MODEL), F32)],
    )
    return pl.pallas_call(
        _expert_body,
        grid_spec=grid_spec,
        out_shape=jax.ShapeDtypeStruct((p, D_MODEL), F32),
        compiler_params=_cparams(("parallel", "arbitrary"), 56),
        name="moe_experts",
    )(tile_expert, n_used, xs, wg, wu, wd)


def _combine_body(d1c_ref, d2c_ref, d1n_ref, d2n_ref, x_ref, g1_ref, g2_ref, ys_ref, oa_ref, ob_ref,
                  y1_ref, y2_ref, sem, *, rows, na_blocks):
    i = pl.program_id(0)
    slot = i % 2

    def gather(d1_ref, d2_ref, s):
        def issue(k, c):
            r0 = pl.multiple_of(k * SUBLANES, SUBLANES)
            for u in range(SUBLANES):
                pltpu.make_async_copy(ys_ref.at[pl.ds(d1_ref[0, 0, r0 + u], 1)], y1_ref.at[s, pl.ds(r0 + u, 1)],
                                      sem.at[0, s]).start()
                pltpu.make_async_copy(ys_ref.at[pl.ds(d2_ref[0, 0, r0 + u], 1)], y2_ref.at[s, pl.ds(r0 + u, 1)],
                                      sem.at[1, s]).start()
            return c
        lax.fori_loop(0, rows // SUBLANES, issue, 0)

    @pl.when(i == 0)
    def _():
        gather(d1c_ref, d2c_ref, 0)

    @pl.when(i + 1 < pl.num_programs(0))
    def _():
        gather(d1n_ref, d2n_ref, 1 - slot)

    pltpu.make_async_copy(ys_ref.at[pl.ds(0, rows)], y1_ref.at[slot], sem.at[0, slot]).wait()
    pltpu.make_async_copy(ys_ref.at[pl.ds(0, rows)], y2_ref.at[slot], sem.at[1, slot]).wait()
    val = x_ref[...] + g1_ref[...] * y1_ref[slot] + g2_ref[...] * y2_ref[slot]

    @pl.when(i < na_blocks)
    def _():
        oa_ref[...] = val

    @pl.when(i >= na_blocks)
    def _():
        ob_ref[...] = val


def _combine(x2d, ys, dest, gates, t_a):
    t = x2d.shape[0]
    rows = TM_COMBINE
    n = t // rows
    na = t_a // rows
    cur = pl.BlockSpec((1, 1, rows), lambda i: (i, 0, 0), memory_space=pltpu.SMEM)
    nxt = pl.BlockSpec((1, 1, rows), lambda i: (jnp.minimum(i + 1, n - 1), 0, 0), memory_space=pltpu.SMEM)
    gate_spec = pl.BlockSpec((rows, 1), lambda i: (i, 0))
    row_spec = pl.BlockSpec((rows, D_MODEL), lambda i: (i, 0))
    d1 = dest[0].reshape(n, 1, rows)
    d2 = dest[1].reshape(n, 1, rows)
    return pl.pallas_call(
        functools.partial(_combine_body, rows=rows, na_blocks=na),
        grid=(n,),
        in_specs=[cur, cur, nxt, nxt, row_spec, gate_spec, gate_spec, pl.BlockSpec(memory_space=pl.ANY)],
        out_specs=[pl.BlockSpec((rows, D_MODEL), lambda i: (jnp.minimum(i, na - 1), 0)),
                   pl.BlockSpec((rows, D_MODEL), lambda i: (jnp.maximum(i - na, 0), 0))],
        out_shape=[jax.ShapeDtypeStruct((t_a, D_MODEL), F32), jax.ShapeDtypeStruct((t - t_a, D_MODEL), F32)],
        scratch_shapes=[pltpu.VMEM((2, rows, D_MODEL), F32), pltpu.VMEM((2, rows, D_MODEL), F32),
                        pltpu.SemaphoreType.DMA((2, 2))],
        compiler_params=_cparams(("arbitrary",), 48),
        name="moe_combine",
    )(d1, d2, d1, d2, x2d, gates[0][:, None], gates[1][:, None], ys)


def _moe(x2d, o2d, w_out, g, w_router, wg, wu, wd, t_a):
    t = x2d.shape[0]
    tm = TM_MOE
    x2d, hp, idx, rank, gates, cnt = _router(x2d, o2d, w_out, g, w_router)
    counts = cnt[:, 0]
    tiles = (counts + tm - 1) // tm
    tile_end = jnp.cumsum(tiles)
    row_start = (tile_end - tiles) * tm
    n_tiles = (2 * t) // tm + N_EXPERTS
    eid = jnp.arange(N_EXPERTS, dtype=I32)[:, None, None]
    dest = jnp.sum(jnp.where(idx[None] == eid, row_start[:, None, None], 0), axis=0) + rank
    tile_expert = jnp.minimum(jnp.searchsorted(tile_end, jnp.arange(n_tiles), side="right"), N_EXPERTS - 1)
    xs = _dispatch(hp, dest, n_tiles * tm)
    ys = _experts(xs, tile_expert.astype(I32), tile_end[-1:].astype(I32), wg, wu, wd)
    return _combine(x2d, ys, dest, gates, t_a)


def _even_layer(xa, xb, norm_mix, norm_ffn, w_in, w_out, ln_g, ln_b, sgu_w, sgu_b, hy_conv,
                f_w1, f_b1, f_w2, f_b2, f_w3, f_b3, f_freq, hy_skip, wg, wu, wd):
    seq = xa.shape[1]
    bsz = xa.shape[0] + xb.shape[0]
    nh = seq // LANES
    sgu_bb = jnp.broadcast_to(sgu_b.astype(F32)[:, :, None], (A_GROUPS, CHUNK, LANES))
    w_in = w_in.astype(BF16)
    ya, hbt = _even_in(xa, xb, norm_mix[None, :], w_in[:, :2 * A_WIDTH], w_in[:, 2 * A_WIDTH:].T,
                       ln_g[None, :], ln_b[None, :], sgu_w.astype(BF16), sgu_bb)
    hbt = hbt.reshape(bsz, -1, nh, LANES)

    tb = _dft_tables(seq)
    kt = _hyena_kernels(seq, f_w1, f_b1, f_w2, f_b2, f_w3, f_b3, f_freq)
    kf = _filter_spectrum(kt.reshape(HYENA_ORDER, B_WIDTH, 2 * nh, LANES), tb)

    taps = hy_conv.astype(F32)
    skip = hy_skip.astype(F32)
    z1 = _hyena_conv(hbt, 0, hbt, B_WIDTH, taps, skip[0], kf, 0, tb, conv_z=True)
    ybt = _hyena_conv(z1, 0, hbt, 2 * B_WIDTH, taps, skip[1], kf, 1, tb, conv_z=False)

    w_out = w_out.astype(BF16)
    x2d = _mix_ffn(xa, xb, ya, ybt.reshape(bsz, B_WIDTH, seq), w_out[:A_WIDTH], w_out[A_WIDTH:],
                   norm_ffn[None, :], wg.astype(BF16), wu.astype(BF16), wd.astype(BF16))
    return x2d.reshape(bsz, seq, D_MODEL)


def _odd_layer(x, n_a, norm_mix, norm_ffn, w_qkv, q_g, k_g, sink, w_out, rel_bias, w_router, wg, wu, wd):
    bsz, seq, _ = x.shape
    q, k2, v2 = _qkv_proj(x, norm_mix[None, :], w_qkv, q_g, k_g)
    o = _attention(q, k2, v2, sink, rel_bias)
    ya, yb = _moe(x.reshape(bsz * seq, D_MODEL), o.reshape(bsz * seq, -1), w_out.astype(BF16), norm_ffn[None, :],
                  w_router, _to_bf16(wg), _to_bf16(wu), _to_bf16(wd), n_a * seq)
    return ya.reshape(n_a, seq, D_MODEL), yb.reshape(bsz - n_a, seq, D_MODEL)


def kernel(x_prompt, x_sample, norm_mix, norm_ffn, ev_w_in, ev_w_out, sgu_ln_g, sgu_ln_b, sgu_w, sgu_b,
           hy_conv, hy_f_w1, hy_f_b1, hy_f_w2, hy_f_b2, hy_f_w3, hy_f_b3, hy_f_freq, hy_skip,
           ffn_w_gate, ffn_w_up, ffn_w_down, at_w_qkv, at_q_norm, at_k_norm, at_sink, at_w_out,
           rel_bias, moe_router, moe_w_gate, moe_w_up, moe_w_down):
    even_p = (ev_w_in, ev_w_out, sgu_ln_g, sgu_ln_b, sgu_w, sgu_b, hy_conv, hy_f_w1, hy_f_b1, hy_f_w2,
              hy_f_b2, hy_f_w3, hy_f_b3, hy_f_freq, hy_skip, ffn_w_gate, ffn_w_up, ffn_w_down)
    odd_p = (at_w_qkv, at_q_norm, at_k_norm, at_sink, at_w_out, moe_router, moe_w_gate, moe_w_up, moe_w_down)
    assert x_prompt.shape[1:] == x_sample.shape[1:]
    n_a = x_prompt.shape[0]
    depth = norm_mix.shape[0]
    assert depth % 2 == 0, "layers come in (even, odd) pairs"
    xa, xb = x_prompt, x_sample
    for i in range(0, depth, 2):
        j = i // 2
        x = _even_layer(xa, xb, norm_mix[i], norm_ffn[i], *[p[j] for p in even_p])
        xa, xb = _odd_layer(x, n_a, norm_mix[i + 1], norm_ffn[i + 1], *[p[j] for p in odd_p[:5]], rel_bias,
                            *[p[j] for p in odd_p[5:]])
    return (xa, xb)
```

```python
import functools
import math

import jax
import jax.numpy as jnp
from jax import lax
from jax.experimental import pallas as pl
from jax.experimental.pallas import tpu as pltpu

F32 = jnp.float32
BF16 = jnp.bfloat16
U32 = jnp.uint32
I32 = jnp.int32

D_MODEL = 1024
A_GROUPS = 4
A_WIDTH = D_MODEL // 2
CHUNK = 128
B_WIDTH = D_MODEL // 2
HYENA_ORDER = 2
FILTER_DIRS = 2
FILTER_BANDS = 16
FILTER_EMB = 1 + 2 * FILTER_BANDS
FILTER_HIDDEN = 64
DECAY_TARGET = 1e-2
FAST_DECAY_PCT = 0.3
SLOW_DECAY_PCT = 1.5
HEAD_DIM = 64
N_HEADS = D_MODEL // HEAD_DIM
N_KV_HEADS = N_HEADS // 4
GQA_GROUP = N_HEADS // N_KV_HEADS
WINDOW = 128
ATT_BLOCK = 128
REL_BUCKETS = 32
REL_MAX_DIST = 128
NEG_INF = -1e30
D_FF = 2816
N_EXPERTS = 8
D_FF_EXPERT = 3584
EPS = 1e-6

LANES = 128
SUBLANES = 8
MIB = 1024 * 1024

TM_PROJ = 512
TM_FFN = 512
FF_CHUNK = 1024
TM_MOE = 512
TF_MOE = D_FF_EXPERT // 2
TM_ROUTER = 512
TM_COMBINE = 512
TM_DISPATCH = 1024
HY_CB = 32
HY_PAIRS = 8

NT_DIMS = (((1,), (1,)), ((), ()))
TN_DIMS = (((0,), (0,)), ((), ()))


def _cparams(sem, vmem_mib):
    return pltpu.CompilerParams(dimension_semantics=sem, vmem_limit_bytes=vmem_mib * MIB)


def _rms_scale(x):
    return x * lax.rsqrt(jnp.mean(x * x, axis=-1, keepdims=True) + EPS)


def _dot(a, b):
    return jnp.dot(a, b, preferred_element_type=F32)


def _dot3(a, b, dims):
    def split(x):
        hi = x.astype(BF16)
        return hi, (x - hi.astype(F32)).astype(BF16)
    (ah, al), (bh, bl) = split(a), split(b)
    mm = lambda x, y: lax.dot_general(x, y, dims, preferred_element_type=F32)
    return mm(ah, bh) + (mm(al, bh) + mm(ah, bl))


def _group_specs(xa, xb, tm):
    na, seq = xa.shape[0], xa.shape[1]
    last = seq // tm - 1
    width = xa.shape[2]
    spec_a = pl.BlockSpec((None, tm, width), lambda b, i: (jnp.minimum(b, na - 1), jnp.where(b < na, i, last), 0))
    spec_b = pl.BlockSpec((None, tm, width), lambda b, i: (jnp.maximum(b - na, 0), jnp.where(b < na, 0, i), 0))
    return spec_a, spec_b


def _group_tile(xa_ref, xb_ref, na):
    return jnp.where(pl.program_id(0) < na, xa_ref[...], xb_ref[...])


def _even_in_body(xa_ref, xb_ref, g_ref, w_ref, wht_ref, lng_ref, lnb_ref, sw_ref, sb_ref, ya_ref, hbt_ref, *, na):
    x = _group_tile(xa_ref, xb_ref, na)
    h = (_rms_scale(x) * g_ref[...]).astype(BF16)
    u = jax.nn.gelu(_dot(h, w_ref[:, 0:A_WIDTH]))
    v = jax.nn.gelu(_dot(h, w_ref[:, A_WIDTH:2 * A_WIDTH]))
    hbt_ref[...] = lax.dot_general(wht_ref[...], h, NT_DIMS, preferred_element_type=F32)
    tm = x.shape[0]
    for gi in range(A_GROUPS):
        cs = slice(gi * LANES, (gi + 1) * LANES)
        vg = v[:, cs]
        xc = vg - jnp.mean(vg, axis=-1, keepdims=True)
        var = jnp.mean(xc * xc, axis=-1, keepdims=True)
        vn = (xc * lax.rsqrt(var + EPS) * lng_ref[:, cs] + lnb_ref[:, cs]).astype(BF16)
        for c in range(tm // CHUNK):
            rs = slice(c * CHUNK, (c + 1) * CHUNK)
            mixed = _dot(sw_ref[gi], vn[rs]) + sb_ref[gi]
            ya_ref[rs, cs] = (u[rs, cs] * mixed).astype(ya_ref.dtype)


def _even_in(xa, xb, g, w_uv, w_hb_t, ln_g, ln_b, sgu_w, sgu_b):
    na, seq, _ = xa.shape
    bsz = na + xb.shape[0]
    tm = TM_PROJ
    n_hb = w_hb_t.shape[0]
    const = lambda *shape: pl.BlockSpec(shape, lambda b, i: (0,) * len(shape))
    return pl.pallas_call(
        functools.partial(_even_in_body, na=na),
        grid=(bsz, seq // tm),
        in_specs=[
            *_group_specs(xa, xb, tm),
            const(1, D_MODEL),
            const(D_MODEL, 2 * A_WIDTH),
            const(n_hb, D_MODEL),
            const(1, A_WIDTH),
            const(1, A_WIDTH),
            const(A_GROUPS, CHUNK, CHUNK),
            const(A_GROUPS, CHUNK, LANES),
        ],
        out_specs=[
            pl.BlockSpec((None, tm, A_WIDTH), lambda b, i: (b, i, 0)),
            pl.BlockSpec((None, n_hb, tm), lambda b, i: (b, 0, i)),
        ],
        out_shape=[
            jax.ShapeDtypeStruct((bsz, seq, A_WIDTH), BF16),
            jax.ShapeDtypeStruct((bsz, n_hb, seq), F32),
        ],
        compiler_params=_cparams(("parallel", "parallel"), 48),
        name="even_in",
    )(xa, xb, g, w_uv, w_hb_t, ln_g, ln_b, sgu_w, sgu_b)


def _filter_body(ft_ref, w1_ref, b1_ref, w2_ref, b2_ref, w3_ref, b3_ref, fr0_ref, fr1_ref, absd_ref, o_ref, *, seq):
    hp = lax.Precision.HIGHEST
    nn = (((1,), (0,)), ((), ()))
    ft = ft_ref[...]
    tl = ft.shape[1]
    h = jnp.sin(fr0_ref[...] * (jnp.dot(w1_ref[...], ft, precision=hp, preferred_element_type=F32) + b1_ref[...]))
    h = jnp.sin(fr1_ref[...] * (jnp.dot(w2_ref[...], h, precision=hp, preferred_element_type=F32) + b2_ref[...]))
    out = _dot3(w3_ref[...], h, nn) + b3_ref[...]
    decay = jnp.exp(-absd_ref[...] * ft[0:1, :])
    pos = pl.program_id(0) * tl + lax.broadcasted_iota(I32, (1, tl), 1)
    for o in range(HYENA_ORDER):
        o_ref[o] = jnp.where(pos == seq, 0.0, out[o * B_WIDTH:(o + 1) * B_WIDTH] * decay)


def _hyena_kernels(seq, w1, b1, w2, b2, w3, b3, freq):
    t = jnp.linspace(0.0, 1.0, seq, dtype=F32)[:, None]
    w = 2.0 * math.pi * jnp.arange(seq, dtype=F32)[:, None] / seq
    bands = jnp.linspace(1e-4, FILTER_BANDS - 1, FILTER_BANDS, dtype=F32)[None, :]
    feats = jnp.concatenate([t, jnp.cos(bands * w), jnp.sin(bands * w)], axis=-1)
    feats = jnp.concatenate([feats, feats[:1], feats[:0:-1]], axis=0)
    emb = -(-FILTER_EMB // SUBLANES) * SUBLANES
    feats_t = jnp.pad(feats, ((0, 0), (0, emb - FILTER_EMB))).T
    w1t = jnp.pad(w1.astype(F32), ((0, emb - FILTER_EMB), (0, 0))).T
    w2t = w2.astype(F32).T
    n_dir = HYENA_ORDER * B_WIDTH
    w3t = w3.astype(F32).T.reshape(FILTER_DIRS, n_dir, FILTER_HIDDEN)
    col = lambda v: v.astype(F32)[:, None]
    b3c = b3.astype(F32).reshape(FILTER_DIRS, n_dir, 1)
    deltas = jnp.linspace(math.log(DECAY_TARGET) / SLOW_DECAY_PCT,
                          math.log(DECAY_TARGET) / FAST_DECAY_PCT, B_WIDTH, dtype=F32)
    tl = min(seq, 1024)
    nhalf = seq // tl
    const = lambda *shape: pl.BlockSpec(shape, lambda i: (0,) * len(shape))
    hid = FILTER_HIDDEN
    return pl.pallas_call(
        functools.partial(_filter_body, seq=seq),
        grid=(2 * nhalf,),
        in_specs=[
            pl.BlockSpec((emb, tl), lambda i: (0, i)),
            const(hid, emb), const(hid, 1), const(hid, hid), const(hid, 1),
            pl.BlockSpec((None, n_dir, hid), lambda i: (i // nhalf, 0, 0)),
            pl.BlockSpec((None, n_dir, 1), lambda i: (i // nhalf, 0, 0)),
            const(hid, 1), const(hid, 1), const(B_WIDTH, 1),
        ],
        out_specs=pl.BlockSpec((HYENA_ORDER, B_WIDTH, tl), lambda i: (0, 0, i)),
        out_shape=jax.ShapeDtypeStruct((HYENA_ORDER, B_WIDTH, 2 * seq), F32),
        compiler_params=_cparams(("parallel",), 48),
        name="hyena_filter",
    )(feats_t, w1t, col(b1), w2t, col(b2), w3t, b3c, col(freq[0]), col(freq[1]), col(jnp.abs(deltas)))


def _dft_tables(seq):
    n = 2 * seq
    nb = n // LANES
    k1 = jnp.arange(nb, dtype=I32)[:, None]
    ang1 = (2.0 * math.pi / nb) * ((k1 * jnp.arange(nb, dtype=I32)[None, :]) % nb).astype(F32)
    f1_full = jnp.concatenate([jnp.cos(ang1), -jnp.sin(ang1)], axis=0)
    f1_half = f1_full[:, : nb // 2]
    f4_half = f1_half.T
    angt = (2.0 * math.pi / n) * ((k1 * jnp.arange(LANES, dtype=I32)[None, :]) % n).astype(F32)
    twr, twi = jnp.cos(angt), -jnp.sin(angt)
    a = jnp.arange(LANES, dtype=I32)
    ang2 = (2.0 * math.pi / LANES) * ((a[:, None] * a[None, :]) % LANES).astype(F32)
    cr, ci = jnp.cos(ang2), -jnp.sin(ang2)
    m2 = jnp.concatenate([jnp.concatenate([cr, ci], axis=1), jnp.concatenate([-ci, cr], axis=1)], axis=0)
    m3 = jnp.concatenate([jnp.concatenate([cr, -ci], axis=1), jnp.concatenate([ci, cr], axis=1)], axis=0)
    return dict(f1_full=f1_full.astype(BF16), f1_half=f1_half.astype(BF16), f4_half=f4_half.astype(BF16),
                twr=twr, twi=twi, twr_h=twr.astype(BF16), twi_h=twi.astype(BF16),
                m2=m2.astype(BF16), m3=m3.astype(BF16))


def _fwd_spectrum(pairs, f1_ref, twr_ref, twi_ref, m2_ref):
    nb = twr_ref.shape[0]
    a_all = [_dot(f1_ref[...], jnp.concatenate(xs, axis=1).astype(BF16)) for xs in pairs]
    twr, twi = twr_ref[...], twi_ref[...]
    out = []
    for a in a_all:
        a = a.astype(twr.dtype)
        lhs = []
        for d in range(2):
            ar = a[:nb, d * LANES:(d + 1) * LANES]
            ai = a[nb:, d * LANES:(d + 1) * LANES]
            lhs.append(jnp.concatenate([ar * twr - ai * twi, ar * twi + ai * twr], axis=1))
        out.append(_dot(jnp.concatenate(lhs, axis=0).astype(BF16), m2_ref[...]))
    return out


def _spec_body(k_ref, f1_ref, twr_ref, twi_ref, m2_ref, o_ref):
    nb = twr_ref.shape[0]
    inv_n = 1.0 / (nb * LANES)

    def group(it, carry):
        c0 = 2 * HY_PAIRS * it
        pairs = [[k_ref[c0 + 2 * g], k_ref[c0 + 2 * g + 1]] for g in range(HY_PAIRS)]
        for g, z in enumerate(_fwd_spectrum(pairs, f1_ref, twr_ref, twi_ref, m2_ref)):
            o_ref[c0 + 2 * g] = (z[:nb] * inv_n).astype(o_ref.dtype)
            o_ref[c0 + 2 * g + 1] = (z[nb:] * inv_n).astype(o_ref.dtype)
        return carry
    lax.fori_loop(0, k_ref.shape[0] // (2 * HY_PAIRS), group, 0)


def _filter_spectrum(kt, tb):
    orders, c, nb, _ = kt.shape
    cb = HY_CB
    const = lambda *shape: pl.BlockSpec(shape, lambda o, j: (0,) * len(shape))
    return pl.pallas_call(
        _spec_body,
        grid=(orders, c // cb),
        in_specs=[
            pl.BlockSpec((None, cb, nb, LANES), lambda o, j: (o, j, 0, 0)),
            const(2 * nb, nb), const(nb, LANES), const(nb, LANES), const(2 * LANES, 2 * LANES),
        ],
        out_specs=pl.BlockSpec((None, cb, nb, 2 * LANES), lambda o, j: (o, j, 0, 0)),
        out_shape=jax.ShapeDtypeStruct((orders, c, nb, 2 * LANES), BF16),
        compiler_params=_cparams(("parallel", "parallel"), 48),
        name="hyena_filter_spectrum",
    )(kt, tb["f1_full"], tb["twr"], tb["twi"], tb["m2"])


def _shift_rows(x, down):
    rows = x.shape[0]
    idx = lax.broadcasted_iota(I32, x.shape, 0)
    if down:
        return jnp.where(idx == 0, 0.0, pltpu.roll(x, 1, axis=0))
    return jnp.where(idx == rows - 1, 0.0, pltpu.roll(x, rows - 1, axis=0))


def _short_conv(x, taps_ref, ch):
    lane = lax.broadcasted_iota(I32, x.shape, 1)
    prev = pltpu.roll(jnp.where(lane == LANES - 1, _shift_rows(x, True), x), 1, axis=1)
    nxt = pltpu.roll(jnp.where(lane == 0, _shift_rows(x, False), x), LANES - 1, axis=1)
    return taps_ref[0, ch] * prev + taps_ref[1, ch] * x + taps_ref[2, ch] * nxt


def _conv_body(taps_ref, skip_ref, z_ref, g_ref, kf_ref, f1_ref, f4_ref, twr_ref, twi_ref, m2_ref, m3_ref,
               o_ref, *, zch, gch, conv_z):
    nb = twr_ref.shape[0]
    cb = z_ref.shape[0]
    base = pl.program_id(0) * cb

    def group(it, carry):
        c0 = 2 * HY_PAIRS * it
        chans = [[c0 + 2 * g, c0 + 2 * g + 1] for g in range(HY_PAIRS)]
        xs = [[_short_conv(z_ref[c], taps_ref, zch + base + c) if conv_z else z_ref[c] for c in pr] for pr in chans]
        zs = _fwd_spectrum(xs, f1_ref, twr_ref, twi_ref, m2_ref)
        ccs = []
        for pr, z in zip(chans, zs):
            z = z.astype(kf_ref.dtype)
            ys = []
            for d, c in enumerate(pr):
                zr = z[d * nb:(d + 1) * nb, :LANES]
                zi = z[d * nb:(d + 1) * nb, LANES:]
                kf = kf_ref[c]
                kr, ki = kf[:, :LANES], kf[:, LANES:]
                ys.append(jnp.concatenate([zr * kr - zi * ki, zr * ki + zi * kr], axis=1))
            ccs.append(_dot(jnp.concatenate(ys, axis=0).astype(BF16), m3_ref[...]))
        twr, twi = twr_ref[...], twi_ref[...]
        for pr, x2, cc in zip(chans, xs, ccs):
            cc = cc.astype(twr.dtype)
            drs, dis = [], []
            for d in range(2):
                ccr = cc[d * nb:(d + 1) * nb, :LANES]
                cci = cc[d * nb:(d + 1) * nb, LANES:]
                drs.append(ccr * twr + cci * twi)
                dis.append(cci * twr - ccr * twi)
            rhs = jnp.concatenate([jnp.concatenate(drs, axis=1), jnp.concatenate(dis, axis=1)], axis=0)
            y = _dot(f4_ref[...], rhs.astype(BF16))
            for d, c in enumerate(pr):
                gate = _short_conv(g_ref[c], taps_ref, gch + base + c)
                o_ref[c] = gate * (y[:, d * LANES:(d + 1) * LANES] + skip_ref[base + c] * x2[d])
        return carry
    lax.fori_loop(0, cb // (2 * HY_PAIRS), group, 0)


def _hyena_conv(zsrc, zch, gsrc, gch, taps, skip, kf, order, tb, conv_z):
    bsz, _, nh, _ = zsrc.shape
    nb = 2 * nh
    cb = HY_CB
    smem = pl.BlockSpec(memory_space=pltpu.SMEM)
    const = lambda *shape: pl.BlockSpec(shape, lambda j, b: (0,) * len(shape))
    zblk, gblk = zch // cb, gch // cb
    return pl.pallas_call(
        functools.partial(_conv_body, zch=zch, gch=gch, conv_z=conv_z),
        grid=(B_WIDTH // cb, bsz),
        in_specs=[
            smem, smem,
            pl.BlockSpec((None, cb, nh, LANES), lambda j, b: (b, zblk + j, 0, 0)),
            pl.BlockSpec((None, cb, nh, LANES), lambda j, b: (b, gblk + j, 0, 0)),
            pl.BlockSpec((None, cb, nb, 2 * LANES), lambda j, b: (order, j, 0, 0)),
            const(2 * nb, nh), const(nh, 2 * nb), const(nb, LANES), const(nb, LANES),
            const(2 * LANES, 2 * LANES), const(2 * LANES, 2 * LANES),
        ],
        out_specs=pl.BlockSpec((None, cb, nh, LANES), lambda j, b: (b, j, 0, 0)),
        out_shape=jax.ShapeDtypeStruct((bsz, B_WIDTH, nh, LANES), F32),
        compiler_params=_cparams(("parallel", "parallel"), 48),
        name="hyena_conv",
    )(taps, skip, zsrc, gsrc, kf, tb["f1_half"], tb["f4_half"], tb["twr_h"], tb["twi_h"], tb["m2"], tb["m3"])


def _mix_ffn_body(xa_ref, xb_ref, ya_ref, ybt_ref, wa_ref, wb_ref, g_ref, wg_ref, wu_ref, wd_ref, o_ref, h_ref,
                  *, na_tiles):
    x = jnp.where(pl.program_id(0) < na_tiles, xa_ref[...], xb_ref[...])
    x = x + _dot(ya_ref[...], wa_ref[...])
    x = x + lax.dot_general(ybt_ref[...].astype(BF16), wb_ref[...], TN_DIMS, preferred_element_type=F32)
    h_ref[...] = (_rms_scale(x) * g_ref[...]).astype(BF16)
    o_ref[...] = x

    def add_chunk(cs):
        h = h_ref[...]
        a = (jax.nn.silu(_dot(h, wg_ref[:, cs])) * _dot(h, wu_ref[:, cs])).astype(BF16)
        o_ref[...] += _dot(a, wd_ref[cs, :])

    def chunk(c, carry):
        add_chunk(pl.ds(pl.multiple_of(c * FF_CHUNK, FF_CHUNK), FF_CHUNK))
        return carry
    n_full = D_FF // FF_CHUNK
    lax.fori_loop(0, n_full, chunk, 0)
    if D_FF % FF_CHUNK:
        add_chunk(slice(n_full * FF_CHUNK, D_FF))


def _mix_ffn(xa, xb, ya, ybt, wa, wb, g, wg, wu, wd):
    na, seq, _ = xa.shape
    bsz = na + xb.shape[0]
    tm = TM_FFN
    tps = seq // tm
    na_tiles = na * tps
    flat = lambda v: v.reshape(-1, v.shape[-1])
    resident = lambda w: pl.BlockSpec(w.shape, lambda i: (0, 0), pipeline_mode=pl.Buffered(1))
    return pl.pallas_call(
        functools.partial(_mix_ffn_body, na_tiles=na_tiles),
        grid=(bsz * tps,),
        in_specs=[
            pl.BlockSpec((tm, D_MODEL), lambda i: (jnp.minimum(i, na_tiles - 1), 0)),
            pl.BlockSpec((tm, D_MODEL), lambda i: (jnp.maximum(i - na_tiles, 0), 0)),
            pl.BlockSpec((tm, ya.shape[-1]), lambda i: (i, 0)),
            pl.BlockSpec((None, ybt.shape[1], tm), lambda i: (i // tps, 0, i % tps)),
            resident(wa), resident(wb), pl.BlockSpec((1, D_MODEL), lambda i: (0, 0)),
            resident(wg), resident(wu), resident(wd),
        ],
        out_specs=pl.BlockSpec((tm, D_MODEL), lambda i: (i, 0)),
        out_shape=jax.ShapeDtypeStruct((bsz * seq, D_MODEL), F32),
        scratch_shapes=[pltpu.VMEM((tm, D_MODEL), BF16)],
        compiler_params=_cparams(("parallel",), 56),
        name="mix_ffn",
    )(flat(xa), flat(xb), flat(ya), ybt, wa, wb, g, wg, wu, wd)


def _qkv_body(x_ref, g_ref, w_ref, qg_ref, kg_ref, q_ref, k_ref, v_ref):
    h = (_rms_scale(x_ref[...]) * g_ref[...]).astype(BF16)
    nq = N_HEADS * HEAD_DIM
    nk = N_KV_HEADS * LANES
    lo = lax.broadcasted_iota(I32, (1, LANES), 1) < HEAD_DIM
    qgain = qg_ref[...] * (HEAD_DIM ** -0.5)
    wide = 2 * LANES
    for c2 in range(nq // wide):
        xw = _dot(h, w_ref[:, c2 * wide:(c2 + 1) * wide])
        for half in range(2):
            c = 2 * c2 + half
            x = xw[:, half * LANES:(half + 1) * LANES]
            x2 = x * x
            s_lo = jnp.sum(jnp.where(lo, x2, 0.0), axis=-1, keepdims=True)
            s_hi = jnp.sum(jnp.where(lo, 0.0, x2), axis=-1, keepdims=True)
            r = jnp.where(lo, lax.rsqrt(s_lo / HEAD_DIM + EPS), lax.rsqrt(s_hi / HEAD_DIM + EPS))
            q_ref[:, c * LANES:(c + 1) * LANES] = (x * r * qgain).astype(q_ref.dtype)
    for c2 in range(nk // wide):
        xw = _dot(h, w_ref[:, nq + c2 * wide:nq + (c2 + 1) * wide])
        for half in range(2):
            c = 2 * c2 + half
            x = xw[:, half * LANES:(half + 1) * LANES]
            k_ref[:, c * LANES:(c + 1) * LANES] = (_rms_scale(x) * kg_ref[...]).astype(k_ref.dtype)
    v_ref[...] = _dot(h, w_ref[:, nq + nk:]).astype(v_ref.dtype)


def _qkv_proj(x, g, w_qkv, q_g, k_g):
    bsz, seq, _ = x.shape
    tm = TM_PROJ
    nq = N_HEADS * HEAD_DIM
    nkv = N_KV_HEADS * HEAD_DIM
    dup = lambda w: jnp.tile(w.reshape(D_MODEL, N_KV_HEADS, 1, HEAD_DIM), (1, 1, 2, 1)).reshape(D_MODEL, 2 * nkv)
    w = jnp.concatenate([w_qkv[:, :nq], dup(w_qkv[:, nq:nq + nkv]), dup(w_qkv[:, nq + nkv:])], axis=1).astype(BF16)
    two = lambda v: jnp.tile(v.astype(F32), 2)[None, :]
    row = lambda width: pl.BlockSpec((None, tm, width), lambda b, i: (b, i, 0))
    const = lambda *shape: pl.BlockSpec(shape, lambda b, i: (0,) * len(shape))
    return pl.pallas_call(
        _qkv_body,
        grid=(bsz, seq // tm),
        in_specs=[row(D_MODEL), const(1, D_MODEL), const(D_MODEL, w.shape[1]), const(1, LANES), const(1, LANES)],
        out_specs=[row(nq), row(2 * nkv), row(2 * nkv)],
        out_shape=[jax.ShapeDtypeStruct((bsz, seq, nq), BF16), jax.ShapeDtypeStruct((bsz, seq, 2 * nkv), BF16),
                   jax.ShapeDtypeStruct((bsz, seq, 2 * nkv), BF16)],
        compiler_params=_cparams(("parallel", "parallel"), 48),
        name="qkv_proj",
    )(x, g, w, two(q_g), two(k_g))


def _t5_bucket(rel):
    nbk = REL_BUCKETS // 2
    max_exact = nbk // 2
    ret = jnp.where(rel > 0, nbk, 0)
    n = jnp.abs(rel)
    large = max_exact + (jnp.log(jnp.maximum(n, 1).astype(F32) / max_exact)
                         / math.log(REL_MAX_DIST / max_exact) * (nbk - max_exact)).astype(I32)
    large = jnp.minimum(large, nbk - 1)
    return ret + jnp.where(n < max_exact, n, large)


def _attn_body(sink_ref, q_ref, kp_ref, ko_ref, kn_ref, vp_ref, vo_ref, vn_ref, bm_ref, o_ref):
    kb = 3 * ATT_BLOCK
    lo = lax.broadcasted_iota(I32, (1, LANES), 1) < HEAD_DIM
    first = lax.broadcasted_iota(I32, (2 * ATT_BLOCK, 1), 0) < ATT_BLOCK
    for hk in range(N_KV_HEADS):
        ks = slice(hk * LANES, (hk + 1) * LANES)
        kk = jnp.concatenate([kp_ref[:, ks], ko_ref[:, ks], kn_ref[:, ks]], axis=0)
        vv = jnp.concatenate([vp_ref[:, ks], vo_ref[:, ks], vn_ref[:, ks]], axis=0)
        zero = jnp.zeros_like(kk)
        kz = jnp.concatenate([jnp.where(lo, kk, zero), jnp.where(lo, zero, kk)], axis=0)
        vz = jnp.concatenate([jnp.where(lo, vv, zero), jnp.where(lo, zero, vv)], axis=0)
        c0 = 2 * hk
        ql = jnp.concatenate([q_ref[:, c0 * LANES:(c0 + 1) * LANES], q_ref[:, (c0 + 1) * LANES:(c0 + 2) * LANES]],
                             axis=0)
        s_all = lax.dot_general(ql, kz, NT_DIMS, preferred_element_type=F32)
        probs, invs = [], []
        for par in range(2):
            ha, hb = GQA_GROUP * hk + par, GQA_GROUP * hk + 2 + par
            s = s_all[:, par * kb:(par + 1) * kb] + jnp.concatenate([bm_ref[ha], bm_ref[hb]], axis=0)
            sk = jnp.where(first, sink_ref[ha], sink_ref[hb])
            m = jnp.maximum(jnp.max(s, axis=-1, keepdims=True), sk)
            pexp = jnp.exp(s - m)
            invs.append(1.0 / (jnp.sum(pexp, axis=-1, keepdims=True) + jnp.exp(sk - m)))
            probs.append(pexp.astype(BF16))
        acc = _dot(jnp.concatenate(probs, axis=1), vz) * jnp.where(lo, invs[0], invs[1])
        o_ref[:, c0 * LANES:(c0 + 1) * LANES] = acc[:ATT_BLOCK].astype(o_ref.dtype)
        o_ref[:, (c0 + 1) * LANES:(c0 + 2) * LANES] = acc[ATT_BLOCK:].astype(o_ref.dtype)


def _attention(q, k2, v2, sink, rel_bias):
    bsz, seq, _ = q.shape
    nblk = seq // ATT_BLOCK
    kb = 3 * ATT_BLOCK
    rel = jnp.arange(kb)[None, :] - ATT_BLOCK - jnp.arange(ATT_BLOCK)[:, None]
    bucket = _t5_bucket(rel)
    rb = rel_bias.astype(F32)
    bias = sum(jnp.where(bucket[None] == b, rb[b][:, None, None], 0.0) for b in range(REL_BUCKETS))
    bm = jnp.where((jnp.abs(rel) <= WINDOW)[None], bias, NEG_INF)
    assert nblk >= 2
    kcol = jnp.arange(kb)[None, None, :]
    bm3 = jnp.stack([jnp.where(kcol >= ATT_BLOCK, bm, NEG_INF), bm, jnp.where(kcol < 2 * ATT_BLOCK, bm, NEG_INF)])
    which = lambda i: jnp.where(i == 0, 0, jnp.where(i == nblk - 1, 2, 1))
    kvw = k2.shape[2]
    prev = lambda b, i: (b, jnp.maximum(i - 1, 0), 0)
    own = lambda b, i: (b, i, 0)
    nxt = lambda b, i: (b, jnp.minimum(i + 1, nblk - 1), 0)
    kv_spec = lambda fn: pl.BlockSpec((None, ATT_BLOCK, kvw), fn)
    return pl.pallas_call(
        _attn_body,
        grid=(bsz, nblk),
        in_specs=[
            pl.BlockSpec(memory_space=pltpu.SMEM),
            pl.BlockSpec((None, ATT_BLOCK, N_HEADS * HEAD_DIM), own),
            kv_spec(prev), kv_spec(own), kv_spec(nxt),
            kv_spec(prev), kv_spec(own), kv_spec(nxt),
            pl.BlockSpec((None, N_HEADS, ATT_BLOCK, kb), lambda b, i: (which(i), 0, 0, 0)),
        ],
        out_specs=pl.BlockSpec((None, ATT_BLOCK, N_HEADS * HEAD_DIM), own),
        out_shape=jax.ShapeDtypeStruct((bsz, seq, N_HEADS * HEAD_DIM), BF16),
        compiler_params=_cparams(("parallel", "parallel"), 48),
        name="window_attention",
    )(sink.astype(F32), q, k2, k2, k2, v2, v2, v2, bm3)


def _router_body(x_ref, o_ref, wo_ref, g_ref, wr_ref, tri_ref, xn_ref, hp_ref, idx_ref, rank_ref, gate_ref, cnt_ref,
                 run_ref):
    @pl.when(pl.program_id(0) == 0)
    def _():
        run_ref[...] = jnp.zeros_like(run_ref)

    x = x_ref[...] + _dot(o_ref[...], wo_ref[...])
    xn_ref[...] = x
    h = _rms_scale(x) * g_ref[...]
    half = D_MODEL // 2
    bits = lax.bitcast_convert_type(h.astype(BF16).astype(F32), U32)
    hp_ref[...] = (bits[:, half:] & jnp.uint32(0xFFFF0000)) | (bits[:, :half] >> 16)

    logits = _dot3(wr_ref[...], h, NT_DIMS)
    eid = lax.broadcasted_iota(I32, logits.shape, 0)
    m1 = jnp.max(logits, axis=0, keepdims=True)
    i1 = jnp.min(jnp.where(logits == m1, eid, N_EXPERTS), axis=0, keepdims=True)
    rest = jnp.where(eid == i1, -jnp.inf, logits)
    m2 = jnp.max(rest, axis=0, keepdims=True)
    i2 = jnp.min(jnp.where(rest == m2, eid, N_EXPERTS), axis=0, keepdims=True)
    e2 = jnp.exp(m2 - m1)
    gate_ref[0:1, :] = 1.0 / (1.0 + e2)
    gate_ref[1:2, :] = e2 / (1.0 + e2)
    idx_ref[0:1, :] = i1
    idx_ref[1:2, :] = i2

    sel1 = eid == i1
    sel2 = eid == i2
    onehot = jnp.where(sel1 | sel2, 1.0, 0.0)
    before = _dot(onehot.astype(BF16), tri_ref[...]) + run_ref[:, 0:1]
    rank_ref[0:1, :] = jnp.sum(jnp.where(sel1, before, 0.0), axis=0, keepdims=True).astype(I32)
    rank_ref[1:2, :] = jnp.sum(jnp.where(sel2, before, 0.0), axis=0, keepdims=True).astype(I32)
    run_ref[...] += jnp.sum(onehot, axis=1, keepdims=True)
    cnt_ref[...] = run_ref[...].astype(I32)


def _router(x2d, o2d, w_out, g, w_router):
    t = x2d.shape[0]
    tm = TM_ROUTER
    tri = (jnp.arange(tm)[:, None] < jnp.arange(tm)[None, :]).astype(BF16)
    two = lambda dt: jax.ShapeDtypeStruct((2, t), dt)
    return pl.pallas_call(
        _router_body,
        grid=(t // tm,),
        in_specs=[
            pl.BlockSpec((tm, D_MODEL), lambda i: (i, 0)),
            pl.BlockSpec((tm, o2d.shape[1]), lambda i: (i, 0)),
            pl.BlockSpec(w_out.shape, lambda i: (0, 0)),
            pl.BlockSpec((1, D_MODEL), lambda i: (0, 0)),
            pl.BlockSpec((N_EXPERTS, D_MODEL), lambda i: (0, 0)),
            pl.BlockSpec((tm, tm), lambda i: (0, 0)),
        ],
        out_specs=[
            pl.BlockSpec((tm, D_MODEL), lambda i: (i, 0)),
            pl.BlockSpec((tm, D_MODEL // 2), lambda i: (i, 0)),
            pl.BlockSpec((2, tm), lambda i: (0, i)),
            pl.BlockSpec((2, tm), lambda i: (0, i)),
            pl.BlockSpec((2, tm), lambda i: (0, i)),
            pl.BlockSpec((N_EXPERTS, LANES), lambda i: (0, 0)),
        ],
        out_shape=[
            jax.ShapeDtypeStruct((t, D_MODEL), F32),
            jax.ShapeDtypeStruct((t, D_MODEL // 2), U32),
            two(I32), two(I32), two(F32),
            jax.ShapeDtypeStruct((N_EXPERTS, LANES), I32),
        ],
        scratch_shapes=[pltpu.VMEM((N_EXPERTS, LANES), F32)],
        compiler_params=_cparams(("arbitrary",), 48),
        name="moe_router",
    )(x2d, o2d, w_out, g, w_router.astype(F32).T, tri)


def _dispatch_body(d1_ref, d2_ref, hp_ref, init_ref, xs_ref, sem, *, rows):
    del init_ref

    def issue(k, c):
        r0 = pl.multiple_of(k * SUBLANES, SUBLANES)
        for u in range(SUBLANES):
            src = hp_ref.at[pl.ds(r0 + u, 1)]
            pltpu.make_async_copy(src, xs_ref.at[pl.ds(d1_ref[0, 0, r0 + u], 1)], sem.at[0]).start()
            pltpu.make_async_copy(src, xs_ref.at[pl.ds(d2_ref[0, 0, r0 + u], 1)], sem.at[1]).start()
        return c
    lax.fori_loop(0, rows // SUBLANES, issue, 0)
    pltpu.make_async_copy(hp_ref, xs_ref.at[pl.ds(0, rows)], sem.at[0]).wait()
    pltpu.make_async_copy(hp_ref, xs_ref.at[pl.ds(0, rows)], sem.at[1]).wait()


def _dispatch(hp, dest, n_rows):
    t, width = hp.shape
    rows = TM_DISPATCH
    idx_spec = pl.BlockSpec((1, 1, rows), lambda i: (i, 0, 0), memory_space=pltpu.SMEM)
    return pl.pallas_call(
        functools.partial(_dispatch_body, rows=rows),
        grid=(t // rows,),
        in_specs=[idx_spec, idx_spec, pl.BlockSpec((rows, width), lambda i: (i, 0)),
                  pl.BlockSpec(memory_space=pl.ANY)],
        out_specs=pl.BlockSpec(memory_space=pl.ANY),
        out_shape=jax.ShapeDtypeStruct((n_rows, width), hp.dtype),
        input_output_aliases={3: 0},
        scratch_shapes=[pltpu.SemaphoreType.DMA((2,))],
        compiler_params=_cparams(("arbitrary",), 32),
        name="moe_dispatch",
    )(dest[0].reshape(t // rows, 1, rows), dest[1].reshape(t // rows, 1, rows), hp,
      jnp.zeros((n_rows, width), hp.dtype))


def _expert_body(te_ref, nu_ref, xs_ref, wg_ref, wu_ref, wd_ref, o_ref, xb_ref, acc_ref):
    i = pl.program_id(0)
    j = pl.program_id(1)
    half = D_MODEL // 2

    @pl.when(j == 0)
    def _():
        w = xs_ref[...]
        xb_ref[:, :half] = lax.bitcast_convert_type(w << 16, F32).astype(BF16)
        xb_ref[:, half:] = lax.bitcast_convert_type(w & jnp.uint32(0xFFFF0000), F32).astype(BF16)
        acc_ref[...] = jnp.zeros_like(acc_ref)

    @pl.when(i < nu_ref[0])
    def _():
        xb = xb_ref[...]
        a = (jax.nn.silu(_dot(xb, wg_ref[...])) * _dot(xb, wu_ref[...])).astype(BF16)
        acc_ref[...] += _dot(a, wd_ref[...])

    @pl.when(j == pl.num_programs(1) - 1)
    def _():
        o_ref[...] = acc_ref[...]


def _experts(xs, tile_expert, n_used, wg, wu, wd):
    p = xs.shape[0]
    tm, tf = TM_MOE, TF_MOE
    grid_spec = pltpu.PrefetchScalarGridSpec(
        num_scalar_prefetch=2,
        grid=(p // tm, D_FF_EXPERT // tf),
        in_specs=[
            pl.BlockSpec((tm, D_MODEL // 2), lambda i, j, te, nu: (i, 0)),
            pl.BlockSpec((None, D_MODEL, tf), lambda i, j, te, nu: (te[i], 0, j)),
            pl.BlockSpec((None, D_MODEL, tf), lambda i, j, te, nu: (te[i], 0, j)),
            pl.BlockSpec((None, tf, D_MODEL), lambda i, j, te, nu: (te[i], j, 0)),
        ],
        out_specs=pl.BlockSpec((tm, D_MODEL), lambda i, j, te, nu: (i, 0)),
        scratch_shapes=[pltpu.VMEM((tm, D_MODEL), BF16), pltpu.VMEM((tm, D_MODEL), F32)],
    )
    return pl.pallas_call(
        _expert_body,
        grid_spec=grid_spec,
        out_shape=jax.ShapeDtypeStruct((p, D_MODEL), F32),
        compiler_params=_cparams(("parallel", "arbitrary"), 56),
        name="moe_experts",
    )(tile_expert, n_used, xs, wg, wu, wd)


def _combine_body(d1c_ref, d2c_ref, d1n_ref, d2n_ref, x_ref, g1_ref, g2_ref, ys_ref, oa_ref, ob_ref,
                  y1_ref, y2_ref, sem, *, rows, na_blocks):
    i = pl.program_id(0)
    slot = i % 2

    def gather(d1_ref, d2_ref, s):
        def issue(k, c):
            r0 = pl.multiple_of(k * SUBLANES, SUBLANES)
            for u in range(SUBLANES):
                pltpu.make_async_copy(ys_ref.at[pl.ds(d1_ref[0, 0, r0 + u], 1)], y1_ref.at[s, pl.ds(r0 + u, 1)],
                                      sem.at[0, s]).start()
                pltpu.make_async_copy(ys_ref.at[pl.ds(d2_ref[0, 0, r0 + u], 1)], y2_ref.at[s, pl.ds(r0 + u, 1)],
                                      sem.at[1, s]).start()
            return c
        lax.fori_loop(0, rows // SUBLANES, issue, 0)

    @pl.when(i == 0)
    def _():
        gather(d1c_ref, d2c_ref, 0)

    @pl.when(i + 1 < pl.num_programs(0))
    def _():
        gather(d1n_ref, d2n_ref, 1 - slot)

    pltpu.make_async_copy(ys_ref.at[pl.ds(0, rows)], y1_ref.at[slot], sem.at[0, slot]).wait()
    pltpu.make_async_copy(ys_ref.at[pl.ds(0, rows)], y2_ref.at[slot], sem.at[1, slot]).wait()
    val = x_ref[...] + g1_ref[...] * y1_ref[slot] + g2_ref[...] * y2_ref[slot]

    @pl.when(i < na_blocks)
    def _():
        oa_ref[...] = val

    @pl.when(i >= na_blocks)
    def _():
        ob_ref[...] = val


def _combine(x2d, ys, dest, gates, t_a):
    t = x2d.shape[0]
    rows = TM_COMBINE
    n = t // rows
    na = t_a // rows
    cur = pl.BlockSpec((1, 1, rows), lambda i: (i, 0, 0), memory_space=pltpu.SMEM)
    nxt = pl.BlockSpec((1, 1, rows), lambda i: (jnp.minimum(i + 1, n - 1), 0, 0), memory_space=pltpu.SMEM)
    gate_spec = pl.BlockSpec((rows, 1), lambda i: (i, 0))
    row_spec = pl.BlockSpec((rows, D_MODEL), lambda i: (i, 0))
    d1 = dest[0].reshape(n, 1, rows)
    d2 = dest[1].reshape(n, 1, rows)
    return pl.pallas_call(
        functools.partial(_combine_body, rows=rows, na_blocks=na),
        grid=(n,),
        in_specs=[cur, cur, nxt, nxt, row_spec, gate_spec, gate_spec, pl.BlockSpec(memory_space=pl.ANY)],
        out_specs=[pl.BlockSpec((rows, D_MODEL), lambda i: (jnp.minimum(i, na - 1), 0)),
                   pl.BlockSpec((rows, D_MODEL), lambda i: (jnp.maximum(i - na, 0), 0))],
        out_shape=[jax.ShapeDtypeStruct((t_a, D_MODEL), F32), jax.ShapeDtypeStruct((t - t_a, D_MODEL), F32)],
        scratch_shapes=[pltpu.VMEM((2, rows, D_MODEL), F32), pltpu.VMEM((2, rows, D_MODEL), F32),
                        pltpu.SemaphoreType.DMA((2, 2))],
        compiler_params=_cparams(("arbitrary",), 48),
        name="moe_combine",
    )(d1, d2, d1, d2, x2d, gates[0][:, None], gates[1][:, None], ys)


def _moe(x2d, o2d, w_out, g, w_router, wg, wu, wd, t_a):
    t = x2d.shape[0]
    tm = TM_MOE
    x2d, hp, idx, rank, gates, cnt = _router(x2d, o2d, w_out, g, w_router)
    counts = cnt[:, 0]
    tiles = (counts + tm - 1) // tm
    tile_end = jnp.cumsum(tiles)
    row_start = (tile_end - tiles) * tm
    n_tiles = (2 * t) // tm + N_EXPERTS
    eid = jnp.arange(N_EXPERTS, dtype=I32)[:, None, None]
    dest = jnp.sum(jnp.where(idx[None] == eid, row_start[:, None, None], 0), axis=0) + rank
    tile_expert = jnp.minimum(jnp.searchsorted(tile_end, jnp.arange(n_tiles), side="right"), N_EXPERTS - 1)
    xs = _dispatch(hp, dest, n_tiles * tm)
    ys = _experts(xs, tile_expert.astype(I32), tile_end[-1:].astype(I32), wg, wu, wd)
    return _combine(x2d, ys, dest, gates, t_a)


def _even_layer(xa, xb, norm_mix, norm_ffn, w_in, w_out, ln_g, ln_b, sgu_w, sgu_b, hy_conv,
                f_w1, f_b1, f_w2, f_b2, f_w3, f_b3, f_freq, hy_skip, wg, wu, wd):
    seq = xa.shape[1]
    bsz = xa.shape[0] + xb.shape[0]
    nh = seq // LANES
    sgu_bb = jnp.broadcast_to(sgu_b.astype(F32)[:, :, None], (A_GROUPS, CHUNK, LANES))
    w_in = w_in.astype(BF16)
    ya, hbt = _even_in(xa, xb, norm_mix[None, :], w_in[:, :2 * A_WIDTH], w_in[:, 2 * A_WIDTH:].T,
                       ln_g[None, :], ln_b[None, :], sgu_w.astype(BF16), sgu_bb)
    hbt = hbt.reshape(bsz, -1, nh, LANES)

    tb = _dft_tables(seq)
    kt = _hyena_kernels(seq, f_w1, f_b1, f_w2, f_b2, f_w3, f_b3, f_freq)
    kf = _filter_spectrum(kt.reshape(HYENA_ORDER, B_WIDTH, 2 * nh, LANES), tb)

    taps = hy_conv.astype(F32)
    skip = hy_skip.astype(F32)
    z1 = _hyena_conv(hbt, 0, hbt, B_WIDTH, taps, skip[0], kf, 0, tb, conv_z=True)
    ybt = _hyena_conv(z1, 0, hbt, 2 * B_WIDTH, taps, skip[1], kf, 1, tb, conv_z=False)

    w_out = w_out.astype(BF16)
    x2d = _mix_ffn(xa, xb, ya, ybt.reshape(bsz, B_WIDTH, seq), w_out[:A_WIDTH], w_out[A_WIDTH:],
                   norm_ffn[None, :], wg.astype(BF16), wu.astype(BF16), wd.astype(BF16))
    return x2d.reshape(bsz, seq, D_MODEL)


def _odd_layer(x, n_a, norm_mix, norm_ffn, w_qkv, q_g, k_g, sink, w_out, rel_bias, w_router, wg, wu, wd):
    bsz, seq, _ = x.shape
    q, k2, v2 = _qkv_proj(x, norm_mix[None, :], w_qkv, q_g, k_g)
    o = _attention(q, k2, v2, sink, rel_bias)
    ya, yb = _moe(x.reshape(bsz * seq, D_MODEL), o.reshape(bsz * seq, -1), w_out.astype(BF16), norm_ffn[None, :],
                  w_router, wg.astype(BF16), wu.astype(BF16), wd.astype(BF16), n_a * seq)
    return ya.reshape(n_a, seq, D_MODEL), yb.reshape(bsz - n_a, seq, D_MODEL)


def kernel(x_prompt, x_sample, norm_mix, norm_ffn, ev_w_in, ev_w_out, sgu_ln_g, sgu_ln_b, sgu_w, sgu_b,
           hy_conv, hy_f_w1, hy_f_b1, hy_f_w2, hy_f_b2, hy_f_w3, hy_f_b3, hy_f_freq, hy_skip,
           ffn_w_gate, ffn_w_up, ffn_w_down, at_w_qkv, at_q_norm, at_k_norm, at_sink, at_w_out,
           rel_bias, moe_router, moe_w_gate, moe_w_up, moe_w_down):
    even_p = (ev_w_in, ev_w_out, sgu_ln_g, sgu_ln_b, sgu_w, sgu_b, hy_conv, hy_f_w1, hy_f_b1, hy_f_w2,
              hy_f_b2, hy_f_w3, hy_f_b3, hy_f_freq, hy_skip, ffn_w_gate, ffn_w_up, ffn_w_down)
    odd_p = (at_w_qkv, at_q_norm, at_k_norm, at_sink, at_w_out, moe_router, moe_w_gate, moe_w_up, moe_w_down)
    assert x_prompt.shape[1:] == x_sample.shape[1:]
    n_a = x_prompt.shape[0]
    depth = norm_mix.shape[0]
    assert depth % 2 == 0, "layers come in (even, odd) pairs"
    xa, xb = x_prompt, x_sample
    for i in range(0, depth, 2):
        j = i // 2
        x = _even_layer(xa, xb, norm_mix[i], norm_ffn[i], *[p[j] for p in even_p])
        xa, xb = _odd_layer(x, n_a, norm_mix[i + 1], norm_ffn[i + 1], *[p[j] for p in odd_p[:5]], rel_bias,
                            *[p[j] for p in odd_p[5:]])
    return (xa, xb)
```

```python
import functools
import math

import jax
import jax.numpy as jnp
from jax import lax
from jax.experimental import pallas as pl
from jax.experimental.pallas import tpu as pltpu

F32 = jnp.float32
BF16 = jnp.bfloat16
U32 = jnp.uint32
I32 = jnp.int32

D_MODEL = 1024
A_GROUPS = 4
A_WIDTH = D_MODEL // 2
CHUNK = 128
B_WIDTH = D_MODEL // 2
HYENA_ORDER = 2
FILTER_DIRS = 2
FILTER_BANDS = 16
FILTER_EMB = 1 + 2 * FILTER_BANDS
FILTER_HIDDEN = 64
DECAY_TARGET = 1e-2
FAST_DECAY_PCT = 0.3
SLOW_DECAY_PCT = 1.5
HEAD_DIM = 64
N_HEADS = D_MODEL // HEAD_DIM
N_KV_HEADS = N_HEADS // 4
GQA_GROUP = N_HEADS // N_KV_HEADS
WINDOW = 128
ATT_BLOCK = 128
REL_BUCKETS = 32
REL_MAX_DIST = 128
NEG_INF = -1e30
D_FF = 2816
N_EXPERTS = 8
D_FF_EXPERT = 3584
EPS = 1e-6

LANES = 128
SUBLANES = 8
MIB = 1024 * 1024

TM_PROJ = 512
TM_QKV = 1024
TM_FFN = 512
FF_CHUNK = 1024
TM_MOE = 512
TF_MOE = D_FF_EXPERT // 2
TM_ROUTER = 512
TM_COMBINE = 512
TM_DISPATCH = 1024
HY_CB = 64
HY_PAIRS = 8

NT_DIMS = (((1,), (1,)), ((), ()))
TN_DIMS = (((0,), (0,)), ((), ()))


def _cparams(sem, vmem_mib):
    return pltpu.CompilerParams(dimension_semantics=sem, vmem_limit_bytes=vmem_mib * MIB)


def _rms_scale(x):
    return x * lax.rsqrt(jnp.mean(x * x, axis=-1, keepdims=True) + EPS)


def _dot(a, b):
    return jnp.dot(a, b, preferred_element_type=F32)


def _dot3(a, b, dims):
    def split(x):
        hi = x.astype(BF16)
        return hi, (x - hi.astype(F32)).astype(BF16)
    (ah, al), (bh, bl) = split(a), split(b)
    mm = lambda x, y: lax.dot_general(x, y, dims, preferred_element_type=F32)
    return mm(ah, bh) + (mm(al, bh) + mm(ah, bl))


def _group_specs(xa, xb, tm):
    na, seq = xa.shape[0], xa.shape[1]
    last = seq // tm - 1
    width = xa.shape[2]
    spec_a = pl.BlockSpec((None, tm, width), lambda b, i: (jnp.minimum(b, na - 1), jnp.where(b < na, i, last), 0))
    spec_b = pl.BlockSpec((None, tm, width), lambda b, i: (jnp.maximum(b - na, 0), jnp.where(b < na, 0, i), 0))
    return spec_a, spec_b


def _group_tile(xa_ref, xb_ref, na):
    return jnp.where(pl.program_id(0) < na, xa_ref[...], xb_ref[...])


def _even_in_body(xa_ref, xb_ref, g_ref, w_ref, wht_ref, lng_ref, lnb_ref, sw_ref, sb_ref, ya_ref, hbt_ref, *, na):
    x = _group_tile(xa_ref, xb_ref, na)
    h = (_rms_scale(x) * g_ref[...]).astype(BF16)
    u = jax.nn.gelu(_dot(h, w_ref[:, 0:A_WIDTH]))
    v = jax.nn.gelu(_dot(h, w_ref[:, A_WIDTH:2 * A_WIDTH]))
    hbt_ref[...] = lax.dot_general(wht_ref[...], h, NT_DIMS, preferred_element_type=F32)
    tm = x.shape[0]
    for gi in range(A_GROUPS):
        cs = slice(gi * LANES, (gi + 1) * LANES)
        vg = v[:, cs]
        xc = vg - jnp.mean(vg, axis=-1, keepdims=True)
        var = jnp.mean(xc * xc, axis=-1, keepdims=True)
        vn = (xc * lax.rsqrt(var + EPS) * lng_ref[:, cs] + lnb_ref[:, cs]).astype(BF16)
        for c in range(tm // CHUNK):
            rs = slice(c * CHUNK, (c + 1) * CHUNK)
            mixed = _dot(sw_ref[gi], vn[rs]) + sb_ref[gi]
            ya_ref[rs, cs] = (u[rs, cs] * mixed).astype(ya_ref.dtype)


def _even_in(xa, xb, g, w_uv, w_hb_t, ln_g, ln_b, sgu_w, sgu_b):
    na, seq, _ = xa.shape
    bsz = na + xb.shape[0]
    tm = TM_PROJ
    n_hb = w_hb_t.shape[0]
    const = lambda *shape: pl.BlockSpec(shape, lambda b, i: (0,) * len(shape))
    return pl.pallas_call(
        functools.partial(_even_in_body, na=na),
        grid=(bsz, seq // tm),
        in_specs=[
            *_group_specs(xa, xb, tm),
            const(1, D_MODEL),
            const(D_MODEL, 2 * A_WIDTH),
            const(n_hb, D_MODEL),
            const(1, A_WIDTH),
            const(1, A_WIDTH),
            const(A_GROUPS, CHUNK, CHUNK),
            const(A_GROUPS, CHUNK, LANES),
        ],
        out_specs=[
            pl.BlockSpec((None, tm, A_WIDTH), lambda b, i: (b, i, 0)),
            pl.BlockSpec((None, n_hb, tm), lambda b, i: (b, 0, i)),
        ],
        out_shape=[
            jax.ShapeDtypeStruct((bsz, seq, A_WIDTH), BF16),
            jax.ShapeDtypeStruct((bsz, n_hb, seq), F32),
        ],
        compiler_params=_cparams(("parallel", "parallel"), 48),
        name="even_in",
    )(xa, xb, g, w_uv, w_hb_t, ln_g, ln_b, sgu_w, sgu_b)


def _filter_body(ft_ref, w1_ref, b1_ref, w2_ref, b2_ref, w3_ref, b3_ref, fr0_ref, fr1_ref, absd_ref, o_ref, *, seq):
    hp = lax.Precision.HIGHEST
    nn = (((1,), (0,)), ((), ()))
    ft = ft_ref[...]
    tl = ft.shape[1]
    h = jnp.sin(fr0_ref[...] * (jnp.dot(w1_ref[...], ft, precision=hp, preferred_element_type=F32) + b1_ref[...]))
    h = jnp.sin(fr1_ref[...] * (jnp.dot(w2_ref[...], h, precision=hp, preferred_element_type=F32) + b2_ref[...]))
    out = _dot3(w3_ref[...], h, nn) + b3_ref[...]
    decay = jnp.exp(-absd_ref[...] * ft[0:1, :])
    pos = pl.program_id(0) * tl + lax.broadcasted_iota(I32, (1, tl), 1)
    for o in range(HYENA_ORDER):
        o_ref[o] = jnp.where(pos == seq, 0.0, out[o * B_WIDTH:(o + 1) * B_WIDTH] * decay)


def _hyena_kernels(seq, w1, b1, w2, b2, w3, b3, freq):
    t = jnp.linspace(0.0, 1.0, seq, dtype=F32)[:, None]
    w = 2.0 * math.pi * jnp.arange(seq, dtype=F32)[:, None] / seq
    bands = jnp.linspace(1e-4, FILTER_BANDS - 1, FILTER_BANDS, dtype=F32)[None, :]
    feats = jnp.concatenate([t, jnp.cos(bands * w), jnp.sin(bands * w)], axis=-1)
    feats = jnp.concatenate([feats, feats[:1], feats[:0:-1]], axis=0)
    emb = -(-FILTER_EMB // SUBLANES) * SUBLANES
    feats_t = jnp.pad(feats, ((0, 0), (0, emb - FILTER_EMB))).T
    w1t = jnp.pad(w1.astype(F32), ((0, emb - FILTER_EMB), (0, 0))).T
    w2t = w2.astype(F32).T
    n_dir = HYENA_ORDER * B_WIDTH
    w3t = w3.astype(F32).T.reshape(FILTER_DIRS, n_dir, FILTER_HIDDEN)
    col = lambda v: v.astype(F32)[:, None]
    b3c = b3.astype(F32).reshape(FILTER_DIRS, n_dir, 1)
    deltas = jnp.linspace(math.log(DECAY_TARGET) / SLOW_DECAY_PCT,
                          math.log(DECAY_TARGET) / FAST_DECAY_PCT, B_WIDTH, dtype=F32)
    tl = min(seq, 1024)
    nhalf = seq // tl
    const = lambda *shape: pl.BlockSpec(shape, lambda i: (0,) * len(shape))
    hid = FILTER_HIDDEN
    return pl.pallas_call(
        functools.partial(_filter_body, seq=seq),
        grid=(2 * nhalf,),
        in_specs=[
            pl.BlockSpec((emb, tl), lambda i: (0, i)),
            const(hid, emb), const(hid, 1), const(hid, hid), const(hid, 1),
            pl.BlockSpec((None, n_dir, hid), lambda i: (i // nhalf, 0, 0)),
            pl.BlockSpec((None, n_dir, 1), lambda i: (i // nhalf, 0, 0)),
            const(hid, 1), const(hid, 1), const(B_WIDTH, 1),
        ],
        out_specs=pl.BlockSpec((HYENA_ORDER, B_WIDTH, tl), lambda i: (0, 0, i)),
        out_shape=jax.ShapeDtypeStruct((HYENA_ORDER, B_WIDTH, 2 * seq), F32),
        compiler_params=_cparams(("parallel",), 48),
        name="hyena_filter",
    )(feats_t, w1t, col(b1), w2t, col(b2), w3t, b3c, col(freq[0]), col(freq[1]), col(jnp.abs(deltas)))


def _dft_tables(seq):
    n = 2 * seq
    nb = n // LANES
    k1 = jnp.arange(nb, dtype=I32)[:, None]
    ang1 = (2.0 * math.pi / nb) * ((k1 * jnp.arange(nb, dtype=I32)[None, :]) % nb).astype(F32)
    f1_full = jnp.concatenate([jnp.cos(ang1), -jnp.sin(ang1)], axis=0)
    f1_half = f1_full[:, : nb // 2]
    f4_half = f1_half.T
    angt = (2.0 * math.pi / n) * ((k1 * jnp.arange(LANES, dtype=I32)[None, :]) % n).astype(F32)
    twr, twi = jnp.cos(angt), -jnp.sin(angt)
    a = jnp.arange(LANES, dtype=I32)
    ang2 = (2.0 * math.pi / LANES) * ((a[:, None] * a[None, :]) % LANES).astype(F32)
    cr, ci = jnp.cos(ang2), -jnp.sin(ang2)
    m2 = jnp.concatenate([jnp.concatenate([cr, ci], axis=1), jnp.concatenate([-ci, cr], axis=1)], axis=0)
    m3 = jnp.concatenate([jnp.concatenate([cr, -ci], axis=1), jnp.concatenate([ci, cr], axis=1)], axis=0)
    return dict(f1_full=f1_full.astype(BF16), f1_half=f1_half.astype(BF16), f4_half=f4_half.astype(BF16),
                twr=twr, twi=twi, twr_h=twr.astype(BF16), twi_h=twi.astype(BF16),
                m2=m2.astype(BF16), m3=m3.astype(BF16))


def _fwd_spectrum(pairs, f1_ref, twr_ref, twi_ref, m2_ref):
    nb = twr_ref.shape[0]
    a_all = [_dot(f1_ref[...], jnp.concatenate(xs, axis=1).astype(BF16)) for xs in pairs]
    twr, twi = twr_ref[...], twi_ref[...]
    out = []
    for a in a_all:
        a = a.astype(twr.dtype)
        lhs = []
        for d in range(2):
            ar = a[:nb, d * LANES:(d + 1) * LANES]
            ai = a[nb:, d * LANES:(d + 1) * LANES]
            lhs.append(jnp.concatenate([ar * twr - ai * twi, ar * twi + ai * twr], axis=1))
        out.append(_dot(jnp.concatenate(lhs, axis=0).astype(BF16), m2_ref[...]))
    return out


def _spec_body(k_ref, f1_ref, twr_ref, twi_ref, m2_ref, o_ref):
    nb = twr_ref.shape[0]
    inv_n = 1.0 / (nb * LANES)

    def group(it, carry):
        c0 = 2 * HY_PAIRS * it
        pairs = [[k_ref[c0 + 2 * g], k_ref[c0 + 2 * g + 1]] for g in range(HY_PAIRS)]
        for g, z in enumerate(_fwd_spectrum(pairs, f1_ref, twr_ref, twi_ref, m2_ref)):
            o_ref[c0 + 2 * g] = (z[:nb] * inv_n).astype(o_ref.dtype)
            o_ref[c0 + 2 * g + 1] = (z[nb:] * inv_n).astype(o_ref.dtype)
        return carry
    lax.fori_loop(0, k_ref.shape[0] // (2 * HY_PAIRS), group, 0)


def _filter_spectrum(kt, tb):
    orders, c, nb, _ = kt.shape
    cb = HY_CB
    const = lambda *shape: pl.BlockSpec(shape, lambda o, j: (0,) * len(shape))
    return pl.pallas_call(
        _spec_body,
        grid=(orders, c // cb),
        in_specs=[
            pl.BlockSpec((None, cb, nb, LANES), lambda o, j: (o, j, 0, 0)),
            const(2 * nb, nb), const(nb, LANES), const(nb, LANES), const(2 * LANES, 2 * LANES),
        ],
        out_specs=pl.BlockSpec((None, cb, nb, 2 * LANES), lambda o, j: (o, j, 0, 0)),
        out_shape=jax.ShapeDtypeStruct((orders, c, nb, 2 * LANES), BF16),
        compiler_params=_cparams(("parallel", "parallel"), 48),
        name="hyena_filter_spectrum",
    )(kt, tb["f1_full"], tb["twr"], tb["twi"], tb["m2"])


def _shift_rows(x, down):
    rows = x.shape[0]
    idx = lax.broadcasted_iota(I32, x.shape, 0)
    if down:
        return jnp.where(idx == 0, 0.0, pltpu.roll(x, 1, axis=0))
    return jnp.where(idx == rows - 1, 0.0, pltpu.roll(x, rows - 1, axis=0))


def _short_conv(x, taps_ref, ch):
    lane = lax.broadcasted_iota(I32, x.shape, 1)
    prev = pltpu.roll(jnp.where(lane == LANES - 1, _shift_rows(x, True), x), 1, axis=1)
    nxt = pltpu.roll(jnp.where(lane == 0, _shift_rows(x, False), x), LANES - 1, axis=1)
    return taps_ref[0, ch] * prev + taps_ref[1, ch] * x + taps_ref[2, ch] * nxt


def _conv_body(taps_ref, skip_ref, z_ref, g_ref, kf_ref, f1_ref, f4_ref, twr_ref, twi_ref, m2_ref, m3_ref,
               o_ref, *, zch, gch, conv_z):
    nb = twr_ref.shape[0]
    cb = z_ref.shape[0]
    base = pl.program_id(0) * cb

    def group(it, carry):
        c0 = 2 * HY_PAIRS * it
        chans = [[c0 + 2 * g, c0 + 2 * g + 1] for g in range(HY_PAIRS)]
        xs = [[_short_conv(z_ref[c], taps_ref, zch + base + c) if conv_z else z_ref[c] for c in pr] for pr in chans]
        zs = _fwd_spectrum(xs, f1_ref, twr_ref, twi_ref, m2_ref)
        ccs = []
        for pr, z in zip(chans, zs):
            z = z.astype(kf_ref.dtype)
            ys = []
            for d, c in enumerate(pr):
                zr = z[d * nb:(d + 1) * nb, :LANES]
                zi = z[d * nb:(d + 1) * nb, LANES:]
                kf = kf_ref[c]
                kr, ki = kf[:, :LANES], kf[:, LANES:]
                ys.append(jnp.concatenate([zr * kr - zi * ki, zr * ki + zi * kr], axis=1))
            ccs.append(_dot(jnp.concatenate(ys, axis=0).astype(BF16), m3_ref[...]))
        twr, twi = twr_ref[...], twi_ref[...]
        for pr, x2, cc in zip(chans, xs, ccs):
            cc = cc.astype(twr.dtype)
            drs, dis = [], []
            for d in range(2):
                ccr = cc[d * nb:(d + 1) * nb, :LANES]
                cci = cc[d * nb:(d + 1) * nb, LANES:]
                drs.append(ccr * twr + cci * twi)
                dis.append(cci * twr - ccr * twi)
            rhs = jnp.concatenate([jnp.concatenate(drs, axis=1), jnp.concatenate(dis, axis=1)], axis=0)
            y = _dot(f4_ref[...], rhs.astype(BF16))
            for d, c in enumerate(pr):
                gate = _short_conv(g_ref[c], taps_ref, gch + base + c)
                o_ref[c] = gate * (y[:, d * LANES:(d + 1) * LANES] + skip_ref[base + c] * x2[d])
        return carry
    lax.fori_loop(0, cb // (2 * HY_PAIRS), group, 0)


def _hyena_conv(zsrc, zch, gsrc, gch, taps, skip, kf, order, tb, conv_z):
    bsz, _, nh, _ = zsrc.shape
    nb = 2 * nh
    cb = HY_CB
    smem = pl.BlockSpec(memory_space=pltpu.SMEM)
    const = lambda *shape: pl.BlockSpec(shape, lambda j, b: (0,) * len(shape))
    zblk, gblk = zch // cb, gch // cb
    return pl.pallas_call(
        functools.partial(_conv_body, zch=zch, gch=gch, conv_z=conv_z),
        grid=(B_WIDTH // cb, bsz),
        in_specs=[
            smem, smem,
            pl.BlockSpec((None, cb, nh, LANES), lambda j, b: (b, zblk + j, 0, 0)),
            pl.BlockSpec((None, cb, nh, LANES), lambda j, b: (b, gblk + j, 0, 0)),
            pl.BlockSpec((None, cb, nb, 2 * LANES), lambda j, b: (order, j, 0, 0)),
            const(2 * nb, nh), const(nh, 2 * nb), const(nb, LANES), const(nb, LANES),
            const(2 * LANES, 2 * LANES), const(2 * LANES, 2 * LANES),
        ],
        out_specs=pl.BlockSpec((None, cb, nh, LANES), lambda j, b: (b, j, 0, 0)),
        out_shape=jax.ShapeDtypeStruct((bsz, B_WIDTH, nh, LANES), F32),
        compiler_params=_cparams(("parallel", "parallel"), 48),
        name="hyena_conv",
    )(taps, skip, zsrc, gsrc, kf, tb["f1_half"], tb["f4_half"], tb["twr_h"], tb["twi_h"], tb["m2"], tb["m3"])


def _mix_ffn_body(xa_ref, xb_ref, ya_ref, ybt_ref, wa_ref, wb_ref, g_ref, wg_ref, wu_ref, wd_ref, o_ref, h_ref,
                  *, na_tiles):
    x = jnp.where(pl.program_id(0) < na_tiles, xa_ref[...], xb_ref[...])
    x = x + _dot(ya_ref[...], wa_ref[...])
    x = x + lax.dot_general(ybt_ref[...].astype(BF16), wb_ref[...], TN_DIMS, preferred_element_type=F32)
    h_ref[...] = (_rms_scale(x) * g_ref[...]).astype(BF16)
    o_ref[...] = x

    def add_chunk(cs):
        h = h_ref[...]
        a = (jax.nn.silu(_dot(h, wg_ref[:, cs])) * _dot(h, wu_ref[:, cs])).astype(BF16)
        o_ref[...] += _dot(a, wd_ref[cs, :])

    def chunk(c, carry):
        add_chunk(pl.ds(pl.multiple_of(c * FF_CHUNK, FF_CHUNK), FF_CHUNK))
        return carry
    n_full = D_FF // FF_CHUNK
    lax.fori_loop(0, n_full, chunk, 0)
    if D_FF % FF_CHUNK:
        add_chunk(slice(n_full * FF_CHUNK, D_FF))


def _mix_ffn(xa, xb, ya, ybt, wa, wb, g, wg, wu, wd):
    na, seq, _ = xa.shape
    bsz = na + xb.shape[0]
    tm = TM_FFN
    tps = seq // tm
    na_tiles = na * tps
    flat = lambda v: v.reshape(-1, v.shape[-1])
    resident = lambda w: pl.BlockSpec(w.shape, lambda i: (0, 0), pipeline_mode=pl.Buffered(1))
    return pl.pallas_call(
        functools.partial(_mix_ffn_body, na_tiles=na_tiles),
        grid=(bsz * tps,),
        in_specs=[
            pl.BlockSpec((tm, D_MODEL), lambda i: (jnp.minimum(i, na_tiles - 1), 0)),
            pl.BlockSpec((tm, D_MODEL), lambda i: (jnp.maximum(i - na_tiles, 0), 0)),
            pl.BlockSpec((tm, ya.shape[-1]), lambda i: (i, 0)),
            pl.BlockSpec((None, ybt.shape[1], tm), lambda i: (i // tps, 0, i % tps)),
            resident(wa), resident(wb), pl.BlockSpec((1, D_MODEL), lambda i: (0, 0)),
            resident(wg), resident(wu), resident(wd),
        ],
        out_specs=pl.BlockSpec((tm, D_MODEL), lambda i: (i, 0)),
        out_shape=jax.ShapeDtypeStruct((bsz * seq, D_MODEL), F32),
        scratch_shapes=[pltpu.VMEM((tm, D_MODEL), BF16)],
        compiler_params=_cparams(("parallel",), 56),
        name="mix_ffn",
    )(flat(xa), flat(xb), flat(ya), ybt, wa, wb, g, wg, wu, wd)


def _qkv_body(x_ref, g_ref, w_ref, qg_ref, kg_ref, q_ref, k_ref, v_ref):
    h = (_rms_scale(x_ref[...]) * g_ref[...]).astype(BF16)
    nq = N_HEADS * HEAD_DIM
    nk = N_KV_HEADS * LANES
    lo = lax.broadcasted_iota(I32, (1, LANES), 1) < HEAD_DIM
    qgain = qg_ref[...] * (HEAD_DIM ** -0.5)
    wide = 2 * LANES
    for c2 in range(nq // wide):
        xw = _dot(h, w_ref[:, c2 * wide:(c2 + 1) * wide])
        for half in range(2):
            c = 2 * c2 + half
            x = xw[:, half * LANES:(half + 1) * LANES]
            x2 = x * x
            s_lo = jnp.sum(jnp.where(lo, x2, 0.0), axis=-1, keepdims=True)
            s_hi = jnp.sum(jnp.where(lo, 0.0, x2), axis=-1, keepdims=True)
            r = jnp.where(lo, lax.rsqrt(s_lo / HEAD_DIM + EPS), lax.rsqrt(s_hi / HEAD_DIM + EPS))
            q_ref[:, c * LANES:(c + 1) * LANES] = (x * r * qgain).astype(q_ref.dtype)
    for c2 in range(nk // wide):
        xw = _dot(h, w_ref[:, nq + c2 * wide:nq + (c2 + 1) * wide])
        for half in range(2):
            c = 2 * c2 + half
            x = xw[:, half * LANES:(half + 1) * LANES]
            k_ref[:, c * LANES:(c + 1) * LANES] = (_rms_scale(x) * kg_ref[...]).astype(k_ref.dtype)
    v_ref[...] = _dot(h, w_ref[:, nq + nk:]).astype(v_ref.dtype)


def _qkv_proj(x, g, w_qkv, q_g, k_g):
    bsz, seq, _ = x.shape
    tm = min(TM_QKV, seq)
    nq = N_HEADS * HEAD_DIM
    nkv = N_KV_HEADS * HEAD_DIM
    dup = lambda w: jnp.tile(w.reshape(D_MODEL, N_KV_HEADS, 1, HEAD_DIM), (1, 1, 2, 1)).reshape(D_MODEL, 2 * nkv)
    w = jnp.concatenate([w_qkv[:, :nq], dup(w_qkv[:, nq:nq + nkv]), dup(w_qkv[:, nq + nkv:])], axis=1).astype(BF16)
    two = lambda v: jnp.tile(v.astype(F32), 2)[None, :]
    row = lambda width: pl.BlockSpec((None, tm, width), lambda b, i: (b, i, 0))
    const = lambda *shape: pl.BlockSpec(shape, lambda b, i: (0,) * len(shape))
    return pl.pallas_call(
        _qkv_body,
        grid=(bsz, seq // tm),
        in_specs=[row(D_MODEL), const(1, D_MODEL), const(D_MODEL, w.shape[1]), const(1, LANES), const(1, LANES)],
        out_specs=[row(nq), row(2 * nkv), row(2 * nkv)],
        out_shape=[jax.ShapeDtypeStruct((bsz, seq, nq), BF16), jax.ShapeDtypeStruct((bsz, seq, 2 * nkv), BF16),
                   jax.ShapeDtypeStruct((bsz, seq, 2 * nkv), BF16)],
        compiler_params=_cparams(("parallel", "parallel"), 48),
        name="qkv_proj",
    )(x, g, w, two(q_g), two(k_g))


def _t5_bucket(rel):
    nbk = REL_BUCKETS // 2
    max_exact = nbk // 2
    ret = jnp.where(rel > 0, nbk, 0)
    n = jnp.abs(rel)
    large = max_exact + (jnp.log(jnp.maximum(n, 1).astype(F32) / max_exact)
                         / math.log(REL_MAX_DIST / max_exact) * (nbk - max_exact)).astype(I32)
    large = jnp.minimum(large, nbk - 1)
    return ret + jnp.where(n < max_exact, n, large)


def _attn_body(sink_ref, q_ref, kp_ref, ko_ref, kn_ref, vp_ref, vo_ref, vn_ref, bm_ref, o_ref):
    kb = 3 * ATT_BLOCK
    lo = lax.broadcasted_iota(I32, (1, LANES), 1) < HEAD_DIM
    first = lax.broadcasted_iota(I32, (2 * ATT_BLOCK, 1), 0) < ATT_BLOCK
    for hk in range(N_KV_HEADS):
        ks = slice(hk * LANES, (hk + 1) * LANES)
        kk = jnp.concatenate([kp_ref[:, ks], ko_ref[:, ks], kn_ref[:, ks]], axis=0)
        vv = jnp.concatenate([vp_ref[:, ks], vo_ref[:, ks], vn_ref[:, ks]], axis=0)
        zero = jnp.zeros_like(kk)
        kz = jnp.concatenate([jnp.where(lo, kk, zero), jnp.where(lo, zero, kk)], axis=0)
        vz = jnp.concatenate([jnp.where(lo, vv, zero), jnp.where(lo, zero, vv)], axis=0)
        c0 = 2 * hk
        ql = jnp.concatenate([q_ref[:, c0 * LANES:(c0 + 1) * LANES], q_ref[:, (c0 + 1) * LANES:(c0 + 2) * LANES]],
                             axis=0)
        s_all = lax.dot_general(ql, kz, NT_DIMS, preferred_element_type=F32)
        probs, invs = [], []
        for par in range(2):
            ha, hb = GQA_GROUP * hk + par, GQA_GROUP * hk + 2 + par
            s = s_all[:, par * kb:(par + 1) * kb] + jnp.concatenate([bm_ref[ha], bm_ref[hb]], axis=0)
            sk = jnp.where(first, sink_ref[ha], sink_ref[hb])
            m = jnp.maximum(jnp.max(s, axis=-1, keepdims=True), sk)
            pexp = jnp.exp(s - m)
            invs.append(1.0 / (jnp.sum(pexp, axis=-1, keepdims=True) + jnp.exp(sk - m)))
            probs.append(pexp.astype(BF16))
        acc = _dot(jnp.concatenate(probs, axis=1), vz) * jnp.where(lo, invs[0], invs[1])
        o_ref[:, c0 * LANES:(c0 + 1) * LANES] = acc[:ATT_BLOCK].astype(o_ref.dtype)
        o_ref[:, (c0 + 1) * LANES:(c0 + 2) * LANES] = acc[ATT_BLOCK:].astype(o_ref.dtype)


def _attention(q, k2, v2, sink, rel_bias):
    bsz, seq, _ = q.shape
    nblk = seq // ATT_BLOCK
    kb = 3 * ATT_BLOCK
    rel = jnp.arange(kb)[None, :] - ATT_BLOCK - jnp.arange(ATT_BLOCK)[:, None]
    bucket = _t5_bucket(rel)
    rb = rel_bias.astype(F32)
    bias = sum(jnp.where(bucket[None] == b, rb[b][:, None, None], 0.0) for b in range(REL_BUCKETS))
    bm = jnp.where((jnp.abs(rel) <= WINDOW)[None], bias, NEG_INF)
    assert nblk >= 2
    kcol = jnp.arange(kb)[None, None, :]
    bm3 = jnp.stack([jnp.where(kcol >= ATT_BLOCK, bm, NEG_INF), bm, jnp.where(kcol < 2 * ATT_BLOCK, bm, NEG_INF)])
    which = lambda i: jnp.where(i == 0, 0, jnp.where(i == nblk - 1, 2, 1))
    kvw = k2.shape[2]
    prev = lambda b, i: (b, jnp.maximum(i - 1, 0), 0)
    own = lambda b, i: (b, i, 0)
    nxt = lambda b, i: (b, jnp.minimum(i + 1, nblk - 1), 0)
    kv_spec = lambda fn: pl.BlockSpec((None, ATT_BLOCK, kvw), fn)
    return pl.pallas_call(
        _attn_body,
        grid=(bsz, nblk),
        in_specs=[
            pl.BlockSpec(memory_space=pltpu.SMEM),
            pl.BlockSpec((None, ATT_BLOCK, N_HEADS * HEAD_DIM), own),
            kv_spec(prev), kv_spec(own), kv_spec(nxt),
            kv_spec(prev), kv_spec(own), kv_spec(nxt),
            pl.BlockSpec((None, N_HEADS, ATT_BLOCK, kb), lambda b, i: (which(i), 0, 0, 0)),
        ],
        out_specs=pl.BlockSpec((None, ATT_BLOCK, N_HEADS * HEAD_DIM), own),
        out_shape=jax.ShapeDtypeStruct((bsz, seq, N_HEADS * HEAD_DIM), BF16),
        compiler_params=_cparams(("parallel", "parallel"), 48),
        name="window_attention",
    )(sink.astype(F32), q, k2, k2, k2, v2, v2, v2, bm3)


def _router_body(x_ref, o_ref, wo_ref, g_ref, wr_ref, tri_ref, xn_ref, hp_ref, idx_ref, rank_ref, gate_ref, cnt_ref,
                 run_ref):
    @pl.when(pl.program_id(0) == 0)
    def _():
        run_ref[...] = jnp.zeros_like(run_ref)

    x = x_ref[...] + _dot(o_ref[...], wo_ref[...])
    xn_ref[...] = x
    h = _rms_scale(x) * g_ref[...]
    half = D_MODEL // 2
    bits = lax.bitcast_convert_type(h.astype(BF16).astype(F32), U32)
    hp_ref[...] = (bits[:, half:] & jnp.uint32(0xFFFF0000)) | (bits[:, :half] >> 16)

    logits = _dot3(wr_ref[...], h, NT_DIMS)
    eid = lax.broadcasted_iota(I32, logits.shape, 0)
    m1 = jnp.max(logits, axis=0, keepdims=True)
    i1 = jnp.min(jnp.where(logits == m1, eid, N_EXPERTS), axis=0, keepdims=True)
    rest = jnp.where(eid == i1, -jnp.inf, logits)
    m2 = jnp.max(rest, axis=0, keepdims=True)
    i2 = jnp.min(jnp.where(rest == m2, eid, N_EXPERTS), axis=0, keepdims=True)
    e2 = jnp.exp(m2 - m1)
    gate_ref[0:1, :] = 1.0 / (1.0 + e2)
    gate_ref[1:2, :] = e2 / (1.0 + e2)
    idx_ref[0:1, :] = i1
    idx_ref[1:2, :] = i2

    sel1 = eid == i1
    sel2 = eid == i2
    onehot = jnp.where(sel1 | sel2, 1.0, 0.0)
    before = _dot(onehot.astype(BF16), tri_ref[...]) + run_ref[:, 0:1]
    rank_ref[0:1, :] = jnp.sum(jnp.where(sel1, before, 0.0), axis=0, keepdims=True).astype(I32)
    rank_ref[1:2, :] = jnp.sum(jnp.where(sel2, before, 0.0), axis=0, keepdims=True).astype(I32)
    run_ref[...] += jnp.sum(onehot, axis=1, keepdims=True)
    cnt_ref[...] = run_ref[...].astype(I32)


def _router(x2d, o2d, w_out, g, w_router):
    t = x2d.shape[0]
    tm = TM_ROUTER
    tri = (jnp.arange(tm)[:, None] < jnp.arange(tm)[None, :]).astype(BF16)
    two = lambda dt: jax.ShapeDtypeStruct((2, t), dt)
    return pl.pallas_call(
        _router_body,
        grid=(t // tm,),
        in_specs=[
            pl.BlockSpec((tm, D_MODEL), lambda i: (i, 0)),
            pl.BlockSpec((tm, o2d.shape[1]), lambda i: (i, 0)),
            pl.BlockSpec(w_out.shape, lambda i: (0, 0)),
            pl.BlockSpec((1, D_MODEL), lambda i: (0, 0)),
            pl.BlockSpec((N_EXPERTS, D_MODEL), lambda i: (0, 0)),
            pl.BlockSpec((tm, tm), lambda i: (0, 0)),
        ],
        out_specs=[
            pl.BlockSpec((tm, D_MODEL), lambda i: (i, 0)),
            pl.BlockSpec((tm, D_MODEL // 2), lambda i: (i, 0)),
            pl.BlockSpec((2, tm), lambda i: (0, i)),
            pl.BlockSpec((2, tm), lambda i: (0, i)),
            pl.BlockSpec((2, tm), lambda i: (0, i)),
            pl.BlockSpec((N_EXPERTS, LANES), lambda i: (0, 0)),
        ],
        out_shape=[
            jax.ShapeDtypeStruct((t, D_MODEL), F32),
            jax.ShapeDtypeStruct((t, D_MODEL // 2), U32),
            two(I32), two(I32), two(F32),
            jax.ShapeDtypeStruct((N_EXPERTS, LANES), I32),
        ],
        scratch_shapes=[pltpu.VMEM((N_EXPERTS, LANES), F32)],
        compiler_params=_cparams(("arbitrary",), 48),
        name="moe_router",
    )(x2d, o2d, w_out, g, w_router.astype(F32).T, tri)


def _dispatch_body(d1_ref, d2_ref, hp_ref, init_ref, xs_ref, sem, *, rows):
    del init_ref

    def issue(k, c):
        r0 = pl.multiple_of(k * SUBLANES, SUBLANES)
        for u in range(SUBLANES):
            src = hp_ref.at[pl.ds(r0 + u, 1)]
            pltpu.make_async_copy(src, xs_ref.at[pl.ds(d1_ref[0, 0, r0 + u], 1)], sem.at[0]).start()
            pltpu.make_async_copy(src, xs_ref.at[pl.ds(d2_ref[0, 0, r0 + u], 1)], sem.at[1]).start()
        return c
    lax.fori_loop(0, rows // SUBLANES, issue, 0)
    pltpu.make_async_copy(hp_ref, xs_ref.at[pl.ds(0, rows)], sem.at[0]).wait()
    pltpu.make_async_copy(hp_ref, xs_ref.at[pl.ds(0, rows)], sem.at[1]).wait()


def _dispatch(hp, dest, n_rows):
    t, width = hp.shape
    rows = TM_DISPATCH
    idx_spec = pl.BlockSpec((1, 1, rows), lambda i: (i, 0, 0), memory_space=pltpu.SMEM)
    return pl.pallas_call(
        functools.partial(_dispatch_body, rows=rows),
        grid=(t // rows,),
        in_specs=[idx_spec, idx_spec, pl.BlockSpec((rows, width), lambda i: (i, 0)),
                  pl.BlockSpec(memory_space=pl.ANY)],
        out_specs=pl.BlockSpec(memory_space=pl.ANY),
        out_shape=jax.ShapeDtypeStruct((n_rows, width), hp.dtype),
        input_output_aliases={3: 0},
        scratch_shapes=[pltpu.SemaphoreType.DMA((2,))],
        compiler_params=_cparams(("arbitrary",), 32),
        name="moe_dispatch",
    )(dest[0].reshape(t // rows, 1, rows), dest[1].reshape(t // rows, 1, rows), hp,
      jnp.zeros((n_rows, width), hp.dtype))


def _expert_body(te_ref, nu_ref, xs_ref, wg_ref, wu_ref, wd_ref, o_ref, xb_ref, acc_ref):
    i = pl.program_id(0)
    j = pl.program_id(1)
    half = D_MODEL // 2

    @pl.when(j == 0)
    def _():
        w = xs_ref[...]
        xb_ref[:, :half] = lax.bitcast_convert_type(w << 16, F32).astype(BF16)
        xb_ref[:, half:] = lax.bitcast_convert_type(w & jnp.uint32(0xFFFF0000), F32).astype(BF16)
        acc_ref[...] = jnp.zeros_like(acc_ref)

    @pl.when(i < nu_ref[0])
    def _():
        xb = xb_ref[...]
        a = (jax.nn.silu(_dot(xb, wg_ref[...])) * _dot(xb, wu_ref[...])).astype(BF16)
        acc_ref[...] += _dot(a, wd_ref[...])

    @pl.when(j == pl.num_programs(1) - 1)
    def _():
        o_ref[...] = acc_ref[...]


def _experts(xs, tile_expert, n_used, wg, wu, wd):
    p = xs.shape[0]
    tm, tf = TM_MOE, TF_MOE
    grid_spec = pltpu.PrefetchScalarGridSpec(
        num_scalar_prefetch=2,
        grid=(p // tm, D_FF_EXPERT // tf),
        in_specs=[
            pl.BlockSpec((tm, D_MODEL // 2), lambda i, j, te, nu: (i, 0)),
            pl.BlockSpec((None, D_MODEL, tf), lambda i, j, te, nu: (te[i], 0, j)),
            pl.BlockSpec((None, D_MODEL, tf), lambda i, j, te, nu: (te[i], 0, j)),
            pl.BlockSpec((None, tf, D_MODEL), lambda i, j, te, nu: (te[i], j, 0)),
        ],
        out_specs=pl.BlockSpec((tm, D_MODEL), lambda i, j, te, nu: (i, 0)),
        scratch_shapes=[pltpu.VMEM((tm, D_MODEL), BF16), pltpu.VMEM((tm, D_MODEL), F32)],
    )
    return pl.pallas_call(
        _expert_body,
        grid_spec=grid_spec,
        out_shape=jax.ShapeDtypeStruct((p, D_MODEL), F32),
        compiler_params=_cparams(("parallel", "arbitrary"), 56),
        name="moe_experts",
    )(tile_expert, n_used, xs, wg, wu, wd)


def _combine_body(d1c_ref, d2c_ref, d1n_ref, d2n_ref, x_ref, g1_ref, g2_ref, ys_ref, oa_ref, ob_ref,
                  y1_ref, y2_ref, sem, *, rows, na_blocks):
    i = pl.program_id(0)
    slot = i % 2

    def gather(d1_ref, d2_ref, s):
        def issue(k, c):
            r0 = pl.multiple_of(k * SUBLANES, SUBLANES)
            for u in range(SUBLANES):
                pltpu.make_async_copy(ys_ref.at[pl.ds(d1_ref[0, 0, r0 + u], 1)], y1_ref.at[s, pl.ds(r0 + u, 1)],
                                      sem.at[0, s]).start()
                pltpu.make_async_copy(ys_ref.at[pl.ds(d2_ref[0, 0, r0 + u], 1)], y2_ref.at[s, pl.ds(r0 + u, 1)],
                                      sem.at[1, s]).start()
            return c
        lax.fori_loop(0, rows // SUBLANES, issue, 0)

    @pl.when(i == 0)
    def _():
        gather(d1c_ref, d2c_ref, 0)

    @pl.when(i + 1 < pl.num_programs(0))
    def _():
        gather(d1n_ref, d2n_ref, 1 - slot)

    pltpu.make_async_copy(ys_ref.at[pl.ds(0, rows)], y1_ref.at[slot], sem.at[0, slot]).wait()
    pltpu.make_async_copy(ys_ref.at[pl.ds(0, rows)], y2_ref.at[slot], sem.at[1, slot]).wait()
    val = x_ref[...] + g1_ref[...] * y1_ref[slot] + g2_ref[...] * y2_ref[slot]

    @pl.when(i < na_blocks)
    def _():
        oa_ref[...] = val

    @pl.when(i >= na_blocks)
    def _():
        ob_ref[...] = val


def _combine(x2d, ys, dest, gates, t_a):
    t = x2d.shape[0]
    rows = TM_COMBINE
    n = t // rows
    na = t_a // rows
    cur = pl.BlockSpec((1, 1, rows), lambda i: (i, 0, 0), memory_space=pltpu.SMEM)
    nxt = pl.BlockSpec((1, 1, rows), lambda i: (jnp.minimum(i + 1, n - 1), 0, 0), memory_space=pltpu.SMEM)
    gate_spec = pl.BlockSpec((rows, 1), lambda i: (i, 0))
    row_spec = pl.BlockSpec((rows, D_MODEL), lambda i: (i, 0))
    d1 = dest[0].reshape(n, 1, rows)
    d2 = dest[1].reshape(n, 1, rows)
    return pl.pallas_call(
        functools.partial(_combine_body, rows=rows, na_blocks=na),
        grid=(n,),
        in_specs=[cur, cur, nxt, nxt, row_spec, gate_spec, gate_spec, pl.BlockSpec(memory_space=pl.ANY)],
        out_specs=[pl.BlockSpec((rows, D_MODEL), lambda i: (jnp.minimum(i, na - 1), 0)),
                   pl.BlockSpec((rows, D_MODEL), lambda i: (jnp.maximum(i - na, 0), 0))],
        out_shape=[jax.ShapeDtypeStruct((t_a, D_MODEL), F32), jax.ShapeDtypeStruct((t - t_a, D_MODEL), F32)],
        scratch_shapes=[pltpu.VMEM((2, rows, D_MODEL), F32), pltpu.VMEM((2, rows, D_MODEL), F32),
                        pltpu.SemaphoreType.DMA((2, 2))],
        compiler_params=_cparams(("arbitrary",), 48),
        name="moe_combine",
    )(d1, d2, d1, d2, x2d, gates[0][:, None], gates[1][:, None], ys)


def _moe(x2d, o2d, w_out, g, w_router, wg, wu, wd, t_a):
    t = x2d.shape[0]
    tm = TM_MOE
    x2d, hp, idx, rank, gates, cnt = _router(x2d, o2d, w_out, g, w_router)
    counts = cnt[:, 0]
    tiles = (counts + tm - 1) // tm
    tile_end = jnp.cumsum(tiles)
    row_start = (tile_end - tiles) * tm
    n_tiles = (2 * t) // tm + N_EXPERTS
    eid = jnp.arange(N_EXPERTS, dtype=I32)[:, None, None]
    dest = jnp.sum(jnp.where(idx[None] == eid, row_start[:, None, None], 0), axis=0) + rank
    tile_expert = jnp.minimum(jnp.searchsorted(tile_end, jnp.arange(n_tiles), side="right"), N_EXPERTS - 1)
    xs = _dispatch(hp, dest, n_tiles * tm)
    ys = _experts(xs, tile_expert.astype(I32), tile_end[-1:].astype(I32), wg, wu, wd)
    return _combine(x2d, ys, dest, gates, t_a)


def _even_layer(xa, xb, norm_mix, norm_ffn, w_in, w_out, ln_g, ln_b, sgu_w, sgu_b, hy_conv,
                f_w1, f_b1, f_w2, f_b2, f_w3, f_b3, f_freq, hy_skip, wg, wu, wd):
    seq = xa.shape[1]
    bsz = xa.shape[0] + xb.shape[0]
    nh = seq // LANES
    sgu_bb = jnp.broadcast_to(sgu_b.astype(F32)[:, :, None], (A_GROUPS, CHUNK, LANES))
    w_in = w_in.astype(BF16)
    ya, hbt = _even_in(xa, xb, norm_mix[None, :], w_in[:, :2 * A_WIDTH], w_in[:, 2 * A_WIDTH:].T,
                       ln_g[None, :], ln_b[None, :], sgu_w.astype(BF16), sgu_bb)
    hbt = hbt.reshape(bsz, -1, nh, LANES)

    tb = _dft_tables(seq)
    kt = _hyena_kernels(seq, f_w1, f_b1, f_w2, f_b2, f_w3, f_b3, f_freq)
    kf = _filter_spectrum(kt.reshape(HYENA_ORDER, B_WIDTH, 2 * nh, LANES), tb)

    taps = hy_conv.astype(F32)
    skip = hy_skip.astype(F32)
    z1 = _hyena_conv(hbt, 0, hbt, B_WIDTH, taps, skip[0], kf, 0, tb, conv_z=True)
    ybt = _hyena_conv(z1, 0, hbt, 2 * B_WIDTH, taps, skip[1], kf, 1, tb, conv_z=False)

    w_out = w_out.astype(BF16)
    x2d = _mix_ffn(xa, xb, ya, ybt.reshape(bsz, B_WIDTH, seq), w_out[:A_WIDTH], w_out[A_WIDTH:],
                   norm_ffn[None, :], wg.astype(BF16), wu.astype(BF16), wd.astype(BF16))
    return x2d.reshape(bsz, seq, D_MODEL)


def _odd_layer(x, n_a, norm_mix, norm_ffn, w_qkv, q_g, k_g, sink, w_out, rel_bias, w_router, wg, wu, wd):
    bsz, seq, _ = x.shape
    q, k2, v2 = _qkv_proj(x, norm_mix[None, :], w_qkv, q_g, k_g)
    o = _attention(q, k2, v2, sink, rel_bias)
    ya, yb = _moe(x.reshape(bsz * seq, D_MODEL), o.reshape(bsz * seq, -1), w_out.astype(BF16), norm_ffn[None, :],
                  w_router, wg.astype(BF16), wu.astype(BF16), wd.astype(BF16), n_a * seq)
    return ya.reshape(n_a, seq, D_MODEL), yb.reshape(bsz - n_a, seq, D_MODEL)


def kernel(x_prompt, x_sample, norm_mix, norm_ffn, ev_w_in, ev_w_out, sgu_ln_g, sgu_ln_b, sgu_w, sgu_b,
           hy_conv, hy_f_w1, hy_f_b1, hy_f_w2, hy_f_b2, hy_f_w3, hy_f_b3, hy_f_freq, hy_skip,
           ffn_w_gate, ffn_w_up, ffn_w_down, at_w_qkv, at_q_norm, at_k_norm, at_sink, at_w_out,
           rel_bias, moe_router, moe_w_gate, moe_w_up, moe_w_down):
    even_p = (ev_w_in, ev_w_out, sgu_ln_g, sgu_ln_b, sgu_w, sgu_b, hy_conv, hy_f_w1, hy_f_b1, hy_f_w2,
              hy_f_b2, hy_f_w3, hy_f_b3, hy_f_freq, hy_skip, ffn_w_gate, ffn_w_up, ffn_w_down)
    odd_p = (at_w_qkv, at_q_norm, at_k_norm, at_sink, at_w_out, moe_router, moe_w_gate, moe_w_up, moe_w_down)
    assert x_prompt.shape[1:] == x_sample.shape[1:]
    n_a = x_prompt.shape[0]
    depth = norm_mix.shape[0]
    assert depth % 2 == 0, "layers come in (even, odd) pairs"
    xa, xb = x_prompt, x_sample
    for i in range(0, depth, 2):
        j = i // 2
        x = _even_layer(xa, xb, norm_mix[i], norm_ffn[i], *[p[j] for p in even_p])
        xa, xb = _odd_layer(x, n_a, norm_mix[i + 1], norm_ffn[i + 1], *[p[j] for p in odd_p[:5]], rel_bias,
                            *[p[j] for p in odd_p[5:]])
    return (xa, xb)
```

```python
import functools
import math

import jax
import jax.numpy as jnp
from jax import lax
from jax.experimental import pallas as pl
from jax.experimental.pallas import tpu as pltpu

F32 = jnp.float32
BF16 = jnp.bfloat16
U32 = jnp.uint32
I32 = jnp.int32

D_MODEL = 1024
A_GROUPS = 4
A_WIDTH = D_MODEL // 2
CHUNK = 128
B_WIDTH = D_MODEL // 2
HYENA_ORDER = 2
FILTER_DIRS = 2
FILTER_BANDS = 16
FILTER_EMB = 1 + 2 * FILTER_BANDS
FILTER_HIDDEN = 64
DECAY_TARGET = 1e-2
FAST_DECAY_PCT = 0.3
SLOW_DECAY_PCT = 1.5
HEAD_DIM = 64
N_HEADS = D_MODEL // HEAD_DIM
N_KV_HEADS = N_HEADS // 4
GQA_GROUP = N_HEADS // N_KV_HEADS
WINDOW = 128
ATT_BLOCK = 128
REL_BUCKETS = 32
REL_MAX_DIST = 128
NEG_INF = -1e30
D_FF = 2816
N_EXPERTS = 8
D_FF_EXPERT = 3584
EPS = 1e-6

LANES = 128
SUBLANES = 8
MIB = 1024 * 1024

TM_PROJ = 512
TM_QKV = 1024
TM_FFN = 512
FF_CHUNK = 1024
TM_MOE = 512
TF_MOE = D_FF_EXPERT // 2
TM_ROUTER = 512
TM_COMBINE = 1024
TM_DISPATCH = 2048
HY_CB = 64
HY_PAIRS = 8

NT_DIMS = (((1,), (1,)), ((), ()))
TN_DIMS = (((0,), (0,)), ((), ()))


def _cparams(sem, vmem_mib):
    return pltpu.CompilerParams(dimension_semantics=sem, vmem_limit_bytes=vmem_mib * MIB)


def _rms_scale(x):
    return x * lax.rsqrt(jnp.mean(x * x, axis=-1, keepdims=True) + EPS)


def _dot(a, b):
    return jnp.dot(a, b, preferred_element_type=F32)


def _dot3(a, b, dims):
    def split(x):
        hi = x.astype(BF16)
        return hi, (x - hi.astype(F32)).astype(BF16)
    (ah, al), (bh, bl) = split(a), split(b)
    mm = lambda x, y: lax.dot_general(x, y, dims, preferred_element_type=F32)
    return mm(ah, bh) + (mm(al, bh) + mm(ah, bl))


def _group_specs(xa, xb, tm):
    na, seq = xa.shape[0], xa.shape[1]
    last = seq // tm - 1
    width = xa.shape[2]
    spec_a = pl.BlockSpec((None, tm, width), lambda b, i: (jnp.minimum(b, na - 1), jnp.where(b < na, i, last), 0))
    spec_b = pl.BlockSpec((None, tm, width), lambda b, i: (jnp.maximum(b - na, 0), jnp.where(b < na, 0, i), 0))
    return spec_a, spec_b


def _group_tile(xa_ref, xb_ref, na):
    return jnp.where(pl.program_id(0) < na, xa_ref[...], xb_ref[...])


def _even_in_body(xa_ref, xb_ref, g_ref, w_ref, wht_ref, lng_ref, lnb_ref, sw_ref, sb_ref, ya_ref, hbt_ref, *, na):
    x = _group_tile(xa_ref, xb_ref, na)
    h = (_rms_scale(x) * g_ref[...]).astype(BF16)
    u = jax.nn.gelu(_dot(h, w_ref[:, 0:A_WIDTH]))
    v = jax.nn.gelu(_dot(h, w_ref[:, A_WIDTH:2 * A_WIDTH]))
    hbt_ref[...] = lax.dot_general(wht_ref[...], h, NT_DIMS, preferred_element_type=F32)
    tm = x.shape[0]
    for gi in range(A_GROUPS):
        cs = slice(gi * LANES, (gi + 1) * LANES)
        vg = v[:, cs]
        xc = vg - jnp.mean(vg, axis=-1, keepdims=True)
        var = jnp.mean(xc * xc, axis=-1, keepdims=True)
        vn = (xc * lax.rsqrt(var + EPS) * lng_ref[:, cs] + lnb_ref[:, cs]).astype(BF16)
        for c in range(tm // CHUNK):
            rs = slice(c * CHUNK, (c + 1) * CHUNK)
            mixed = _dot(sw_ref[gi], vn[rs]) + sb_ref[gi]
            ya_ref[rs, cs] = (u[rs, cs] * mixed).astype(ya_ref.dtype)


def _even_in(xa, xb, g, w_uv, w_hb_t, ln_g, ln_b, sgu_w, sgu_b):
    na, seq, _ = xa.shape
    bsz = na + xb.shape[0]
    tm = TM_PROJ
    n_hb = w_hb_t.shape[0]
    const = lambda *shape: pl.BlockSpec(shape, lambda b, i: (0,) * len(shape))
    return pl.pallas_call(
        functools.partial(_even_in_body, na=na),
        grid=(bsz, seq // tm),
        in_specs=[
            *_group_specs(xa, xb, tm),
            const(1, D_MODEL),
            const(D_MODEL, 2 * A_WIDTH),
            const(n_hb, D_MODEL),
            const(1, A_WIDTH),
            const(1, A_WIDTH),
            const(A_GROUPS, CHUNK, CHUNK),
            const(A_GROUPS, CHUNK, LANES),
        ],
        out_specs=[
            pl.BlockSpec((None, tm, A_WIDTH), lambda b, i: (b, i, 0)),
            pl.BlockSpec((None, n_hb, tm), lambda b, i: (b, 0, i)),
        ],
        out_shape=[
            jax.ShapeDtypeStruct((bsz, seq, A_WIDTH), BF16),
            jax.ShapeDtypeStruct((bsz, n_hb, seq), F32),
        ],
        compiler_params=_cparams(("parallel", "parallel"), 48),
        name="even_in",
    )(xa, xb, g, w_uv, w_hb_t, ln_g, ln_b, sgu_w, sgu_b)


def _filter_body(ft_ref, w1_ref, b1_ref, w2_ref, b2_ref, w3_ref, b3_ref, fr0_ref, fr1_ref, absd_ref, o_ref, *, seq):
    hp = lax.Precision.HIGHEST
    nn = (((1,), (0,)), ((), ()))
    ft = ft_ref[...]
    tl = ft.shape[1]
    h = jnp.sin(fr0_ref[...] * (jnp.dot(w1_ref[...], ft, precision=hp, preferred_element_type=F32) + b1_ref[...]))
    h = jnp.sin(fr1_ref[...] * (jnp.dot(w2_ref[...], h, precision=hp, preferred_element_type=F32) + b2_ref[...]))
    out = _dot3(w3_ref[...], h, nn) + b3_ref[...]
    decay = jnp.exp(-absd_ref[...] * ft[0:1, :])
    pos = pl.program_id(0) * tl + lax.broadcasted_iota(I32, (1, tl), 1)
    for o in range(HYENA_ORDER):
        o_ref[o] = jnp.where(pos == seq, 0.0, out[o * B_WIDTH:(o + 1) * B_WIDTH] * decay)


def _hyena_kernels(seq, w1, b1, w2, b2, w3, b3, freq):
    t = jnp.linspace(0.0, 1.0, seq, dtype=F32)[:, None]
    w = 2.0 * math.pi * jnp.arange(seq, dtype=F32)[:, None] / seq
    bands = jnp.linspace(1e-4, FILTER_BANDS - 1, FILTER_BANDS, dtype=F32)[None, :]
    feats = jnp.concatenate([t, jnp.cos(bands * w), jnp.sin(bands * w)], axis=-1)
    feats = jnp.concatenate([feats, feats[:1], feats[:0:-1]], axis=0)
    emb = -(-FILTER_EMB // SUBLANES) * SUBLANES
    feats_t = jnp.pad(feats, ((0, 0), (0, emb - FILTER_EMB))).T
    w1t = jnp.pad(w1.astype(F32), ((0, emb - FILTER_EMB), (0, 0))).T
    w2t = w2.astype(F32).T
    n_dir = HYENA_ORDER * B_WIDTH
    w3t = w3.astype(F32).T.reshape(FILTER_DIRS, n_dir, FILTER_HIDDEN)
    col = lambda v: v.astype(F32)[:, None]
    b3c = b3.astype(F32).reshape(FILTER_DIRS, n_dir, 1)
    deltas = jnp.linspace(math.log(DECAY_TARGET) / SLOW_DECAY_PCT,
                          math.log(DECAY_TARGET) / FAST_DECAY_PCT, B_WIDTH, dtype=F32)
    tl = min(seq, 1024)
    nhalf = seq // tl
    const = lambda *shape: pl.BlockSpec(shape, lambda i: (0,) * len(shape))
    hid = FILTER_HIDDEN
    return pl.pallas_call(
        functools.partial(_filter_body, seq=seq),
        grid=(2 * nhalf,),
        in_specs=[
            pl.BlockSpec((emb, tl), lambda i: (0, i)),
            const(hid, emb), const(hid, 1), const(hid, hid), const(hid, 1),
            pl.BlockSpec((None, n_dir, hid), lambda i: (i // nhalf, 0, 0)),
            pl.BlockSpec((None, n_dir, 1), lambda i: (i // nhalf, 0, 0)),
            const(hid, 1), const(hid, 1), const(B_WIDTH, 1),
        ],
        out_specs=pl.BlockSpec((HYENA_ORDER, B_WIDTH, tl), lambda i: (0, 0, i)),
        out_shape=jax.ShapeDtypeStruct((HYENA_ORDER, B_WIDTH, 2 * seq), F32),
        compiler_params=_cparams(("parallel",), 48),
        name="hyena_filter",
    )(feats_t, w1t, col(b1), w2t, col(b2), w3t, b3c, col(freq[0]), col(freq[1]), col(jnp.abs(deltas)))


def _dft_tables(seq):
    n = 2 * seq
    nb = n // LANES
    k1 = jnp.arange(nb, dtype=I32)[:, None]
    ang1 = (2.0 * math.pi / nb) * ((k1 * jnp.arange(nb, dtype=I32)[None, :]) % nb).astype(F32)
    f1_full = jnp.concatenate([jnp.cos(ang1), -jnp.sin(ang1)], axis=0)
    f1_half = f1_full[:, : nb // 2]
    f4_half = f1_half.T
    angt = (2.0 * math.pi / n) * ((k1 * jnp.arange(LANES, dtype=I32)[None, :]) % n).astype(F32)
    twr, twi = jnp.cos(angt), -jnp.sin(angt)
    a = jnp.arange(LANES, dtype=I32)
    ang2 = (2.0 * math.pi / LANES) * ((a[:, None] * a[None, :]) % LANES).astype(F32)
    cr, ci = jnp.cos(ang2), -jnp.sin(ang2)
    m2 = jnp.concatenate([jnp.concatenate([cr, ci], axis=1), jnp.concatenate([-ci, cr], axis=1)], axis=0)
    m3 = jnp.concatenate([jnp.concatenate([cr, -ci], axis=1), jnp.concatenate([ci, cr], axis=1)], axis=0)
    return dict(f1_full=f1_full.astype(BF16), f1_half=f1_half.astype(BF16), f4_half=f4_half.astype(BF16),
                twr=twr, twi=twi, twr_h=twr.astype(BF16), twi_h=twi.astype(BF16),
                m2=m2.astype(BF16), m3=m3.astype(BF16))


def _fwd_spectrum(pairs, f1_ref, twr_ref, twi_ref, m2_ref):
    nb = twr_ref.shape[0]
    a_all = [_dot(f1_ref[...], jnp.concatenate(xs, axis=1).astype(BF16)) for xs in pairs]
    twr, twi = twr_ref[...], twi_ref[...]
    out = []
    for a in a_all:
        a = a.astype(twr.dtype)
        lhs = []
        for d in range(2):
            ar = a[:nb, d * LANES:(d + 1) * LANES]
            ai = a[nb:, d * LANES:(d + 1) * LANES]
            lhs.append(jnp.concatenate([ar * twr - ai * twi, ar * twi + ai * twr], axis=1))
        out.append(_dot(jnp.concatenate(lhs, axis=0).astype(BF16), m2_ref[...]))
    return out


def _spec_body(k_ref, f1_ref, twr_ref, twi_ref, m2_ref, o_ref):
    nb = twr_ref.shape[0]
    inv_n = 1.0 / (nb * LANES)

    def group(it, carry):
        c0 = 2 * HY_PAIRS * it
        pairs = [[k_ref[c0 + 2 * g], k_ref[c0 + 2 * g + 1]] for g in range(HY_PAIRS)]
        for g, z in enumerate(_fwd_spectrum(pairs, f1_ref, twr_ref, twi_ref, m2_ref)):
            o_ref[c0 + 2 * g] = (z[:nb] * inv_n).astype(o_ref.dtype)
            o_ref[c0 + 2 * g + 1] = (z[nb:] * inv_n).astype(o_ref.dtype)
        return carry
    lax.fori_loop(0, k_ref.shape[0] // (2 * HY_PAIRS), group, 0)


def _filter_spectrum(kt, tb):
    orders, c, nb, _ = kt.shape
    cb = HY_CB
    const = lambda *shape: pl.BlockSpec(shape, lambda o, j: (0,) * len(shape))
    return pl.pallas_call(
        _spec_body,
        grid=(orders, c // cb),
        in_specs=[
            pl.BlockSpec((None, cb, nb, LANES), lambda o, j: (o, j, 0, 0)),
            const(2 * nb, nb), const(nb, LANES), const(nb, LANES), const(2 * LANES, 2 * LANES),
        ],
        out_specs=pl.BlockSpec((None, cb, nb, 2 * LANES), lambda o, j: (o, j, 0, 0)),
        out_shape=jax.ShapeDtypeStruct((orders, c, nb, 2 * LANES), BF16),
        compiler_params=_cparams(("parallel", "parallel"), 48),
        name="hyena_filter_spectrum",
    )(kt, tb["f1_full"], tb["twr"], tb["twi"], tb["m2"])


def _shift_rows(x, down):
    rows = x.shape[0]
    idx = lax.broadcasted_iota(I32, x.shape, 0)
    if down:
        return jnp.where(idx == 0, 0.0, pltpu.roll(x, 1, axis=0))
    return jnp.where(idx == rows - 1, 0.0, pltpu.roll(x, rows - 1, axis=0))


def _short_conv(x, taps_ref, ch):
    lane = lax.broadcasted_iota(I32, x.shape, 1)
    prev = pltpu.roll(jnp.where(lane == LANES - 1, _shift_rows(x, True), x), 1, axis=1)
    nxt = pltpu.roll(jnp.where(lane == 0, _shift_rows(x, False), x), LANES - 1, axis=1)
    return taps_ref[0, ch] * prev + taps_ref[1, ch] * x + taps_ref[2, ch] * nxt


def _conv_body(taps_ref, skip_ref, z_ref, g_ref, kf_ref, f1_ref, f4_ref, twr_ref, twi_ref, m2_ref, m3_ref,
               o_ref, *, zch, gch, conv_z):
    nb = twr_ref.shape[0]
    cb = z_ref.shape[0]
    base = pl.program_id(0) * cb

    def group(it, carry):
        c0 = 2 * HY_PAIRS * it
        chans = [[c0 + 2 * g, c0 + 2 * g + 1] for g in range(HY_PAIRS)]
        xs = [[_short_conv(z_ref[c], taps_ref, zch + base + c) if conv_z else z_ref[c] for c in pr] for pr in chans]
        zs = _fwd_spectrum(xs, f1_ref, twr_ref, twi_ref, m2_ref)
        ccs = []
        for pr, z in zip(chans, zs):
            z = z.astype(kf_ref.dtype)
            ys = []
            for d, c in enumerate(pr):
                zr = z[d * nb:(d + 1) * nb, :LANES]
                zi = z[d * nb:(d + 1) * nb, LANES:]
                kf = kf_ref[c]
                kr, ki = kf[:, :LANES], kf[:, LANES:]
                ys.append(jnp.concatenate([zr * kr - zi * ki, zr * ki + zi * kr], axis=1))
            ccs.append(_dot(jnp.concatenate(ys, axis=0).astype(BF16), m3_ref[...]))
        twr, twi = twr_ref[...], twi_ref[...]
        for pr, x2, cc in zip(chans, xs, ccs):
            cc = cc.astype(twr.dtype)
            drs, dis = [], []
            for d in range(2):
                ccr = cc[d * nb:(d + 1) * nb, :LANES]
                cci = cc[d * nb:(d + 1) * nb, LANES:]
                drs.append(ccr * twr + cci * twi)
                dis.append(cci * twr - ccr * twi)
            rhs = jnp.concatenate([jnp.concatenate(drs, axis=1), jnp.concatenate(dis, axis=1)], axis=0)
            y = _dot(f4_ref[...], rhs.astype(BF16))
            for d, c in enumerate(pr):
                gate = _short_conv(g_ref[c], taps_ref, gch + base + c)
                o_ref[c] = gate * (y[:, d * LANES:(d + 1) * LANES] + skip_ref[base + c] * x2[d])
        return carry
    lax.fori_loop(0, cb // (2 * HY_PAIRS), group, 0)


def _hyena_conv(zsrc, zch, gsrc, gch, taps, skip, kf, order, tb, conv_z):
    bsz, _, nh, _ = zsrc.shape
    nb = 2 * nh
    cb = HY_CB
    smem = pl.BlockSpec(memory_space=pltpu.SMEM)
    const = lambda *shape: pl.BlockSpec(shape, lambda j, b: (0,) * len(shape))
    zblk, gblk = zch // cb, gch // cb
    return pl.pallas_call(
        functools.partial(_conv_body, zch=zch, gch=gch, conv_z=conv_z),
        grid=(B_WIDTH // cb, bsz),
        in_specs=[
            smem, smem,
            pl.BlockSpec((None, cb, nh, LANES), lambda j, b: (b, zblk + j, 0, 0)),
            pl.BlockSpec((None, cb, nh, LANES), lambda j, b: (b, gblk + j, 0, 0)),
            pl.BlockSpec((None, cb, nb, 2 * LANES), lambda j, b: (order, j, 0, 0)),
            const(2 * nb, nh), const(nh, 2 * nb), const(nb, LANES), const(nb, LANES),
            const(2 * LANES, 2 * LANES), const(2 * LANES, 2 * LANES),
        ],
        out_specs=pl.BlockSpec((None, cb, nh, LANES), lambda j, b: (b, j, 0, 0)),
        out_shape=jax.ShapeDtypeStruct((bsz, B_WIDTH, nh, LANES), F32),
        compiler_params=_cparams(("parallel", "parallel"), 48),
        name="hyena_conv",
    )(taps, skip, zsrc, gsrc, kf, tb["f1_half"], tb["f4_half"], tb["twr_h"], tb["twi_h"], tb["m2"], tb["m3"])


def _mix_ffn_body(xa_ref, xb_ref, ya_ref, ybt_ref, wa_ref, wb_ref, g_ref, wg_ref, wu_ref, wd_ref, o_ref, h_ref,
                  *, na_tiles):
    x = jnp.where(pl.program_id(0) < na_tiles, xa_ref[...], xb_ref[...])
    x = x + _dot(ya_ref[...], wa_ref[...])
    x = x + lax.dot_general(ybt_ref[...].astype(BF16), wb_ref[...], TN_DIMS, preferred_element_type=F32)
    h_ref[...] = (_rms_scale(x) * g_ref[...]).astype(BF16)
    o_ref[...] = x

    def add_chunk(cs):
        h = h_ref[...]
        a = (jax.nn.silu(_dot(h, wg_ref[:, cs])) * _dot(h, wu_ref[:, cs])).astype(BF16)
        o_ref[...] += _dot(a, wd_ref[cs, :])

    def chunk(c, carry):
        add_chunk(pl.ds(pl.multiple_of(c * FF_CHUNK, FF_CHUNK), FF_CHUNK))
        return carry
    n_full = D_FF // FF_CHUNK
    lax.fori_loop(0, n_full, chunk, 0)
    if D_FF % FF_CHUNK:
        add_chunk(slice(n_full * FF_CHUNK, D_FF))


def _mix_ffn(xa, xb, ya, ybt, wa, wb, g, wg, wu, wd):
    na, seq, _ = xa.shape
    bsz = na + xb.shape[0]
    tm = TM_FFN
    tps = seq // tm
    na_tiles = na * tps
    flat = lambda v: v.reshape(-1, v.shape[-1])
    resident = lambda w: pl.BlockSpec(w.shape, lambda i: (0, 0), pipeline_mode=pl.Buffered(1))
    return pl.pallas_call(
        functools.partial(_mix_ffn_body, na_tiles=na_tiles),
        grid=(bsz * tps,),
        in_specs=[
            pl.BlockSpec((tm, D_MODEL), lambda i: (jnp.minimum(i, na_tiles - 1), 0)),
            pl.BlockSpec((tm, D_MODEL), lambda i: (jnp.maximum(i - na_tiles, 0), 0)),
            pl.BlockSpec((tm, ya.shape[-1]), lambda i: (i, 0)),
            pl.BlockSpec((None, ybt.shape[1], tm), lambda i: (i // tps, 0, i % tps)),
            resident(wa), resident(wb), pl.BlockSpec((1, D_MODEL), lambda i: (0, 0)),
            resident(wg), resident(wu), resident(wd),
        ],
        out_specs=pl.BlockSpec((tm, D_MODEL), lambda i: (i, 0)),
        out_shape=jax.ShapeDtypeStruct((bsz * seq, D_MODEL), F32),
        scratch_shapes=[pltpu.VMEM((tm, D_MODEL), BF16)],
        compiler_params=_cparams(("parallel",), 56),
        name="mix_ffn",
    )(flat(xa), flat(xb), flat(ya), ybt, wa, wb, g, wg, wu, wd)


def _qkv_body(x_ref, g_ref, w_ref, qg_ref, kg_ref, q_ref, k_ref, v_ref):
    h = (_rms_scale(x_ref[...]) * g_ref[...]).astype(BF16)
    nq = N_HEADS * HEAD_DIM
    nk = N_KV_HEADS * LANES
    lo = lax.broadcasted_iota(I32, (1, LANES), 1) < HEAD_DIM
    qgain = qg_ref[...] * (HEAD_DIM ** -0.5)
    wide = 2 * LANES
    for c2 in range(nq // wide):
        xw = _dot(h, w_ref[:, c2 * wide:(c2 + 1) * wide])
        for half in range(2):
            c = 2 * c2 + half
            x = xw[:, half * LANES:(half + 1) * LANES]
            x2 = x * x
            s_lo = jnp.sum(jnp.where(lo, x2, 0.0), axis=-1, keepdims=True)
            s_hi = jnp.sum(jnp.where(lo, 0.0, x2), axis=-1, keepdims=True)
            r = jnp.where(lo, lax.rsqrt(s_lo / HEAD_DIM + EPS), lax.rsqrt(s_hi / HEAD_DIM + EPS))
            q_ref[:, c * LANES:(c + 1) * LANES] = (x * r * qgain).astype(q_ref.dtype)
    for c2 in range(nk // wide):
        xw = _dot(h, w_ref[:, nq + c2 * wide:nq + (c2 + 1) * wide])
        for half in range(2):
            c = 2 * c2 + half
            x = xw[:, half * LANES:(half + 1) * LANES]
            k_ref[:, c * LANES:(c + 1) * LANES] = (_rms_scale(x) * kg_ref[...]).astype(k_ref.dtype)
    v_ref[...] = _dot(h, w_ref[:, nq + nk:]).astype(v_ref.dtype)


def _qkv_proj(x, g, w_qkv, q_g, k_g):
    bsz, seq, _ = x.shape
    tm = min(TM_QKV, seq)
    nq = N_HEADS * HEAD_DIM
    nkv = N_KV_HEADS * HEAD_DIM
    dup = lambda w: jnp.tile(w.reshape(D_MODEL, N_KV_HEADS, 1, HEAD_DIM), (1, 1, 2, 1)).reshape(D_MODEL, 2 * nkv)
    w = jnp.concatenate([w_qkv[:, :nq], dup(w_qkv[:, nq:nq + nkv]), dup(w_qkv[:, nq + nkv:])], axis=1).astype(BF16)
    two = lambda v: jnp.tile(v.astype(F32), 2)[None, :]
    row = lambda width: pl.BlockSpec((None, tm, width), lambda b, i: (b, i, 0))
    const = lambda *shape: pl.BlockSpec(shape, lambda b, i: (0,) * len(shape))
    return pl.pallas_call(
        _qkv_body,
        grid=(bsz, seq // tm),
        in_specs=[row(D_MODEL), const(1, D_MODEL), const(D_MODEL, w.shape[1]), const(1, LANES), const(1, LANES)],
        out_specs=[row(nq), row(2 * nkv), row(2 * nkv)],
        out_shape=[jax.ShapeDtypeStruct((bsz, seq, nq), BF16), jax.ShapeDtypeStruct((bsz, seq, 2 * nkv), BF16),
                   jax.ShapeDtypeStruct((bsz, seq, 2 * nkv), BF16)],
        compiler_params=_cparams(("parallel", "parallel"), 48),
        name="qkv_proj",
    )(x, g, w, two(q_g), two(k_g))


def _t5_bucket(rel):
    nbk = REL_BUCKETS // 2
    max_exact = nbk // 2
    ret = jnp.where(rel > 0, nbk, 0)
    n = jnp.abs(rel)
    large = max_exact + (jnp.log(jnp.maximum(n, 1).astype(F32) / max_exact)
                         / math.log(REL_MAX_DIST / max_exact) * (nbk - max_exact)).astype(I32)
    large = jnp.minimum(large, nbk - 1)
    return ret + jnp.where(n < max_exact, n, large)


def _attn_body(sink_ref, q_ref, kp_ref, ko_ref, kn_ref, vp_ref, vo_ref, vn_ref, bm_ref, o_ref):
    kb = 3 * ATT_BLOCK
    lo = lax.broadcasted_iota(I32, (1, LANES), 1) < HEAD_DIM
    first = lax.broadcasted_iota(I32, (2 * ATT_BLOCK, 1), 0) < ATT_BLOCK
    for hk in range(N_KV_HEADS):
        ks = slice(hk * LANES, (hk + 1) * LANES)
        kk = jnp.concatenate([kp_ref[:, ks], ko_ref[:, ks], kn_ref[:, ks]], axis=0)
        vv = jnp.concatenate([vp_ref[:, ks], vo_ref[:, ks], vn_ref[:, ks]], axis=0)
        zero = jnp.zeros_like(kk)
        kz = jnp.concatenate([jnp.where(lo, kk, zero), jnp.where(lo, zero, kk)], axis=0)
        vz = jnp.concatenate([jnp.where(lo, vv, zero), jnp.where(lo, zero, vv)], axis=0)
        c0 = 2 * hk
        ql = jnp.concatenate([q_ref[:, c0 * LANES:(c0 + 1) * LANES], q_ref[:, (c0 + 1) * LANES:(c0 + 2) * LANES]],
                             axis=0)
        s_all = lax.dot_general(ql, kz, NT_DIMS, preferred_element_type=F32)
        probs, invs = [], []
        for par in range(2):
            ha, hb = GQA_GROUP * hk + par, GQA_GROUP * hk + 2 + par
            s = s_all[:, par * kb:(par + 1) * kb] + jnp.concatenate([bm_ref[ha], bm_ref[hb]], axis=0)
            sk = jnp.where(first, sink_ref[ha], sink_ref[hb])
            m = jnp.maximum(jnp.max(s, axis=-1, keepdims=True), sk)
            pexp = jnp.exp(s - m)
            invs.append(1.0 / (jnp.sum(pexp, axis=-1, keepdims=True) + jnp.exp(sk - m)))
            probs.append(pexp.astype(BF16))
        acc = _dot(jnp.concatenate(probs, axis=1), vz) * jnp.where(lo, invs[0], invs[1])
        o_ref[:, c0 * LANES:(c0 + 1) * LANES] = acc[:ATT_BLOCK].astype(o_ref.dtype)
        o_ref[:, (c0 + 1) * LANES:(c0 + 2) * LANES] = acc[ATT_BLOCK:].astype(o_ref.dtype)


def _attention(q, k2, v2, sink, rel_bias):
    bsz, seq, _ = q.shape
    nblk = seq // ATT_BLOCK
    kb = 3 * ATT_BLOCK
    rel = jnp.arange(kb)[None, :] - ATT_BLOCK - jnp.arange(ATT_BLOCK)[:, None]
    bucket = _t5_bucket(rel)
    rb = rel_bias.astype(F32)
    bias = sum(jnp.where(bucket[None] == b, rb[b][:, None, None], 0.0) for b in range(REL_BUCKETS))
    bm = jnp.where((jnp.abs(rel) <= WINDOW)[None], bias, NEG_INF)
    assert nblk >= 2
    kcol = jnp.arange(kb)[None, None, :]
    bm3 = jnp.stack([jnp.where(kcol >= ATT_BLOCK, bm, NEG_INF), bm, jnp.where(kcol < 2 * ATT_BLOCK, bm, NEG_INF)])
    which = lambda i: jnp.where(i == 0, 0, jnp.where(i == nblk - 1, 2, 1))
    kvw = k2.shape[2]
    prev = lambda b, i: (b, jnp.maximum(i - 1, 0), 0)
    own = lambda b, i: (b, i, 0)
    nxt = lambda b, i: (b, jnp.minimum(i + 1, nblk - 1), 0)
    kv_spec = lambda fn: pl.BlockSpec((None, ATT_BLOCK, kvw), fn)
    return pl.pallas_call(
        _attn_body,
        grid=(bsz, nblk),
        in_specs=[
            pl.BlockSpec(memory_space=pltpu.SMEM),
            pl.BlockSpec((None, ATT_BLOCK, N_HEADS * HEAD_DIM), own),
            kv_spec(prev), kv_spec(own), kv_spec(nxt),
            kv_spec(prev), kv_spec(own), kv_spec(nxt),
            pl.BlockSpec((None, N_HEADS, ATT_BLOCK, kb), lambda b, i: (which(i), 0, 0, 0)),
        ],
        out_specs=pl.BlockSpec((None, ATT_BLOCK, N_HEADS * HEAD_DIM), own),
        out_shape=jax.ShapeDtypeStruct((bsz, seq, N_HEADS * HEAD_DIM), BF16),
        compiler_params=_cparams(("parallel", "parallel"), 48),
        name="window_attention",
    )(sink.astype(F32), q, k2, k2, k2, v2, v2, v2, bm3)


def _router_body(x_ref, o_ref, wo_ref, g_ref, wr_ref, tri_ref, xn_ref, hp_ref, idx_ref, rank_ref, gate_ref, cnt_ref,
                 run_ref):
    @pl.when(pl.program_id(0) == 0)
    def _():
        run_ref[...] = jnp.zeros_like(run_ref)

    x = x_ref[...] + _dot(o_ref[...], wo_ref[...])
    xn_ref[...] = x
    h = _rms_scale(x) * g_ref[...]
    half = D_MODEL // 2
    bits = lax.bitcast_convert_type(h.astype(BF16).astype(F32), U32)
    hp_ref[...] = (bits[:, half:] & jnp.uint32(0xFFFF0000)) | (bits[:, :half] >> 16)

    logits = _dot3(wr_ref[...], h, NT_DIMS)
    eid = lax.broadcasted_iota(I32, logits.shape, 0)
    m1 = jnp.max(logits, axis=0, keepdims=True)
    i1 = jnp.min(jnp.where(logits == m1, eid, N_EXPERTS), axis=0, keepdims=True)
    rest = jnp.where(eid == i1, -jnp.inf, logits)
    m2 = jnp.max(rest, axis=0, keepdims=True)
    i2 = jnp.min(jnp.where(rest == m2, eid, N_EXPERTS), axis=0, keepdims=True)
    e2 = jnp.exp(m2 - m1)
    gate_ref[0:1, :] = 1.0 / (1.0 + e2)
    gate_ref[1:2, :] = e2 / (1.0 + e2)
    idx_ref[0:1, :] = i1
    idx_ref[1:2, :] = i2

    sel1 = eid == i1
    sel2 = eid == i2
    onehot = jnp.where(sel1 | sel2, 1.0, 0.0)
    before = _dot(onehot.astype(BF16), tri_ref[...]) + run_ref[:, 0:1]
    rank_ref[0:1, :] = jnp.sum(jnp.where(sel1, before, 0.0), axis=0, keepdims=True).astype(I32)
    rank_ref[1:2, :] = jnp.sum(jnp.where(sel2, before, 0.0), axis=0, keepdims=True).astype(I32)
    run_ref[...] += jnp.sum(onehot, axis=1, keepdims=True)
    cnt_ref[...] = run_ref[...].astype(I32)


def _router(x2d, o2d, w_out, g, w_router):
    t = x2d.shape[0]
    tm = TM_ROUTER
    tri = (jnp.arange(tm)[:, None] < jnp.arange(tm)[None, :]).astype(BF16)
    two = lambda dt: jax.ShapeDtypeStruct((2, t), dt)
    return pl.pallas_call(
        _router_body,
        grid=(t // tm,),
        in_specs=[
            pl.BlockSpec((tm, D_MODEL), lambda i: (i, 0)),
            pl.BlockSpec((tm, o2d.shape[1]), lambda i: (i, 0)),
            pl.BlockSpec(w_out.shape, lambda i: (0, 0)),
            pl.BlockSpec((1, D_MODEL), lambda i: (0, 0)),
            pl.BlockSpec((N_EXPERTS, D_MODEL), lambda i: (0, 0)),
            pl.BlockSpec((tm, tm), lambda i: (0, 0)),
        ],
        out_specs=[
            pl.BlockSpec((tm, D_MODEL), lambda i: (i, 0)),
            pl.BlockSpec((tm, D_MODEL // 2), lambda i: (i, 0)),
            pl.BlockSpec((2, tm), lambda i: (0, i)),
            pl.BlockSpec((2, tm), lambda i: (0, i)),
            pl.BlockSpec((2, tm), lambda i: (0, i)),
            pl.BlockSpec((N_EXPERTS, LANES), lambda i: (0, 0)),
        ],
        out_shape=[
            jax.ShapeDtypeStruct((t, D_MODEL), F32),
            jax.ShapeDtypeStruct((t, D_MODEL // 2), U32),
            two(I32), two(I32), two(F32),
            jax.ShapeDtypeStruct((N_EXPERTS, LANES), I32),
        ],
        scratch_shapes=[pltpu.VMEM((N_EXPERTS, LANES), F32)],
        compiler_params=_cparams(("arbitrary",), 48),
        name="moe_router",
    )(x2d, o2d, w_out, g, w_router.astype(F32).T, tri)


def _dispatch_body(d1_ref, d2_ref, hp_ref, init_ref, xs_ref, sem, *, rows):
    del init_ref

    def issue(k, c):
        r0 = pl.multiple_of(k * SUBLANES, SUBLANES)
        for u in range(SUBLANES):
            src = hp_ref.at[pl.ds(r0 + u, 1)]
            pltpu.make_async_copy(src, xs_ref.at[pl.ds(d1_ref[0, 0, r0 + u], 1)], sem.at[0]).start()
            pltpu.make_async_copy(src, xs_ref.at[pl.ds(d2_ref[0, 0, r0 + u], 1)], sem.at[1]).start()
        return c
    lax.fori_loop(0, rows // SUBLANES, issue, 0)
    pltpu.make_async_copy(hp_ref, xs_ref.at[pl.ds(0, rows)], sem.at[0]).wait()
    pltpu.make_async_copy(hp_ref, xs_ref.at[pl.ds(0, rows)], sem.at[1]).wait()


def _dispatch(hp, dest, n_rows):
    t, width = hp.shape
    rows = TM_DISPATCH
    idx_spec = pl.BlockSpec((1, 1, rows), lambda i: (i, 0, 0), memory_space=pltpu.SMEM)
    return pl.pallas_call(
        functools.partial(_dispatch_body, rows=rows),
        grid=(t // rows,),
        in_specs=[idx_spec, idx_spec, pl.BlockSpec((rows, width), lambda i: (i, 0)),
                  pl.BlockSpec(memory_space=pl.ANY)],
        out_specs=pl.BlockSpec(memory_space=pl.ANY),
        out_shape=jax.ShapeDtypeStruct((n_rows, width), hp.dtype),
        input_output_aliases={3: 0},
        scratch_shapes=[pltpu.SemaphoreType.DMA((2,))],
        compiler_params=_cparams(("arbitrary",), 32),
        name="moe_dispatch",
    )(dest[0].reshape(t // rows, 1, rows), dest[1].reshape(t // rows, 1, rows), hp,
      jnp.zeros((n_rows, width), hp.dtype))


def _expert_body(te_ref, nu_ref, xs_ref, wg_ref, wu_ref, wd_ref, o_ref, xb_ref, acc_ref):
    i = pl.program_id(0)
    j = pl.program_id(1)
    half = D_MODEL // 2

    @pl.when(j == 0)
    def _():
        w = xs_ref[...]
        xb_ref[:, :half] = lax.bitcast_convert_type(w << 16, F32).astype(BF16)
        xb_ref[:, half:] = lax.bitcast_convert_type(w & jnp.uint32(0xFFFF0000), F32).astype(BF16)
        acc_ref[...] = jnp.zeros_like(acc_ref)

    @pl.when(i < nu_ref[0])
    def _():
        xb = xb_ref[...]
        a = (jax.nn.silu(_dot(xb, wg_ref[...])) * _dot(xb, wu_ref[...])).astype(BF16)
        acc_ref[...] += _dot(a, wd_ref[...])

    @pl.when(j == pl.num_programs(1) - 1)
    def _():
        o_ref[...] = acc_ref[...]


def _experts(xs, tile_expert, n_used, wg, wu, wd):
    p = xs.shape[0]
    tm, tf = TM_MOE, TF_MOE
    grid_spec = pltpu.PrefetchScalarGridSpec(
        num_scalar_prefetch=2,
        grid=(p // tm, D_FF_EXPERT // tf),
        in_specs=[
            pl.BlockSpec((tm, D_MODEL // 2), lambda i, j, te, nu: (i, 0)),
            pl.BlockSpec((None, D_MODEL, tf), lambda i, j, te, nu: (te[i], 0, j)),
            pl.BlockSpec((None, D_MODEL, tf), lambda i, j, te, nu: (te[i], 0, j)),
            pl.BlockSpec((None, tf, D_MODEL), lambda i, j, te, nu: (te[i], j, 0)),
        ],
        out_specs=pl.BlockSpec((tm, D_MODEL), lambda i, j, te, nu: (i, 0)),
        scratch_shapes=[pltpu.VMEM((tm, D_MODEL), BF16), pltpu.VMEM((tm, D_MODEL), F32)],
    )
    return pl.pallas_call(
        _expert_body,
        grid_spec=grid_spec,
        out_shape=jax.ShapeDtypeStruct((p, D_MODEL), F32),
        compiler_params=_cparams(("parallel", "arbitrary"), 56),
        name="moe_experts",
    )(tile_expert, n_used, xs, wg, wu, wd)


def _combine_body(d1c_ref, d2c_ref, d1n_ref, d2n_ref, x_ref, g1_ref, g2_ref, ys_ref, oa_ref, ob_ref,
                  y1_ref, y2_ref, sem, *, rows, na_blocks):
    i = pl.program_id(0)
    slot = i % 2

    def gather(d1_ref, d2_ref, s):
        def issue(k, c):
            r0 = pl.multiple_of(k * SUBLANES, SUBLANES)
            for u in range(SUBLANES):
                pltpu.make_async_copy(ys_ref.at[pl.ds(d1_ref[0, 0, r0 + u], 1)], y1_ref.at[s, pl.ds(r0 + u, 1)],
                                      sem.at[0, s]).start()
                pltpu.make_async_copy(ys_ref.at[pl.ds(d2_ref[0, 0, r0 + u], 1)], y2_ref.at[s, pl.ds(r0 + u, 1)],
                                      sem.at[1, s]).start()
            return c
        lax.fori_loop(0, rows // SUBLANES, issue, 0)

    @pl.when(i == 0)
    def _():
        gather(d1c_ref, d2c_ref, 0)

    @pl.when(i + 1 < pl.num_programs(0))
    def _():
        gather(d1n_ref, d2n_ref, 1 - slot)

    pltpu.make_async_copy(ys_ref.at[pl.ds(0, rows)], y1_ref.at[slot], sem.at[0, slot]).wait()
    pltpu.make_async_copy(ys_ref.at[pl.ds(0, rows)], y2_ref.at[slot], sem.at[1, slot]).wait()
    val = x_ref[...] + g1_ref[...] * y1_ref[slot] + g2_ref[...] * y2_ref[slot]

    @pl.when(i < na_blocks)
    def _():
        oa_ref[...] = val

    @pl.when(i >= na_blocks)
    def _():
        ob_ref[...] = val


def _combine(x2d, ys, dest, gates, t_a):
    t = x2d.shape[0]
    rows = TM_COMBINE
    n = t // rows
    na = t_a // rows
    cur = pl.BlockSpec((1, 1, rows), lambda i: (i, 0, 0), memory_space=pltpu.SMEM)
    nxt = pl.BlockSpec((1, 1, rows), lambda i: (jnp.minimum(i + 1, n - 1), 0, 0), memory_space=pltpu.SMEM)
    gate_spec = pl.BlockSpec((rows, 1), lambda i: (i, 0))
    row_spec = pl.BlockSpec((rows, D_MODEL), lambda i: (i, 0))
    d1 = dest[0].reshape(n, 1, rows)
    d2 = dest[1].reshape(n, 1, rows)
    return pl.pallas_call(
        functools.partial(_combine_body, rows=rows, na_blocks=na),
        grid=(n,),
        in_specs=[cur, cur, nxt, nxt, row_spec, gate_spec, gate_spec, pl.BlockSpec(memory_space=pl.ANY)],
        out_specs=[pl.BlockSpec((rows, D_MODEL), lambda i: (jnp.minimum(i, na - 1), 0)),
                   pl.BlockSpec((rows, D_MODEL), lambda i: (jnp.maximum(i - na, 0), 0))],
        out_shape=[jax.ShapeDtypeStruct((t_a, D_MODEL), F32), jax.ShapeDtypeStruct((t - t_a, D_MODEL), F32)],
        scratch_shapes=[pltpu.VMEM((2, rows, D_MODEL), F32), pltpu.VMEM((2, rows, D_MODEL), F32),
                        pltpu.SemaphoreType.DMA((2, 2))],
        compiler_params=_cparams(("arbitrary",), 56),
        name="moe_combine",
    )(d1, d2, d1, d2, x2d, gates[0][:, None], gates[1][:, None], ys)


def _moe(x2d, o2d, w_out, g, w_router, wg, wu, wd, t_a):
    t = x2d.shape[0]
    tm = TM_MOE
    x2d, hp, idx, rank, gates, cnt = _router(x2d, o2d, w_out, g, w_router)
    counts = cnt[:, 0]
    tiles = (counts + tm - 1) // tm
    tile_end = jnp.cumsum(tiles)
    row_start = (tile_end - tiles) * tm
    n_tiles = (2 * t) // tm + N_EXPERTS
    eid = jnp.arange(N_EXPERTS, dtype=I32)[:, None, None]
    dest = jnp.sum(jnp.where(idx[None] == eid, row_start[:, None, None], 0), axis=0) + rank
    tile_expert = jnp.minimum(jnp.searchsorted(tile_end, jnp.arange(n_tiles), side="right"), N_EXPERTS - 1)
    xs = _dispatch(hp, dest, n_tiles * tm)
    ys = _experts(xs, tile_expert.astype(I32), tile_end[-1:].astype(I32), wg, wu, wd)
    return _combine(x2d, ys, dest, gates, t_a)


def _even_layer(xa, xb, norm_mix, norm_ffn, w_in, w_out, ln_g, ln_b, sgu_w, sgu_b, hy_conv,
                f_w1, f_b1, f_w2, f_b2, f_w3, f_b3, f_freq, hy_skip, wg, wu, wd):
    seq = xa.shape[1]
    bsz = xa.shape[0] + xb.shape[0]
    nh = seq // LANES
    sgu_bb = jnp.broadcast_to(sgu_b.astype(F32)[:, :, None], (A_GROUPS, CHUNK, LANES))
    w_in = w_in.astype(BF16)
    ya, hbt = _even_in(xa, xb, norm_mix[None, :], w_in[:, :2 * A_WIDTH], w_in[:, 2 * A_WIDTH:].T,
                       ln_g[None, :], ln_b[None, :], sgu_w.astype(BF16), sgu_bb)
    hbt = hbt.reshape(bsz, -1, nh, LANES)

    tb = _dft_tables(seq)
    kt = _hyena_kernels(seq, f_w1, f_b1, f_w2, f_b2, f_w3, f_b3, f_freq)
    kf = _filter_spectrum(kt.reshape(HYENA_ORDER, B_WIDTH, 2 * nh, LANES), tb)

    taps = hy_conv.astype(F32)
    skip = hy_skip.astype(F32)
    z1 = _hyena_conv(hbt, 0, hbt, B_WIDTH, taps, skip[0], kf, 0, tb, conv_z=True)
    ybt = _hyena_conv(z1, 0, hbt, 2 * B_WIDTH, taps, skip[1], kf, 1, tb, conv_z=False)

    w_out = w_out.astype(BF16)
    x2d = _mix_ffn(xa, xb, ya, ybt.reshape(bsz, B_WIDTH, seq), w_out[:A_WIDTH], w_out[A_WIDTH:],
                   norm_ffn[None, :], wg.astype(BF16), wu.astype(BF16), wd.astype(BF16))
    return x2d.reshape(bsz, seq, D_MODEL)


def _odd_layer(x, n_a, norm_mix, norm_ffn, w_qkv, q_g, k_g, sink, w_out, rel_bias, w_router, wg, wu, wd):
    bsz, seq, _ = x.shape
    q, k2, v2 = _qkv_proj(x, norm_mix[None, :], w_qkv, q_g, k_g)
    o = _attention(q, k2, v2, sink, rel_bias)
    ya, yb = _moe(x.reshape(bsz * seq, D_MODEL), o.reshape(bsz * seq, -1), w_out.astype(BF16), norm_ffn[None, :],
                  w_router, wg.astype(BF16), wu.astype(BF16), wd.astype(BF16), n_a * seq)
    return ya.reshape(n_a, seq, D_MODEL), yb.reshape(bsz - n_a, seq, D_MODEL)


def kernel(x_prompt, x_sample, norm_mix, norm_ffn, ev_w_in, ev_w_out, sgu_ln_g, sgu_ln_b, sgu_w, sgu_b,
           hy_conv, hy_f_w1, hy_f_b1, hy_f_w2, hy_f_b2, hy_f_w3, hy_f_b3, hy_f_freq, hy_skip,
           ffn_w_gate, ffn_w_up, ffn_w_down, at_w_qkv, at_q_norm, at_k_norm, at_sink, at_w_out,
           rel_bias, moe_router, moe_w_gate, moe_w_up, moe_w_down):
    even_p = (ev_w_in, ev_w_out, sgu_ln_g, sgu_ln_b, sgu_w, sgu_b, hy_conv, hy_f_w1, hy_f_b1, hy_f_w2,
              hy_f_b2, hy_f_w3, hy_f_b3, hy_f_freq, hy_skip, ffn_w_gate, ffn_w_up, ffn_w_down)
    odd_p = (at_w_qkv, at_q_norm, at_k_norm, at_sink, at_w_out, moe_router, moe_w_gate, moe_w_up, moe_w_down)
    assert x_prompt.shape[1:] == x_sample.shape[1:]
    n_a = x_prompt.shape[0]
    depth = norm_mix.shape[0]
    assert depth % 2 == 0, "layers come in (even, odd) pairs"
    xa, xb = x_prompt, x_sample
    for i in range(0, depth, 2):
        j = i // 2
        x = _even_layer(xa, xb, norm_mix[i], norm_ffn[i], *[p[j] for p in even_p])
        xa, xb = _odd_layer(x, n_a, norm_mix[i + 1], norm_ffn[i + 1], *[p[j] for p in odd_p[:5]], rel_bias,
                            *[p[j] for p in odd_p[5:]])
    return (xa, xb)
```
